```python
import functools
import jax, jax.numpy as jnp
from jax import lax
import numpy as np

D_MODEL = 2048
BATCH = 1
SEQ = 16384
DEPTH = 1
DEC_BATCH = 128
DEC_SEQ = 1
PAST_LEN = 16384
PAGE_SIZE = 128

N_META = 16
LRU_WIDTH = 1024
N_LRU_BLOCKS = 8
LRU_BLOCK = LRU_WIDTH // N_LRU_BLOCKS
CONV_W = 4
LRU_C = 8.0
N_HEADS = 16
N_KV_HEADS = 4
HEAD_DIM = 64
GROUP = N_HEADS // N_KV_HEADS
ATTN_WIDTH = N_HEADS * HEAD_DIM
KV_WIDTH = N_KV_HEADS * HEAD_DIM
WINDOW = 128
BLOCK = 128
D_FF = 4 * D_MODEL
EPS = 1e-6
NEG_INF = -1e30
SPLIT_IDX = (LRU_WIDTH, 2 * LRU_WIDTH, 2 * LRU_WIDTH + ATTN_WIDTH, 2 * LRU_WIDTH + ATTN_WIDTH + KV_WIDTH, 2 * LRU_WIDTH + ATTN_WIDTH + 2 * KV_WIDTH, 2 * LRU_WIDTH + ATTN_WIDTH + 2 * KV_WIDTH + D_MODEL)
IN_WIDTH = SPLIT_IDX[-1] + D_MODEL

kernel_name = "griffin_swa_sink_alibi_meta_decode_step"


def rms_norm(x, g):
    xf = x.astype(jnp.float32)
    y = xf * lax.rsqrt(jnp.mean(xf * xf, axis=-1, keepdims=True) + EPS) * g.astype(jnp.float32)
    return y.astype(x.dtype)


def alibi_slopes():
    return 2.0 ** (-8.0 * jnp.arange(1, N_HEADS + 1, dtype=jnp.float32) / N_HEADS)


def sink_attention(q, k, v, dist, valid, sinks):
    qf = q.astype(jnp.float32) * (HEAD_DIM ** -0.5)
    s = jnp.einsum('...qhgd,...khd->...hgqk', qf, k.astype(jnp.float32))
    slopes = alibi_slopes().reshape(N_KV_HEADS, GROUP, 1, 1)
    s = jnp.where(valid, s - slopes * dist, NEG_INF)
    sink = sinks.astype(jnp.float32).reshape(N_KV_HEADS, GROUP, 1, 1)
    m = jnp.maximum(jnp.max(s, axis=-1, keepdims=True), sink)
    e = jnp.exp(s - m)
    p = e / (jnp.sum(e, axis=-1, keepdims=True) + jnp.exp(sink - m))
    o = jnp.einsum('...hgqk,...khd->...qhgd', p, v.astype(jnp.float32))
    return o.astype(q.dtype)


def prompt_attention(q, k, v, sinks, cache_len):
    B, T = q.shape[0], q.shape[1]
    pad = (-T) % BLOCK
    Tp = T + pad
    nb = Tp // BLOCK
    qb = jnp.pad(q, ((0, 0), (pad, 0), (0, 0), (0, 0), (0, 0))).reshape(B, nb, BLOCK, N_KV_HEADS, GROUP, HEAD_DIM)

    def band(a):
        ext = jnp.pad(a, ((0, 0), (pad + BLOCK, 0), (0, 0), (0, 0)))
        prev = ext[:, :Tp].reshape(B, nb, BLOCK, N_KV_HEADS, HEAD_DIM)
        cur = ext[:, BLOCK:].reshape(B, nb, BLOCK, N_KV_HEADS, HEAD_DIM)
        meta = jnp.broadcast_to(a[:, None, :N_META], (B, nb, N_META, N_KV_HEADS, HEAD_DIM))
        return jnp.concatenate([meta, prev, cur], axis=2)

    kb, vb = band(k), band(v)
    r = jnp.arange(BLOCK)[:, None]
    c = jnp.arange(2 * BLOCK)[None, :]
    blk = jnp.arange(nb)[:, None, None]
    q_pos = blk * BLOCK + r[None] - pad
    k_pos = (blk - 1) * BLOCK + c[None] - pad
    d = q_pos - k_pos
    band_valid = (d >= 0) & (d <= WINDOW) & (k_pos >= N_META)
    meta_valid = q_pos >= jnp.arange(N_META)
    valid = jnp.concatenate([meta_valid, band_valid], axis=-1)[:, None, None]
    dist = jnp.concatenate([jnp.zeros((BLOCK, N_META), jnp.int32), r + BLOCK - c], axis=-1).astype(jnp.float32)
    o = sink_attention(qb, kb, vb, dist, valid, sinks)
    o = o.reshape(B, Tp, ATTN_WIDTH)[:, pad:]
    return o, (k[:, :N_META], v[:, :N_META], k[:, T - cache_len:], v[:, T - cache_len:])


def sample_attention(q, k, v, meta_k, meta_v, win_k, win_v, sinks):
    B, S = q.shape[0], q.shape[1]
    Lb = win_k.shape[1]
    kw = jnp.concatenate([win_k.astype(k.dtype), k], axis=1)
    vw = jnp.concatenate([win_v.astype(v.dtype), v], axis=1)
    kc = jnp.concatenate([meta_k.astype(k.dtype), kw], axis=1)
    vc = jnp.concatenate([meta_v.astype(v.dtype), vw], axis=1)
    i = jnp.arange(S)[:, None]
    j = jnp.arange(Lb + S)[None, :]
    d = i + Lb - j
    k_pos = PAST_LEN - Lb + j
    band_valid = (d >= 0) & (d <= WINDOW) & (k_pos >= N_META)
    valid = jnp.concatenate([jnp.ones((S, N_META), bool), band_valid], axis=-1)
    dist = jnp.concatenate([jnp.zeros((S, N_META), jnp.int32), d], axis=-1).astype(jnp.float32)
    o = sink_attention(q, kc, vc, dist, valid, sinks).reshape(B, S, ATTN_WIDTH)
    return o, (kw[:, -Lb:], vw[:, -Lb:])


def causal_conv(x_ext, w, b):
    L = x_ext.shape[1] - (CONV_W - 1)
    y = x_ext[:, 0:L] * w[0]
    for t in range(1, CONV_W):
        y = y + x_ext[:, t:t + L] * w[t]
    return y + b


def block_diag(x, w, b):
    B, L = x.shape[0], x.shape[1]
    y = jnp.einsum('blni,nij->blnj', x.reshape(B, L, N_LRU_BLOCKS, LRU_BLOCK), w)
    return y.reshape(B, L, LRU_WIDTH) + b


def rg_lru(xc, h0, w_gx, b_gx, w_ga, b_ga, a_param):
    gate_x = jax.nn.sigmoid(block_diag(xc, w_gx, b_gx)).astype(jnp.float32)
    gate_a = jax.nn.sigmoid(block_diag(xc, w_ga, b_ga)).astype(jnp.float32)
    log_a = -LRU_C * gate_a * jax.nn.softplus(-a_param.astype(jnp.float32))
    a = jnp.exp(log_a)
    b = jnp.sqrt(-jnp.expm1(2.0 * log_a)) * gate_x * xc.astype(jnp.float32)

    def step(h, ab):
        a_t, b_t = ab
        h = a_t * h + b_t
        return h, h

    h_last, hs = lax.scan(step, h0.astype(jnp.float32), (jnp.swapaxes(a, 0, 1), jnp.swapaxes(b, 0, 1)))
    return jnp.swapaxes(hs, 0, 1).astype(xc.dtype), h_last.astype(h0.dtype)


def trunk_layer(x, conv_prev, h0, attend, norm1_g, w_in, conv_w, conv_b, w_gate_x, b_gate_x, w_gate_a, b_gate_a,
                lru_a_param, w_lru_out, w_attn_out, w_o, norm2_g, w_mlp_up, w_mlp_down):
    B, L = x.shape[0], x.shape[1]
    xn = rms_norm(x, norm1_g)
    proj = xn @ w_in
    xb, yb, q, k, v, g_lru, g_attn = jnp.split(proj, SPLIT_IDX, axis=-1)
    x_ext = jnp.concatenate([conv_prev.astype(xb.dtype), xb], axis=1)
    xc = causal_conv(x_ext, conv_w, conv_b)
    hs, h_last = rg_lru(xc, h0, w_gate_x, b_gate_x, w_gate_a, b_gate_a, lru_a_param)
    lru_out = jax.nn.gelu(yb) * hs
    q = q.reshape(B, L, N_KV_HEADS, GROUP, HEAD_DIM)
    k = k.reshape(B, L, N_KV_HEADS, HEAD_DIM)
    v = v.reshape(B, L, N_KV_HEADS, HEAD_DIM)
    attn_out, attn_state = attend(q, k, v)
    merged = jax.nn.sigmoid(g_lru) * (lru_out @ w_lru_out) + jax.nn.sigmoid(g_attn) * (attn_out @ w_attn_out)
    h = x + merged @ w_o
    u = jnp.square(jax.nn.relu(rms_norm(h, norm2_g) @ w_mlp_up))
    out = h + u @ w_mlp_down
    return out, x_ext[:, -(CONV_W - 1):], h_last, attn_state


def setup_inputs(seed: int = 0) -> dict:
    key = jax.random.key(seed)
    ks = jax.random.split(key, 32)
    cw = min(WINDOW, PAST_LEN)

    def nrm(k, shape, scale):
        return jax.random.normal(k, shape, jnp.float32) * scale

    u = jax.random.uniform(ks[14], (DEPTH, LRU_WIDTH), jnp.float32, 0.9, 0.999)
    s = u ** (1.0 / LRU_C)
    lru_a_param = jnp.log(s) - jnp.log1p(-s)
    return {
        "x_prompt": nrm(ks[0], (BATCH, SEQ, D_MODEL), 1.0),
        "x_sample": nrm(ks[1], (DEC_BATCH, DEC_SEQ, D_MODEL), 1.0),
        "cache_meta_k": nrm(ks[2], (DEPTH, DEC_BATCH, N_META, N_KV_HEADS, HEAD_DIM), 1.0),
        "cache_meta_v": nrm(ks[3], (DEPTH, DEC_BATCH, N_META, N_KV_HEADS, HEAD_DIM), 1.0),
        "cache_win_k": nrm(ks[4], (DEPTH, DEC_BATCH, cw, N_KV_HEADS, HEAD_DIM), 1.0),
        "cache_win_v": nrm(ks[5], (DEPTH, DEC_BATCH, cw, N_KV_HEADS, HEAD_DIM), 1.0),
        "state_conv": nrm(ks[6], (DEPTH, DEC_BATCH, CONV_W - 1, LRU_WIDTH), 1.0),
        "state_h": nrm(ks[7], (DEPTH, DEC_BATCH, LRU_WIDTH), 0.5),
        "meta_tokens": nrm(ks[8], (N_META, D_MODEL), 1.0),
        "norm1_g": 1.0 + nrm(ks[9], (DEPTH, D_MODEL), 0.02),
        "w_in": nrm(ks[10], (DEPTH, D_MODEL, IN_WIDTH), D_MODEL ** -0.5),
        "conv_w": nrm(ks[11], (DEPTH, CONV_W, LRU_WIDTH), CONV_W ** -0.5),
        "conv_b": nrm(ks[12], (DEPTH, LRU_WIDTH), 0.02),
        "w_gate_x": nrm(ks[13], (DEPTH, N_LRU_BLOCKS, LRU_BLOCK, LRU_BLOCK), LRU_BLOCK ** -0.5),
        "b_gate_x": nrm(ks[15], (DEPTH, LRU_WIDTH), 0.02),
        "w_gate_a": nrm(ks[16], (DEPTH, N_LRU_BLOCKS, LRU_BLOCK, LRU_BLOCK), LRU_BLOCK ** -0.5),
        "b_gate_a": nrm(ks[17], (DEPTH, LRU_WIDTH), 0.02),
        "lru_a_param": lru_a_param,
        "attn_sinks": nrm(ks[18], (DEPTH, N_HEADS), 0.5),
        "w_lru_out": nrm(ks[19], (DEPTH, LRU_WIDTH, D_MODEL), LRU_WIDTH ** -0.5),
        "w_attn_out": nrm(ks[20], (DEPTH, ATTN_WIDTH, D_MODEL), ATTN_WIDTH ** -0.5),
        "w_o": nrm(ks[21], (DEPTH, D_MODEL, D_MODEL), D_MODEL ** -0.5),
        "norm2_g": 1.0 + nrm(ks[22], (DEPTH, D_MODEL), 0.02),
        "w_mlp_up": nrm(ks[23], (DEPTH, D_MODEL, D_FF), D_MODEL ** -0.5),
        "w_mlp_down": nrm(ks[24], (DEPTH, D_FF, D_MODEL), D_FF ** -0.5),
        "final_norm_g": 1.0 + nrm(ks[25], (D_MODEL,), 0.02),
    }


def reference(x_prompt, x_sample, cache_meta_k, cache_meta_v, cache_win_k, cache_win_v, state_conv, state_h,
              meta_tokens, norm1_g, w_in, conv_w, conv_b, w_gate_x, b_gate_x, w_gate_a, b_gate_a, lru_a_param,
              attn_sinks, w_lru_out, w_attn_out, w_o, norm2_g, w_mlp_up, w_mlp_down, final_norm_g):
    B = x_prompt.shape[0]
    cw = min(WINDOW, PAST_LEN)
    xp = jnp.concatenate([jnp.broadcast_to(meta_tokens.astype(x_prompt.dtype)[None], (B, N_META, D_MODEL)), x_prompt], axis=1)
    xs = x_sample
    p_mk, p_mv, p_wk, p_wv, p_conv, p_h = [], [], [], [], [], []
    s_wk, s_wv, s_conv, s_h = [], [], [], []
    for l in range(DEPTH):
        lp = (norm1_g[l], w_in[l], conv_w[l], conv_b[l], w_gate_x[l], b_gate_x[l], w_gate_a[l], b_gate_a[l],
              lru_a_param[l], w_lru_out[l], w_attn_out[l], w_o[l], norm2_g[l], w_mlp_up[l], w_mlp_down[l])
        attend_p = functools.partial(prompt_attention, sinks=attn_sinks[l], cache_len=cw)
        conv0 = jnp.zeros((B, CONV_W - 1, LRU_WIDTH), xp.dtype)
        h0 = jnp.zeros((B, LRU_WIDTH), state_h.dtype)
        xp, conv_p, h_p, (mk, mv, wk, wv) = trunk_layer(xp, conv0, h0, attend_p, *lp)
        p_mk.append(mk)
        p_mv.append(mv)
        p_wk.append(wk)
        p_wv.append(wv)
        p_conv.append(conv_p)
        p_h.append(h_p)
        attend_s = functools.partial(sample_attention, meta_k=cache_meta_k[l], meta_v=cache_meta_v[l],
                                     win_k=cache_win_k[l], win_v=cache_win_v[l], sinks=attn_sinks[l])
        xs, conv_s, h_s, (wk_s, wv_s) = trunk_layer(xs, state_conv[l], state_h[l], attend_s, *lp)
        s_wk.append(wk_s)
        s_wv.append(wv_s)
        s_conv.append(conv_s)
        s_h.append(h_s)
    y_prompt = rms_norm(xp, final_norm_g)[:, N_META:]
    y_sample = rms_norm(xs, final_norm_g)
    return (y_prompt, y_sample, jnp.stack(p_mk), jnp.stack(p_mv), jnp.stack(p_wk), jnp.stack(p_wv), jnp.stack(p_conv), jnp.stack(p_h), jnp.stack(s_wk), jnp.stack(s_wv), jnp.stack(s_conv), jnp.stack(s_h))
```

```python
import functools

import numpy as np
import jax
import jax.numpy as jnp
from jax import lax
from jax.experimental import pallas as pl
from jax.experimental.pallas import tpu as pltpu

D_MODEL = 2048
SEQ = 16384
DEC_BATCH = 128
N_META = 16
LRU_WIDTH = 1024
N_LRU_BLOCKS = 8
LRU_BLOCK = LRU_WIDTH // N_LRU_BLOCKS
CONV_W = 4
LRU_C = 8.0
N_HEADS = 16
N_KV_HEADS = 4
HEAD_DIM = 64
GROUP = N_HEADS // N_KV_HEADS
ATTN_WIDTH = N_HEADS * HEAD_DIM
KV_WIDTH = N_KV_HEADS * HEAD_DIM
WINDOW = 128
BLOCK = 128
D_FF = 4 * D_MODEL
EPS = 1e-6
NEG_INF = -1e30
IN_WIDTH = 2 * LRU_WIDTH + ATTN_WIDTH + 2 * KV_WIDTH + 2 * D_MODEL
REF_COL_GL = 2 * LRU_WIDTH + ATTN_WIDTH + 2 * KV_WIDTH
COL_GL, COL_GA, COL_XB, COL_YB, COL_Q, COL_K, COL_V = 0, 2048, 4096, 5120, 6144, 7168, 7424

EXTRA_ROWS = 256
META_ROW0 = EXTRA_ROWS - N_META
SUBLANES = 8
LANES = 128
MIB = 1024 * 1024

F32 = jnp.float32
BF16 = jnp.bfloat16


def _slopes():
    return [2.0 ** (-8.0 * (h + 1) / N_HEADS) for h in range(N_HEADS)]


def _cparams(sem, vmem_mib):
    return pltpu.CompilerParams(dimension_semantics=sem, vmem_limit_bytes=int(vmem_mib * MIB))


def _inproj_kernel(x_ref, g_ref, w_ref, o_ref, xs_ref):
    @pl.when(pl.program_id(1) == 0)
    def _():
        x = x_ref[...]
        ms = jnp.mean(x * x, axis=-1, keepdims=True)
        xs_ref[...] = (x * lax.rsqrt(ms + EPS) * g_ref[...]).astype(BF16)

    o_ref[...] = jnp.dot(xs_ref[...], w_ref[...], preferred_element_type=F32)


def _inproj(x, g, w, tm, tn):
    m = x.shape[0]
    return pl.pallas_call(
        _inproj_kernel,
        grid=(m // tm, IN_WIDTH // tn),
        in_specs=[
            pl.BlockSpec((tm, D_MODEL), lambda i, j: (i, 0)),
            pl.BlockSpec((1, D_MODEL), lambda i, j: (0, 0)),
            pl.BlockSpec((D_MODEL, tn), lambda i, j: (0, j)),
        ],
        out_specs=pl.BlockSpec((tm, tn), lambda i, j: (i, j)),
        out_shape=jax.ShapeDtypeStruct((m, IN_WIDTH), F32),
        scratch_shapes=[pltpu.VMEM((tm, D_MODEL), BF16)],
        compiler_params=_cparams(("arbitrary", "arbitrary"), 48),
        name=f"inproj_{m}",
    )(x, g, w)


def _gate_ab(xc_n, n, wgx_ref, bgx_ref, wga_ref, bga_ref, ap_ref):
    cols = slice(n * LRU_BLOCK, (n + 1) * LRU_BLOCK)
    xcb = xc_n.astype(BF16)
    gx = jnp.dot(xcb, wgx_ref[n], preferred_element_type=F32) + bgx_ref[:, cols]
    ga = jnp.dot(xcb, wga_ref[n], preferred_element_type=F32) + bga_ref[:, cols]
    gate_x = jax.nn.sigmoid(gx)
    gate_a = jax.nn.sigmoid(ga)
    log_a = -LRU_C * gate_a * jax.nn.softplus(-ap_ref[:, cols])
    a = jnp.exp(log_a)
    b = jnp.sqrt(-jnp.tanh(log_a) * (a * a + 1.0)) * gate_x * xc_n
    return a, b


def _lru_rows(rows, ext_ref, a3_ref, b3_ref, h3_ref, xb, yb_ref, cw_ref, cb_ref,
              wgx_ref, bgx_ref, wga_ref, bga_ref, ap_ref, lru_ref, h_init, g_lo, g_hi):
    groups = rows // SUBLANES
    ext_ref[SUBLANES:SUBLANES + rows, :] = xb
    xc = ext_ref[5:5 + rows, :] * cw_ref[0:1, :]
    xc = xc + ext_ref[6:6 + rows, :] * cw_ref[1:2, :]
    xc = xc + ext_ref[7:7 + rows, :] * cw_ref[2:3, :]
    xc = xc + xb * cw_ref[3:4, :]
    xc = xc + cb_ref[...]
    for n in range(N_LRU_BLOCKS):
        a, b = _gate_ab(xc[:, n * LRU_BLOCK:(n + 1) * LRU_BLOCK], n, wgx_ref, bgx_ref, wga_ref, bga_ref, ap_ref)
        a3_ref[:, n * SUBLANES:(n + 1) * SUBLANES, :] = a.reshape(groups, SUBLANES, LANES)
        b3_ref[:, n * SUBLANES:(n + 1) * SUBLANES, :] = b.reshape(groups, SUBLANES, LANES)

    def group_step(i, h):
        for r in range(SUBLANES):
            step = pl.ds(r, N_LRU_BLOCKS, stride=SUBLANES)
            h = a3_ref[i, step, :] * h + b3_ref[i, step, :]
            h3_ref[i, step, :] = h
        return h

    h = lax.fori_loop(g_lo, g_hi, group_step, h_init)
    for n in range(N_LRU_BLOCKS):
        cols = slice(n * LRU_BLOCK, (n + 1) * LRU_BLOCK)
        hs = h3_ref[:, n * SUBLANES:(n + 1) * SUBLANES, :].reshape(rows, LANES)
        lru_ref[:, cols] = (jax.nn.gelu(yb_ref[:, cols]) * hs).astype(BF16)
    return h


def _lru_main_kernel(xb_ref, yb_ref, tail_ref, h0_ref, cw_ref, cb_ref, wgx_ref, bgx_ref, wga_ref, bga_ref,
                     ap_ref, lru_ref, hlast_ref, ext_ref, a3_ref, b3_ref, h3_ref, hstate_ref, *, rows):
    @pl.when(pl.program_id(0) == 0)
    def _():
        ext_ref[0:SUBLANES, :] = tail_ref[...]
        hstate_ref[...] = h0_ref[...]

    h = _lru_rows(rows, ext_ref, a3_ref, b3_ref, h3_ref, xb_ref[...], yb_ref, cw_ref, cb_ref,
                  wgx_ref, bgx_ref, wga_ref, bga_ref, ap_ref, lru_ref, hstate_ref[...], 0, rows // SUBLANES)
    hstate_ref[...] = h
    hlast_ref[...] = h
    ext_ref[0:SUBLANES, :] = ext_ref[rows:rows + SUBLANES, :]


def _lru_main(proj, tail, h0, cw, cb, wgx, bgx, wga, bga, ap, tt):
    m = proj.shape[0]
    const = lambda shape: pl.BlockSpec(shape, lambda i: (0,) * len(shape))
    scan_shape = (tt // SUBLANES, N_LRU_BLOCKS * SUBLANES, LANES)
    return pl.pallas_call(
        functools.partial(_lru_main_kernel, rows=tt),
        grid=(m // tt,),
        in_specs=[
            pl.BlockSpec((tt, LRU_WIDTH), lambda i: (i, COL_XB // LRU_WIDTH)),
            pl.BlockSpec((tt, LRU_WIDTH), lambda i: (i, COL_YB // LRU_WIDTH)),
            const((SUBLANES, LRU_WIDTH)),
            const((SUBLANES, LANES)),
            const((CONV_W, LRU_WIDTH)),
            const((1, LRU_WIDTH)),
            const((N_LRU_BLOCKS, LRU_BLOCK, LRU_BLOCK)),
            const((1, LRU_WIDTH)),
            const((N_LRU_BLOCKS, LRU_BLOCK, LRU_BLOCK)),
            const((1, LRU_WIDTH)),
            const((1, LRU_WIDTH)),
        ],
        out_specs=[
            pl.BlockSpec((tt, LRU_WIDTH), lambda i: (i, 0)),
            const((SUBLANES, LANES)),
        ],
        out_shape=[
            jax.ShapeDtypeStruct((m, LRU_WIDTH), BF16),
            jax.ShapeDtypeStruct((SUBLANES, LANES), F32),
        ],
        scratch_shapes=[
            pltpu.VMEM((tt + SUBLANES, LRU_WIDTH), F32),
            pltpu.VMEM(scan_shape, F32),
            pltpu.VMEM(scan_shape, F32),
            pltpu.VMEM(scan_shape, F32),
            pltpu.VMEM((SUBLANES, LANES), F32),
        ],
        compiler_params=_cparams(("arbitrary",), 32),
        name="lru_main",
    )(proj, proj, tail, h0, cw, cb, wgx, bgx, wga, bga, ap)


def _lru_extra_kernel(xb_ref, yb_ref, sconv_ref, sh_ref, cw_ref, cb_ref, wgx_ref, bgx_ref, wga_ref, bga_ref,
                      ap_ref, lru_ref, hs_ref, hmeta_ref, ext_ref, a3_ref, b3_ref, h3_ref):
    nb = DEC_BATCH
    xb = xb_ref[0:nb, :]
    xc = sconv_ref[:, 0:LRU_WIDTH] * cw_ref[0:1, :]
    xc = xc + sconv_ref[:, LRU_WIDTH:2 * LRU_WIDTH] * cw_ref[1:2, :]
    xc = xc + sconv_ref[:, 2 * LRU_WIDTH:3 * LRU_WIDTH] * cw_ref[2:3, :]
    xc = xc + xb * cw_ref[3:4, :]
    xc = xc + cb_ref[...]
    for n in range(N_LRU_BLOCKS):
        cols = slice(n * LRU_BLOCK, (n + 1) * LRU_BLOCK)
        a, b = _gate_ab(xc[:, cols], n, wgx_ref, bgx_ref, wga_ref, bga_ref, ap_ref)
        h = a * sh_ref[:, cols] + b
        hs_ref[:, cols] = h
        lru_ref[0:nb, cols] = (jax.nn.gelu(yb_ref[0:nb, cols]) * h).astype(BF16)
    rows = EXTRA_ROWS - nb
    ext_ref[0:SUBLANES, :] = jnp.zeros((SUBLANES, LRU_WIDTH), F32)
    h3_ref[...] = jnp.zeros(h3_ref.shape, F32)
    first_group = (rows - N_META) // SUBLANES
    hmeta_ref[...] = _lru_rows(rows, ext_ref, a3_ref, b3_ref, h3_ref, xb_ref[nb:, :], yb_ref.at[nb:, :], cw_ref,
                               cb_ref, wgx_ref, bgx_ref, wga_ref, bga_ref, ap_ref, lru_ref.at[nb:, :],
                               jnp.zeros((SUBLANES, LANES), F32), first_group, rows // SUBLANES)


def _lru_extra(proj_e, sconv, sh, cw, cb, wgx, bgx, wga, bga, ap):
    const = lambda shape: pl.BlockSpec(shape, lambda i: (0,) * len(shape))
    rows = EXTRA_ROWS - DEC_BATCH
    scan_shape = (rows // SUBLANES, N_LRU_BLOCKS * SUBLANES, LANES)
    return pl.pallas_call(
        _lru_extra_kernel,
        grid=(1,),
        in_specs=[
            pl.BlockSpec((EXTRA_ROWS, LRU_WIDTH), lambda i: (0, COL_XB // LRU_WIDTH)),
            pl.BlockSpec((EXTRA_ROWS, LRU_WIDTH), lambda i: (0, COL_YB // LRU_WIDTH)),
            const((DEC_BATCH, (CONV_W - 1) * LRU_WIDTH)),
            const((DEC_BATCH, LRU_WIDTH)),
            const((CONV_W, LRU_WIDTH)),
            const((1, LRU_WIDTH)),
            const((N_LRU_BLOCKS, LRU_BLOCK, LRU_BLOCK)),
            const((1, LRU_WIDTH)),
            const((N_LRU_BLOCKS, LRU_BLOCK, LRU_BLOCK)),
            const((1, LRU_WIDTH)),
            const((1, LRU_WIDTH)),
        ],
        out_specs=[
            const((EXTRA_ROWS, LRU_WIDTH)),
            const((DEC_BATCH, LRU_WIDTH)),
            const((SUBLANES, LANES)),
        ],
        out_shape=[
            jax.ShapeDtypeStruct((EXTRA_ROWS, LRU_WIDTH), BF16),
            jax.ShapeDtypeStruct((DEC_BATCH, LRU_WIDTH), F32),
            jax.ShapeDtypeStruct((SUBLANES, LANES), F32),
        ],
        scratch_shapes=[
            pltpu.VMEM((rows + SUBLANES, LRU_WIDTH), F32),
            pltpu.VMEM(scan_shape, F32),
            pltpu.VMEM(scan_shape, F32),
            pltpu.VMEM(scan_shape, F32),
        ],
        compiler_params=_cparams(("arbitrary",), 32),
        name="lru_extra",
    )(proj_e, proj_e, sconv, sh, cw, cb, wgx, bgx, wga, bga, ap)


def _attend_kv_head(q_h, kk, vv, dist, valid, slopes, sinks):
    r = q_h.shape[0]
    qs = jnp.concatenate([q_h[:, g * HEAD_DIM:(g + 1) * HEAD_DIM] for g in range(GROUP)], axis=0)
    qs = (qs * (HEAD_DIM ** -0.5)).astype(BF16)
    s_all = lax.dot_general(qs, kk, (((1,), (1,)), ((), ())), preferred_element_type=F32)
    es, invs = [], []
    for g in range(GROUP):
        s = s_all[g * r:(g + 1) * r, :]
        s = jnp.where(valid, s - slopes[g] * dist, NEG_INF)
        m = jnp.maximum(jnp.max(s, axis=-1, keepdims=True), sinks[g])
        e = jnp.exp(s - m)
        denom = jnp.sum(e, axis=-1, keepdims=True) + jnp.exp(sinks[g] - m)
        es.append(e.astype(BF16))
        invs.append(denom)
    o_all = jnp.dot(jnp.concatenate(es, axis=0), vv, preferred_element_type=F32)
    return jnp.concatenate([o_all[g * r:(g + 1) * r, :] / invs[g] for g in range(GROUP)], axis=-1)


def _attn_main_kernel(sink_ref, q_ref, kc_ref, kp_ref, vc_ref, vp_ref, km_ref, vm_ref, o_ref):
    pb = pl.program_id(0)
    nk = N_META + 2 * BLOCK
    row = lax.broadcasted_iota(jnp.int32, (BLOCK, nk), 0)
    col = lax.broadcasted_iota(jnp.int32, (BLOCK, nk), 1)
    d = row + BLOCK - (col - N_META)
    band_ok = (d >= 0) & (d <= WINDOW) & ((col >= N_META + BLOCK) | (pb > 0))
    valid = (col < N_META) | band_ok
    dist = jnp.where(col < N_META, 0, d).astype(F32)
    slopes = _slopes()
    for h in range(N_KV_HEADS):
        hc = slice(h * HEAD_DIM, (h + 1) * HEAD_DIM)
        kk = jnp.concatenate([km_ref[:, hc], kp_ref[:, hc], kc_ref[:, hc]], axis=0).astype(BF16)
        vv = jnp.concatenate([vm_ref[:, hc], vp_ref[:, hc], vc_ref[:, hc]], axis=0).astype(BF16)
        q_h = q_ref[:, h * GROUP * HEAD_DIM:(h + 1) * GROUP * HEAD_DIM]
        o = _attend_kv_head(q_h, kk, vv, dist, valid, slopes[h * GROUP:(h + 1) * GROUP],
                            [sink_ref[h * GROUP + g] for g in range(GROUP)])
        o_ref[:, h * GROUP * HEAD_DIM:(h + 1) * GROUP * HEAD_DIM] = o.astype(BF16)


def _attn_main(sinks, proj, k_meta, v_meta):
    m = proj.shape[0]
    kv = lambda col: (pl.BlockSpec((BLOCK, KV_WIDTH), lambda i: (i, col // KV_WIDTH)),
                      pl.BlockSpec((BLOCK, KV_WIDTH), lambda i: (jnp.maximum(i - 1, 0), col // KV_WIDTH)))
    kc, kp = kv(COL_K)
    vc, vp = kv(COL_V)
    return pl.pallas_call(
        _attn_main_kernel,
        grid=(m // BLOCK,),
        in_specs=[
            pl.BlockSpec(memory_space=pltpu.SMEM),
            pl.BlockSpec((BLOCK, ATTN_WIDTH), lambda i: (i, COL_Q // ATTN_WIDTH)),
            kc, kp, vc, vp,
            pl.BlockSpec((N_META, KV_WIDTH), lambda i: (0, 0)),
            pl.BlockSpec((N_META, KV_WIDTH), lambda i: (0, 0)),
        ],
        out_specs=pl.BlockSpec((BLOCK, ATTN_WIDTH), lambda i: (i, 0)),
        out_shape=jax.ShapeDtypeStruct((m, ATTN_WIDTH), BF16),
        compiler_params=_cparams(("arbitrary",), 32),
        name="attn_main",
    )(sinks, proj, proj, proj, proj, proj, k_meta, v_meta)


def _attn_meta_kernel(sink_ref, q_ref, km_ref, vm_ref, o_ref):
    row = lax.broadcasted_iota(jnp.int32, (N_META, N_META), 0)
    col = lax.broadcasted_iota(jnp.int32, (N_META, N_META), 1)
    valid = row >= col
    dist = jnp.zeros((N_META, N_META), F32)
    slopes = _slopes()
    for h in range(N_KV_HEADS):
        hc = slice(h * HEAD_DIM, (h + 1) * HEAD_DIM)
        q_h = q_ref[:, h * GROUP * HEAD_DIM:(h + 1) * GROUP * HEAD_DIM]
        o = _attend_kv_head(q_h, km_ref[:, hc].astype(BF16), vm_ref[:, hc].astype(BF16), dist, valid,
                            slopes[h * GROUP:(h + 1) * GROUP], [sink_ref[h * GROUP + g] for g in range(GROUP)])
        o_ref[:, h * GROUP * HEAD_DIM:(h + 1) * GROUP * HEAD_DIM] = o


def _attn_meta(sinks, q_meta, k_meta, v_meta):
    const = lambda shape: pl.BlockSpec(shape, lambda i: (0,) * len(shape))
    return pl.pallas_call(
        _attn_meta_kernel,
        grid=(1,),
        in_specs=[pl.BlockSpec(memory_space=pltpu.SMEM), const((N_META, ATTN_WIDTH)),
                  const((N_META, KV_WIDTH)), const((N_META, KV_WIDTH))],
        out_specs=const((N_META, ATTN_WIDTH)),
        out_shape=jax.ShapeDtypeStruct((N_META, ATTN_WIDTH), F32),
        compiler_params=_cparams(("arbitrary",), 16),
        name="attn_meta",
    )(sinks, q_meta, k_meta, v_meta)


SAMPLE_KEYS = N_META + WINDOW + 1


def _attn_sample_kernel(q_ref, k_ref, v_ref, bias_ref, sink_ref, o_ref):
    for h in range(N_KV_HEADS):
        hc = slice(h * HEAD_DIM, (h + 1) * HEAD_DIM)
        q = (q_ref[:, h] * (HEAD_DIM ** -0.5)).astype(BF16)
        k = k_ref[:, :, hc].astype(BF16)
        v = v_ref[:, :, hc].astype(BF16)
        s = jnp.einsum("bgd,bjd->bgj", q, k, preferred_element_type=F32) + bias_ref[h]
        sink = sink_ref[h]
        m = jnp.maximum(jnp.max(s, axis=-1, keepdims=True), sink)
        e = jnp.exp(s - m)
        denom = jnp.sum(e, axis=-1, keepdims=True) + jnp.exp(sink - m)
        o = jnp.einsum("bgj,bjd->bgd", e.astype(BF16), v, preferred_element_type=F32)
        o_ref[:, h] = o / denom


def _attn_sample(q4, keys, vals, bias, sinks4, bb):
    return pl.pallas_call(
        _attn_sample_kernel,
        grid=(DEC_BATCH // bb,),
        in_specs=[
            pl.BlockSpec((bb, N_KV_HEADS, GROUP, HEAD_DIM), lambda i: (i, 0, 0, 0)),
            pl.BlockSpec((bb, SAMPLE_KEYS, KV_WIDTH), lambda i: (i, 0, 0)),
            pl.BlockSpec((bb, SAMPLE_KEYS, KV_WIDTH), lambda i: (i, 0, 0)),
            pl.BlockSpec((N_KV_HEADS, GROUP, SAMPLE_KEYS), lambda i: (0, 0, 0)),
            pl.BlockSpec((N_KV_HEADS, GROUP, 1), lambda i: (0, 0, 0)),
        ],
        out_specs=pl.BlockSpec((bb, N_KV_HEADS, GROUP, HEAD_DIM), lambda i: (i, 0, 0, 0)),
        out_shape=jax.ShapeDtypeStruct((DEC_BATCH, N_KV_HEADS, GROUP, HEAD_DIM), F32),
        compiler_params=_cparams(("arbitrary",), 32),
        name="attn_sample",
    )(q4, keys, vals, bias, sinks4)


def _outproj_kernel(lru_ref, attn_ref, gl_ref, ga_ref, x_ref, wl_ref, wa_ref, wo_ref, o_ref):
    l = jnp.dot(lru_ref[...], wl_ref[...], preferred_element_type=F32)
    a = jnp.dot(attn_ref[...], wa_ref[...], preferred_element_type=F32)
    merged = jax.nn.sigmoid(gl_ref[...]) * l + jax.nn.sigmoid(ga_ref[...]) * a
    o_ref[...] = x_ref[...] + jnp.dot(merged.astype(BF16), wo_ref[...], preferred_element_type=F32)


def _outproj(lru, attn, proj, x, wl, wa, wo, tm):
    m = x.shape[0]
    const = lambda shape: pl.BlockSpec(shape, lambda i: (0,) * len(shape))
    return pl.pallas_call(
        _outproj_kernel,
        grid=(m // tm,),
        in_specs=[
            pl.BlockSpec((tm, LRU_WIDTH), lambda i: (i, 0)),
            pl.BlockSpec((tm, ATTN_WIDTH), lambda i: (i, 0)),
            pl.BlockSpec((tm, D_MODEL), lambda i: (i, COL_GL // D_MODEL)),
            pl.BlockSpec((tm, D_MODEL), lambda i: (i, COL_GA // D_MODEL)),
            pl.BlockSpec((tm, D_MODEL), lambda i: (i, 0)),
            const((LRU_WIDTH, D_MODEL)),
            const((ATTN_WIDTH, D_MODEL)),
            const((D_MODEL, D_MODEL)),
        ],
        out_specs=pl.BlockSpec((tm, D_MODEL), lambda i: (i, 0)),
        out_shape=jax.ShapeDtypeStruct((m, D_MODEL), F32),
        compiler_params=_cparams(("arbitrary",), 56),
        name=f"outproj_{m}",
    )(lru, attn, proj, proj, x, wl, wa, wo)


def _mlp_kernel(h_ref, g2_ref, gf_ref, wu_ref, wd_ref, o_ref, hn_ref):
    f = pl.program_id(1)

    @pl.when(f == 0)
    def _():
        h = h_ref[...]
        ms = jnp.mean(h * h, axis=-1, keepdims=True)
        hn_ref[...] = (h * lax.rsqrt(ms + EPS) * g2_ref[...]).astype(BF16)
        o_ref[...] = jnp.zeros(o_ref.shape, F32)

    u = jnp.dot(hn_ref[...], wu_ref[...], preferred_element_type=F32)
    u = jnp.square(jnp.maximum(u, 0.0))
    o_ref[...] += jnp.dot(u.astype(BF16), wd_ref[...], preferred_element_type=F32)

    @pl.when(f == pl.num_programs(1) - 1)
    def _():
        out = h_ref[...] + o_ref[...]
        ms = jnp.mean(out * out, axis=-1, keepdims=True)
        o_ref[...] = out * lax.rsqrt(ms + EPS) * gf_ref[...]


def _mlp(h, g2, gf, wu, wd, tm, tf):
    m = h.shape[0]
    return pl.pallas_call(
        _mlp_kernel,
        grid=(m // tm, D_FF // tf),
        in_specs=[
            pl.BlockSpec((tm, D_MODEL), lambda i, f: (i, 0)),
            pl.BlockSpec((1, D_MODEL), lambda i, f: (0, 0)),
            pl.BlockSpec((1, D_MODEL), lambda i, f: (0, 0)),
            pl.BlockSpec((D_MODEL, tf), lambda i, f: (0, f)),
            pl.BlockSpec((tf, D_MODEL), lambda i, f: (f, 0)),
        ],
        out_specs=pl.BlockSpec((tm, D_MODEL), lambda i, f: (i, 0)),
        out_shape=jax.ShapeDtypeStruct((m, D_MODEL), F32),
        scratch_shapes=[pltpu.VMEM((tm, D_MODEL), BF16)],
        compiler_params=_cparams(("arbitrary", "arbitrary"), 48),
        name=f"mlp_{m}",
    )(h, g2, gf, wu, wd)


def _sample_bias():
    dist = np.concatenate([np.zeros(N_META), WINDOW - np.arange(WINDOW), np.zeros(1)]).astype(np.float32)
    slopes = np.asarray(_slopes(), np.float32).reshape(N_KV_HEADS, GROUP, 1)
    return jnp.asarray(-slopes * dist[None, None, :])


def kernel(x_prompt, x_sample, cache_meta_k, cache_meta_v, cache_win_k, cache_win_v, state_conv, state_h,
           meta_tokens, norm1_g, w_in, conv_w, conv_b, w_gate_x, b_gate_x, w_gate_a, b_gate_a, lru_a_param,
           attn_sinks, w_lru_out, w_attn_out, w_o, norm2_g, w_mlp_up, w_mlp_down, final_norm_g):
    row = lambda v: v.reshape(1, -1)
    g1, g2, gf = row(norm1_g[0]), row(norm2_g[0]), row(final_norm_g)
    w_in_b = jnp.concatenate([w_in[0][:, REF_COL_GL:], w_in[0][:, :REF_COL_GL]], axis=1).astype(BF16)
    wgx, wga = w_gate_x[0].astype(BF16), w_gate_a[0].astype(BF16)
    wl, wa, wo = w_lru_out[0].astype(BF16), w_attn_out[0].astype(BF16), w_o[0].astype(BF16)
    wu, wd = w_mlp_up[0].astype(BF16), w_mlp_down[0].astype(BF16)
    cw, cb = conv_w[0], row(conv_b[0])
    bgx, bga, ap = row(b_gate_x[0]), row(b_gate_a[0]), row(lru_a_param[0])
    sinks = attn_sinks[0]

    x_main = x_prompt.reshape(SEQ, D_MODEL)
    x_extra = jnp.concatenate([
        x_sample.reshape(DEC_BATCH, D_MODEL),
        jnp.zeros((EXTRA_ROWS - DEC_BATCH - N_META, D_MODEL), F32),
        meta_tokens.astype(F32)], axis=0)

    proj_m = _inproj(x_main, g1, w_in_b, tm=512, tn=1280)
    proj_e = _inproj(x_extra, g1, w_in_b, tm=EXTRA_ROWS, tn=1280)

    lru_e, h_sample, h_meta = _lru_extra(proj_e, state_conv[0].reshape(DEC_BATCH, (CONV_W - 1) * LRU_WIDTH),
                                         state_h[0], cw, cb, wgx, bgx, wga, bga, ap)
    tail = proj_e[EXTRA_ROWS - SUBLANES:, COL_XB:COL_XB + LRU_WIDTH]
    lru_m, h_last = _lru_main(proj_m, tail, h_meta, cw, cb, wgx, bgx, wga, bga, ap, tt=256)

    k_meta = proj_e[META_ROW0:, COL_K:COL_K + KV_WIDTH]
    v_meta = proj_e[META_ROW0:, COL_V:COL_V + KV_WIDTH]
    attn_m = _attn_main(sinks, proj_m, k_meta, v_meta)
    attn_meta = _attn_meta(sinks, proj_e[META_ROW0:, COL_Q:COL_Q + ATTN_WIDTH], k_meta, v_meta)
    k_new = proj_e[:DEC_BATCH, COL_K:COL_K + KV_WIDTH]
    v_new = proj_e[:DEC_BATCH, COL_V:COL_V + KV_WIDTH]
    kw = jnp.concatenate([cache_win_k[0].reshape(DEC_BATCH, WINDOW, KV_WIDTH), k_new[:, None, :]], axis=1)
    vw = jnp.concatenate([cache_win_v[0].reshape(DEC_BATCH, WINDOW, KV_WIDTH), v_new[:, None, :]], axis=1)
    keys = jnp.concatenate([cache_meta_k[0].reshape(DEC_BATCH, N_META, KV_WIDTH), kw], axis=1)
    vals = jnp.concatenate([cache_meta_v[0].reshape(DEC_BATCH, N_META, KV_WIDTH), vw], axis=1)
    q4 = proj_e[:DEC_BATCH, COL_Q:COL_Q + ATTN_WIDTH].reshape(DEC_BATCH, N_KV_HEADS, GROUP, HEAD_DIM)
    attn_s = _attn_sample(q4, keys, vals, _sample_bias(), sinks.reshape(N_KV_HEADS, GROUP, 1), bb=8)
    attn_e = jnp.concatenate([
        attn_s.reshape(DEC_BATCH, ATTN_WIDTH),
        jnp.zeros((EXTRA_ROWS - DEC_BATCH - N_META, ATTN_WIDTH), F32),
        attn_meta], axis=0).astype(BF16)

    res_m = _outproj(lru_m, attn_m, proj_m, x_main, wl, wa, wo, tm=256)
    res_e = _outproj(lru_e, attn_e, proj_e, x_extra, wl, wa, wo, tm=EXTRA_ROWS)
    y_m = _mlp(res_m, g2, gf, wu, wd, tm=512, tf=512)
    y_e = _mlp(res_e, g2, gf, wu, wd, tm=EXTRA_ROWS, tf=512)

    kv5 = lambda a, n: a.reshape(1, -1, n, N_KV_HEADS, HEAD_DIM)
    return (
        y_m.reshape(1, SEQ, D_MODEL),
        y_e[:DEC_BATCH].reshape(DEC_BATCH, 1, D_MODEL),
        kv5(k_meta, N_META), kv5(v_meta, N_META),
        kv5(proj_m[SEQ - WINDOW:, COL_K:COL_K + KV_WIDTH], WINDOW),
        kv5(proj_m[SEQ - WINDOW:, COL_V:COL_V + KV_WIDTH], WINDOW),
        proj_m[SEQ - (CONV_W - 1):, COL_XB:COL_XB + LRU_WIDTH].reshape(1, 1, CONV_W - 1, LRU_WIDTH),
        h_last.reshape(1, 1, LRU_WIDTH),
        kv5(kw[:, 1:], WINDOW), kv5(vw[:, 1:], WINDOW),
        jnp.concatenate([state_conv[0][:, 1:], proj_e[:DEC_BATCH, None, COL_XB:COL_XB + LRU_WIDTH]], axis=1)[None],
        h_sample[None],
    )
```

```python
import functools

import numpy as np
import jax
import jax.numpy as jnp
from jax import lax
from jax.experimental import pallas as pl
from jax.experimental.pallas import tpu as pltpu

D_MODEL = 2048
SEQ = 16384
DEC_BATCH = 128
N_META = 16
LRU_WIDTH = 1024
N_LRU_BLOCKS = 8
LRU_BLOCK = LRU_WIDTH // N_LRU_BLOCKS
CONV_W = 4
LRU_C = 8.0
N_HEADS = 16
N_KV_HEADS = 4
HEAD_DIM = 64
GROUP = N_HEADS // N_KV_HEADS
ATTN_WIDTH = N_HEADS * HEAD_DIM
KV_WIDTH = N_KV_HEADS * HEAD_DIM
WINDOW = 128
BLOCK = 128
D_FF = 4 * D_MODEL
EPS = 1e-6
NEG_INF = -1e30
IN_WIDTH = 2 * LRU_WIDTH + ATTN_WIDTH + 2 * KV_WIDTH + 2 * D_MODEL
REF_COL_GL = 2 * LRU_WIDTH + ATTN_WIDTH + 2 * KV_WIDTH
COL_GL, COL_GA, COL_XB, COL_YB, COL_Q, COL_K, COL_V = 0, 2048, 4096, 5120, 6144, 7168, 7424

EXTRA_ROWS = 256
META_ROW0 = EXTRA_ROWS - N_META
SUBLANES = 8
LANES = 128
MIB = 1024 * 1024

F32 = jnp.float32
BF16 = jnp.bfloat16


def _slopes():
    return [2.0 ** (-8.0 * (h + 1) / N_HEADS) for h in range(N_HEADS)]


V7X_VMEM_REQUEST_CAP = 60 * MIB


def _nbytes(shape, dtype):
    return int(np.prod(shape)) * jnp.dtype(dtype).itemsize


def _cparams(sem, pipelined, resident=0):
    estimate = 2 * pipelined + resident
    limit = min(V7X_VMEM_REQUEST_CAP, estimate + estimate // 4 + 2 * MIB)
    return pltpu.CompilerParams(dimension_semantics=sem, vmem_limit_bytes=limit)


def _tile_plan(rows):
    if rows == EXTRA_ROWS:
        return dict(inproj=(EXTRA_ROWS, 1536), outproj=EXTRA_ROWS, mlp=(EXTRA_ROWS, 512))
    return dict(inproj=(1024, 1536), lru=256, outproj=256, mlp=(1024, 512))


def _inproj_kernel(x_ref, g_ref, w_ref, o_ref, xs_ref):
    @pl.when(pl.program_id(1) == 0)
    def _():
        x = x_ref[...]
        ms = jnp.mean(x * x, axis=-1, keepdims=True)
        xs_ref[...] = (x * lax.rsqrt(ms + EPS) * g_ref[...]).astype(BF16)

    o_ref[...] = jnp.dot(xs_ref[...], w_ref[...], preferred_element_type=F32)


def _inproj(x, g, w):
    m = x.shape[0]
    tm, tn = _tile_plan(m)["inproj"]
    windows = _nbytes((tm, D_MODEL), F32) + _nbytes((D_MODEL, tn), BF16) + _nbytes((tm, tn), F32)
    return pl.pallas_call(
        _inproj_kernel,
        grid=(m // tm, IN_WIDTH // tn),
        in_specs=[
            pl.BlockSpec((tm, D_MODEL), lambda i, j: (i, 0)),
            pl.BlockSpec((1, D_MODEL), lambda i, j: (0, 0)),
            pl.BlockSpec((D_MODEL, tn), lambda i, j: (0, j)),
        ],
        out_specs=pl.BlockSpec((tm, tn), lambda i, j: (i, j)),
        out_shape=jax.ShapeDtypeStruct((m, IN_WIDTH), F32),
        scratch_shapes=[pltpu.VMEM((tm, D_MODEL), BF16)],
        compiler_params=_cparams(("arbitrary", "arbitrary"), windows, _nbytes((tm, D_MODEL), BF16)),
        name=f"inproj_{m}",
    )(x, g, w)


def _gate_ab(xc_n, n, wgx_ref, bgx_ref, wga_ref, bga_ref, ap_ref):
    cols = slice(n * LRU_BLOCK, (n + 1) * LRU_BLOCK)
    xcb = xc_n.astype(BF16)
    gx = jnp.dot(xcb, wgx_ref[n], preferred_element_type=F32) + bgx_ref[:, cols]
    ga = jnp.dot(xcb, wga_ref[n], preferred_element_type=F32) + bga_ref[:, cols]
    gate_x = jax.nn.sigmoid(gx)
    gate_a = jax.nn.sigmoid(ga)
    log_a = -LRU_C * gate_a * jax.nn.softplus(-ap_ref[:, cols])
    a = jnp.exp(log_a)
    b = jnp.sqrt(-jnp.tanh(log_a) * (a * a + 1.0)) * gate_x * xc_n
    return a, b


def _lru_rows(rows, ext_ref, a3_ref, b3_ref, h3_ref, xb, yb_ref, cw_ref, cb_ref,
              wgx_ref, bgx_ref, wga_ref, bga_ref, ap_ref, lru_ref, h_init, g_lo, g_hi):
    groups = rows // SUBLANES
    ext_ref[SUBLANES:SUBLANES + rows, :] = xb
    first = SUBLANES - (CONV_W - 1)
    xc = ext_ref[first:first + rows, :] * cw_ref[0:1, :]
    for t in range(1, CONV_W - 1):
        xc = xc + ext_ref[first + t:first + t + rows, :] * cw_ref[t:t + 1, :]
    xc = xc + xb * cw_ref[CONV_W - 1:CONV_W, :]
    xc = xc + cb_ref[...]
    for n in range(N_LRU_BLOCKS):
        a, b = _gate_ab(xc[:, n * LRU_BLOCK:(n + 1) * LRU_BLOCK], n, wgx_ref, bgx_ref, wga_ref, bga_ref, ap_ref)
        a3_ref[:, n * SUBLANES:(n + 1) * SUBLANES, :] = a.reshape(groups, SUBLANES, LANES)
        b3_ref[:, n * SUBLANES:(n + 1) * SUBLANES, :] = b.reshape(groups, SUBLANES, LANES)

    def group_step(i, h):
        for r in range(SUBLANES):
            step = pl.ds(r, N_LRU_BLOCKS, stride=SUBLANES)
            h = a3_ref[i, step, :] * h + b3_ref[i, step, :]
            h3_ref[i, step, :] = h
        return h

    h = lax.fori_loop(g_lo, g_hi, group_step, h_init)
    for n in range(N_LRU_BLOCKS):
        cols = slice(n * LRU_BLOCK, (n + 1) * LRU_BLOCK)
        hs = h3_ref[:, n * SUBLANES:(n + 1) * SUBLANES, :].reshape(rows, LANES)
        lru_ref[:, cols] = (jax.nn.gelu(yb_ref[:, cols]) * hs).astype(BF16)
    return h


def _lru_main_kernel(xb_ref, yb_ref, tail_ref, h0_ref, cw_ref, cb_ref, wgx_ref, bgx_ref, wga_ref, bga_ref,
                     ap_ref, lru_ref, hlast_ref, ext_ref, a3_ref, b3_ref, h3_ref, hstate_ref, *, rows):
    @pl.when(pl.program_id(0) == 0)
    def _():
        ext_ref[0:SUBLANES, :] = tail_ref[...]
        hstate_ref[...] = h0_ref[...]

    h = _lru_rows(rows, ext_ref, a3_ref, b3_ref, h3_ref, xb_ref[...], yb_ref, cw_ref, cb_ref,
                  wgx_ref, bgx_ref, wga_ref, bga_ref, ap_ref, lru_ref, hstate_ref[...], 0, rows // SUBLANES)
    hstate_ref[...] = h
    hlast_ref[...] = h
    ext_ref[0:SUBLANES, :] = ext_ref[rows:rows + SUBLANES, :]


def _lru_main(proj, tail, h0, cw, cb, wgx, bgx, wga, bga, ap):
    m = proj.shape[0]
    tt = _tile_plan(m)["lru"]
    const = lambda shape: pl.BlockSpec(shape, lambda i: (0,) * len(shape))
    scan_shape = (tt // SUBLANES, N_LRU_BLOCKS * SUBLANES, LANES)
    tile = _nbytes((tt, LRU_WIDTH), F32)
    windows = 3 * tile + 2 * _nbytes(wgx.shape, BF16)
    return pl.pallas_call(
        functools.partial(_lru_main_kernel, rows=tt),
        grid=(m // tt,),
        in_specs=[
            pl.BlockSpec((tt, LRU_WIDTH), lambda i: (i, COL_XB // LRU_WIDTH)),
            pl.BlockSpec((tt, LRU_WIDTH), lambda i: (i, COL_YB // LRU_WIDTH)),
            const((SUBLANES, LRU_WIDTH)),
            const((SUBLANES, LANES)),
            const((CONV_W, LRU_WIDTH)),
            const((1, LRU_WIDTH)),
            const((N_LRU_BLOCKS, LRU_BLOCK, LRU_BLOCK)),
            const((1, LRU_WIDTH)),
            const((N_LRU_BLOCKS, LRU_BLOCK, LRU_BLOCK)),
            const((1, LRU_WIDTH)),
            const((1, LRU_WIDTH)),
        ],
        out_specs=[
            pl.BlockSpec((tt, LRU_WIDTH), lambda i: (i, 0)),
            const((SUBLANES, LANES)),
        ],
        out_shape=[
            jax.ShapeDtypeStruct((m, LRU_WIDTH), BF16),
            jax.ShapeDtypeStruct((SUBLANES, LANES), F32),
        ],
        scratch_shapes=[
            pltpu.VMEM((tt + SUBLANES, LRU_WIDTH), F32),
            pltpu.VMEM(scan_shape, F32),
            pltpu.VMEM(scan_shape, F32),
            pltpu.VMEM(scan_shape, F32),
            pltpu.VMEM((SUBLANES, LANES), F32),
        ],
        compiler_params=_cparams(("arbitrary",), windows, 8 * tile),
        name="lru_main",
    )(proj, proj, tail, h0, cw, cb, wgx, bgx, wga, bga, ap)


def _lru_extra_kernel(xb_ref, yb_ref, sconv_ref, sh_ref, cw_ref, cb_ref, wgx_ref, bgx_ref, wga_ref, bga_ref,
                      ap_ref, lru_ref, hs_ref, hmeta_ref, ext_ref, a3_ref, b3_ref, h3_ref):
    nb = DEC_BATCH
    xb = xb_ref[0:nb, :]
    xc = sconv_ref[:, 0:LRU_WIDTH] * cw_ref[0:1, :]
    xc = xc + sconv_ref[:, LRU_WIDTH:2 * LRU_WIDTH] * cw_ref[1:2, :]
    xc = xc + sconv_ref[:, 2 * LRU_WIDTH:3 * LRU_WIDTH] * cw_ref[2:3, :]
    xc = xc + xb * cw_ref[3:4, :]
    xc = xc + cb_ref[...]
    for n in range(N_LRU_BLOCKS):
        cols = slice(n * LRU_BLOCK, (n + 1) * LRU_BLOCK)
        a, b = _gate_ab(xc[:, cols], n, wgx_ref, bgx_ref, wga_ref, bga_ref, ap_ref)
        h = a * sh_ref[:, cols] + b
        hs_ref[:, cols] = h
        lru_ref[0:nb, cols] = (jax.nn.gelu(yb_ref[0:nb, cols]) * h).astype(BF16)
    rows = EXTRA_ROWS - nb
    ext_ref[0:SUBLANES, :] = jnp.zeros((SUBLANES, LRU_WIDTH), F32)
    h3_ref[...] = jnp.zeros(h3_ref.shape, F32)
    first_group = (rows - N_META) // SUBLANES
    hmeta_ref[...] = _lru_rows(rows, ext_ref, a3_ref, b3_ref, h3_ref, xb_ref[nb:, :], yb_ref.at[nb:, :], cw_ref,
                               cb_ref, wgx_ref, bgx_ref, wga_ref, bga_ref, ap_ref, lru_ref.at[nb:, :],
                               jnp.zeros((SUBLANES, LANES), F32), first_group, rows // SUBLANES)


def _lru_extra(proj_e, sconv, sh, cw, cb, wgx, bgx, wga, bga, ap):
    const = lambda shape: pl.BlockSpec(shape, lambda i: (0,) * len(shape))
    rows = EXTRA_ROWS - DEC_BATCH
    scan_shape = (rows // SUBLANES, N_LRU_BLOCKS * SUBLANES, LANES)
    return pl.pallas_call(
        _lru_extra_kernel,
        grid=(1,),
        in_specs=[
            pl.BlockSpec((EXTRA_ROWS, LRU_WIDTH), lambda i: (0, COL_XB // LRU_WIDTH)),
            pl.BlockSpec((EXTRA_ROWS, LRU_WIDTH), lambda i: (0, COL_YB // LRU_WIDTH)),
            const((DEC_BATCH, (CONV_W - 1) * LRU_WIDTH)),
            const((DEC_BATCH, LRU_WIDTH)),
            const((CONV_W, LRU_WIDTH)),
            const((1, LRU_WIDTH)),
            const((N_LRU_BLOCKS, LRU_BLOCK, LRU_BLOCK)),
            const((1, LRU_WIDTH)),
            const((N_LRU_BLOCKS, LRU_BLOCK, LRU_BLOCK)),
            const((1, LRU_WIDTH)),
            const((1, LRU_WIDTH)),
        ],
        out_specs=[
            const((EXTRA_ROWS, LRU_WIDTH)),
            const((DEC_BATCH, LRU_WIDTH)),
            const((SUBLANES, LANES)),
        ],
        out_shape=[
            jax.ShapeDtypeStruct((EXTRA_ROWS, LRU_WIDTH), BF16),
            jax.ShapeDtypeStruct((DEC_BATCH, LRU_WIDTH), F32),
            jax.ShapeDtypeStruct((SUBLANES, LANES), F32),
        ],
        scratch_shapes=[
            pltpu.VMEM((rows + SUBLANES, LRU_WIDTH), F32),
            pltpu.VMEM(scan_shape, F32),
            pltpu.VMEM(scan_shape, F32),
            pltpu.VMEM(scan_shape, F32),
        ],
        compiler_params=_cparams(("arbitrary",), 8 * _nbytes((DEC_BATCH, LRU_WIDTH), F32),
                                 8 * _nbytes((rows, LRU_WIDTH), F32)),
        name="lru_extra",
    )(proj_e, proj_e, sconv, sh, cw, cb, wgx, bgx, wga, bga, ap)


def _attend_kv_head(q_h, kk, vv, bias, sinks):
    r = q_h.shape[0]
    qs = jnp.concatenate([q_h[:, g * HEAD_DIM:(g + 1) * HEAD_DIM] for g in range(GROUP)], axis=0)
    qs = (qs * (HEAD_DIM ** -0.5)).astype(BF16)
    s = lax.dot_general(kk.astype(BF16), qs, (((1,), (1,)), ((), ())), preferred_element_type=F32) + bias
    sink = jnp.concatenate([jnp.full((1, r), sinks[g], F32) for g in range(GROUP)], axis=1)
    m = jnp.maximum(jnp.max(s, axis=0, keepdims=True), sink)
    e = jnp.exp(s - m)
    denom = jnp.sum(e, axis=0, keepdims=True) + jnp.exp(sink - m)
    o_t = jnp.dot(vv.T.astype(BF16), e.astype(BF16), preferred_element_type=F32) / denom
    pairs = [jnp.concatenate([o_t[:, g * r:(g + 1) * r], o_t[:, (g + 1) * r:(g + 2) * r]], axis=0).T
             for g in range(0, GROUP, 2)]
    return jnp.concatenate(pairs, axis=-1)


ATTN_KEYS = N_META + 2 * BLOCK


def _attn_main_kernel(sink_ref, q_ref, kc_ref, kp_ref, vc_ref, vp_ref, km_ref, vm_ref, o_ref, bias_ref):
    pb = pl.program_id(0)
    slopes = _slopes()

    @pl.when(pb <= 1)
    def _():
        key = lax.broadcasted_iota(jnp.int32, (ATTN_KEYS, BLOCK), 0)
        qry = lax.broadcasted_iota(jnp.int32, (ATTN_KEYS, BLOCK), 1)
        d = qry + BLOCK - (key - N_META)
        band_ok = (d >= 0) & (d <= WINDOW) & ((key >= N_META + BLOCK) | (pb > 0))
        valid = (key < N_META) | band_ok
        dist = jnp.where(key < N_META, 0, d).astype(F32)
        for hd in range(N_HEADS):
            lanes = slice((hd % GROUP) * BLOCK, (hd % GROUP + 1) * BLOCK)
            bias_ref[hd // GROUP, :, lanes] = jnp.where(valid, -(slopes[hd] * dist), NEG_INF)

    for h in range(N_KV_HEADS):
        hc = slice(h * HEAD_DIM, (h + 1) * HEAD_DIM)
        qc = slice(h * GROUP * HEAD_DIM, (h + 1) * GROUP * HEAD_DIM)
        kk = jnp.concatenate([km_ref[:, hc], kp_ref[:, hc], kc_ref[:, hc]], axis=0)
        vv = jnp.concatenate([vm_ref[:, hc], vp_ref[:, hc], vc_ref[:, hc]], axis=0)
        o = _attend_kv_head(q_ref[:, qc], kk, vv, bias_ref[h], [sink_ref[h * GROUP + g] for g in range(GROUP)])
        o_ref[:, qc] = o.astype(BF16)


def _attn_main(sinks, proj, k_meta, v_meta):
    m = proj.shape[0]
    kv = lambda col: (pl.BlockSpec((BLOCK, KV_WIDTH), lambda i: (i, col // KV_WIDTH)),
                      pl.BlockSpec((BLOCK, KV_WIDTH), lambda i: (jnp.maximum(i - 1, 0), col // KV_WIDTH)))
    kc, kp = kv(COL_K)
    vc, vp = kv(COL_V)
    return pl.pallas_call(
        _attn_main_kernel,
        grid=(m // BLOCK,),
        in_specs=[
            pl.BlockSpec(memory_space=pltpu.SMEM),
            pl.BlockSpec((BLOCK, ATTN_WIDTH), lambda i: (i, COL_Q // ATTN_WIDTH)),
            kc, kp, vc, vp,
            pl.BlockSpec((N_META, KV_WIDTH), lambda i: (0, 0)),
            pl.BlockSpec((N_META, KV_WIDTH), lambda i: (0, 0)),
        ],
        out_specs=pl.BlockSpec((BLOCK, ATTN_WIDTH), lambda i: (i, 0)),
        out_shape=jax.ShapeDtypeStruct((m, ATTN_WIDTH), BF16),
        scratch_shapes=[pltpu.VMEM((N_KV_HEADS, ATTN_KEYS, GROUP * BLOCK), F32)],
        compiler_params=_cparams(("arbitrary",), 3 * _nbytes((BLOCK, ATTN_WIDTH), F32),
                                 3 * _nbytes((N_KV_HEADS, ATTN_KEYS, GROUP * BLOCK), F32)),
        name="attn_main",
    )(sinks, proj, proj, proj, proj, proj, k_meta, v_meta)


def _attn_meta_kernel(sink_ref, q_ref, km_ref, vm_ref, o_ref):
    key = lax.broadcasted_iota(jnp.int32, (N_META, GROUP * N_META), 0)
    qry = lax.broadcasted_iota(jnp.int32, (N_META, GROUP * N_META), 1) % N_META
    bias = jnp.where(qry >= key, 0.0, NEG_INF)
    for h in range(N_KV_HEADS):
        hc = slice(h * HEAD_DIM, (h + 1) * HEAD_DIM)
        q_h = q_ref[:, h * GROUP * HEAD_DIM:(h + 1) * GROUP * HEAD_DIM]
        o = _attend_kv_head(q_h, km_ref[:, hc], vm_ref[:, hc], bias,
                            [sink_ref[h * GROUP + g] for g in range(GROUP)])
        o_ref[:, h * GROUP * HEAD_DIM:(h + 1) * GROUP * HEAD_DIM] = o


def _attn_meta(sinks, q_meta, k_meta, v_meta):
    const = lambda shape: pl.BlockSpec(shape, lambda i: (0,) * len(shape))
    return pl.pallas_call(
        _attn_meta_kernel,
        grid=(1,),
        in_specs=[pl.BlockSpec(memory_space=pltpu.SMEM), const((N_META, ATTN_WIDTH)),
                  const((N_META, KV_WIDTH)), const((N_META, KV_WIDTH))],
        out_specs=const((N_META, ATTN_WIDTH)),
        out_shape=jax.ShapeDtypeStruct((N_META, ATTN_WIDTH), F32),
        compiler_params=_cparams(("arbitrary",), 4 * _nbytes((N_META, ATTN_WIDTH), F32)),
        name="attn_meta",
    )(sinks, q_meta, k_meta, v_meta)


def _attn_sample_kernel(q_ref, km_ref, kw_ref, kn_ref, vm_ref, vw_ref, vn_ref, bias_ref, sink_ref,
                        o_ref, kwo_ref, vwo_ref):
    for src_ref, new_ref, dst_ref in ((kw_ref, kn_ref, kwo_ref), (vw_ref, vn_ref, vwo_ref)):
        dst_ref[:, 0:WINDOW - 1, :] = src_ref[:, 1:WINDOW, :]
        dst_ref[:, WINDOW - 1:WINDOW, :] = new_ref[...]
    dot_qk = functools.partial(jnp.einsum, "bgd,bjd->bgj", preferred_element_type=F32)
    dot_pv = functools.partial(jnp.einsum, "bgj,bjd->bgd", preferred_element_type=F32)
    for h in range(N_KV_HEADS):
        hc = slice(h * HEAD_DIM, (h + 1) * HEAD_DIM)
        q = (q_ref[:, h] * (HEAD_DIM ** -0.5)).astype(BF16)
        k_new = kn_ref[:, :, hc].astype(BF16).astype(F32)
        v_new = vn_ref[:, :, hc].astype(BF16).astype(F32)
        s_m = dot_qk(q, km_ref[:, :, hc].astype(BF16))
        s_w = dot_qk(q, kw_ref[:, :, hc].astype(BF16)) + bias_ref[h]
        s_n = jnp.sum(q.astype(F32) * k_new, axis=-1, keepdims=True)
        sink = sink_ref[h]
        m = jnp.maximum(jnp.maximum(jnp.max(s_m, axis=-1, keepdims=True), jnp.max(s_w, axis=-1, keepdims=True)),
                        jnp.maximum(s_n, sink))
        e_m, e_w, e_n = jnp.exp(s_m - m), jnp.exp(s_w - m), jnp.exp(s_n - m)
        denom = (jnp.sum(e_m, axis=-1, keepdims=True) + jnp.sum(e_w, axis=-1, keepdims=True) + e_n
                 + jnp.exp(sink - m))
        o = (dot_pv(e_m.astype(BF16), vm_ref[:, :, hc].astype(BF16))
             + dot_pv(e_w.astype(BF16), vw_ref[:, :, hc].astype(BF16))
             + e_n.astype(BF16).astype(F32) * v_new)
        o_ref[:, h] = o / denom


def _attn_sample(q4, k_meta, k_win, k_new, v_meta, v_win, v_new, bias, sinks4, bb):
    b3 = lambda rows: pl.BlockSpec((bb, rows, KV_WIDTH), lambda i: (i, 0, 0))
    q_spec = pl.BlockSpec((bb, N_KV_HEADS, GROUP, HEAD_DIM), lambda i: (i, 0, 0, 0))
    win_shape = jax.ShapeDtypeStruct((DEC_BATCH, WINDOW, KV_WIDTH), F32)
    return pl.pallas_call(
        _attn_sample_kernel,
        grid=(DEC_BATCH // bb,),
        in_specs=[
            q_spec,
            b3(N_META), b3(WINDOW), b3(1),
            b3(N_META), b3(WINDOW), b3(1),
            pl.BlockSpec((N_KV_HEADS, GROUP, WINDOW), lambda i: (0, 0, 0)),
            pl.BlockSpec((N_KV_HEADS, GROUP, 1), lambda i: (0, 0, 0)),
        ],
        out_specs=[q_spec, b3(WINDOW), b3(WINDOW)],
        out_shape=[jax.ShapeDtypeStruct((DEC_BATCH, N_KV_HEADS, GROUP, HEAD_DIM), F32), win_shape, win_shape],
        compiler_params=_cparams(("arbitrary",), 5 * _nbytes((bb, WINDOW, KV_WIDTH), F32),
                                 4 * _nbytes((bb, WINDOW, KV_WIDTH), F32)),
        name="attn_sample",
    )(q4, k_meta, k_win, k_new, v_meta, v_win, v_new, bias, sinks4)


def _outproj_kernel(lru_ref, attn_ref, gl_ref, ga_ref, x_ref, wl_ref, wa_ref, wo_ref, o_ref):
    l = jnp.dot(lru_ref[...], wl_ref[...], preferred_element_type=F32)
    a = jnp.dot(attn_ref[...], wa_ref[...], preferred_element_type=F32)
    merged = jax.nn.sigmoid(gl_ref[...]) * l + jax.nn.sigmoid(ga_ref[...]) * a
    o_ref[...] = x_ref[...] + jnp.dot(merged.astype(BF16), wo_ref[...], preferred_element_type=F32)


def _outproj(lru, attn, proj, x, wl, wa, wo):
    m = x.shape[0]
    tm = _tile_plan(m)["outproj"]
    const = lambda shape: pl.BlockSpec(shape, lambda i: (0,) * len(shape))
    tile = _nbytes((tm, D_MODEL), F32)
    windows = 5 * tile + _nbytes(wl.shape, BF16) + _nbytes(wa.shape, BF16) + _nbytes(wo.shape, BF16)
    return pl.pallas_call(
        _outproj_kernel,
        grid=(m // tm,),
        in_specs=[
            pl.BlockSpec((tm, LRU_WIDTH), lambda i: (i, 0)),
            pl.BlockSpec((tm, ATTN_WIDTH), lambda i: (i, 0)),
            pl.BlockSpec((tm, D_MODEL), lambda i: (i, COL_GL // D_MODEL)),
            pl.BlockSpec((tm, D_MODEL), lambda i: (i, COL_GA // D_MODEL)),
            pl.BlockSpec((tm, D_MODEL), lambda i: (i, 0)),
            const((LRU_WIDTH, D_MODEL)),
            const((ATTN_WIDTH, D_MODEL)),
            const((D_MODEL, D_MODEL)),
        ],
        out_specs=pl.BlockSpec((tm, D_MODEL), lambda i: (i, 0)),
        out_shape=jax.ShapeDtypeStruct((m, D_MODEL), F32),
        compiler_params=_cparams(("arbitrary",), windows, 3 * tile),
        name=f"outproj_{m}",
    )(lru, attn, proj, proj, x, wl, wa, wo)


def _mlp_kernel(h_ref, g2_ref, gf_ref, wu_ref, wd_ref, o_ref, hn_ref):
    f = pl.program_id(1)

    @pl.when(f == 0)
    def _():
        h = h_ref[...]
        ms = jnp.mean(h * h, axis=-1, keepdims=True)
        hn_ref[...] = (h * lax.rsqrt(ms + EPS) * g2_ref[...]).astype(BF16)
        o_ref[...] = jnp.zeros(o_ref.shape, F32)

    u = jnp.dot(hn_ref[...], wu_ref[...], preferred_element_type=F32)
    u = jnp.square(jnp.maximum(u, 0.0))
    o_ref[...] += jnp.dot(u.astype(BF16), wd_ref[...], preferred_element_type=F32)

    @pl.when(f == pl.num_programs(1) - 1)
    def _():
        out = h_ref[...] + o_ref[...]
        ms = jnp.mean(out * out, axis=-1, keepdims=True)
        o_ref[...] = out * lax.rsqrt(ms + EPS) * gf_ref[...]


def _mlp(h, g2, gf, wu, wd):
    m = h.shape[0]
    tm, tf = _tile_plan(m)["mlp"]
    windows = 2 * _nbytes((tm, D_MODEL), F32) + 2 * _nbytes((D_MODEL, tf), BF16)
    resident = _nbytes((tm, D_MODEL), BF16) + 2 * _nbytes((tm, tf), F32)
    return pl.pallas_call(
        _mlp_kernel,
        grid=(m // tm, D_FF // tf),
        in_specs=[
            pl.BlockSpec((tm, D_MODEL), lambda i, f: (i, 0)),
            pl.BlockSpec((1, D_MODEL), lambda i, f: (0, 0)),
            pl.BlockSpec((1, D_MODEL), lambda i, f: (0, 0)),
            pl.BlockSpec((D_MODEL, tf), lambda i, f: (0, f)),
            pl.BlockSpec((tf, D_MODEL), lambda i, f: (f, 0)),
        ],
        out_specs=pl.BlockSpec((tm, D_MODEL), lambda i, f: (i, 0)),
        out_shape=jax.ShapeDtypeStruct((m, D_MODEL), F32),
        scratch_shapes=[pltpu.VMEM((tm, D_MODEL), BF16)],
        compiler_params=_cparams(("arbitrary", "arbitrary"), windows, resident),
        name=f"mlp_{m}",
    )(h, g2, gf, wu, wd)


def _sample_bias():
    dist = (WINDOW - np.arange(WINDOW)).astype(np.float32)
    slopes = np.asarray(_slopes(), np.float32).reshape(N_KV_HEADS, GROUP, 1)
    return jnp.asarray(-(slopes * dist[None, None, :]))


def kernel(x_prompt, x_sample, cache_meta_k, cache_meta_v, cache_win_k, cache_win_v, state_conv, state_h,
           meta_tokens, norm1_g, w_in, conv_w, conv_b, w_gate_x, b_gate_x, w_gate_a, b_gate_a, lru_a_param,
           attn_sinks, w_lru_out, w_attn_out, w_o, norm2_g, w_mlp_up, w_mlp_down, final_norm_g):
    row = lambda v: v.reshape(1, -1)
    g1, g2, gf = row(norm1_g[0]), row(norm2_g[0]), row(final_norm_g)
    w_in_b = jnp.concatenate([w_in[0][:, REF_COL_GL:], w_in[0][:, :REF_COL_GL]], axis=1).astype(BF16)
    wgx, wga = w_gate_x[0].astype(BF16), w_gate_a[0].astype(BF16)
    wl, wa, wo = w_lru_out[0].astype(BF16), w_attn_out[0].astype(BF16), w_o[0].astype(BF16)
    wu, wd = w_mlp_up[0].astype(BF16), w_mlp_down[0].astype(BF16)
    cw, cb = conv_w[0], row(conv_b[0])
    bgx, bga, ap = row(b_gate_x[0]), row(b_gate_a[0]), row(lru_a_param[0])
    sinks = attn_sinks[0]

    x_main = x_prompt.reshape(SEQ, D_MODEL)
    x_extra = jnp.concatenate([
        x_sample.reshape(DEC_BATCH, D_MODEL),
        jnp.zeros((EXTRA_ROWS - DEC_BATCH - N_META, D_MODEL), F32),
        meta_tokens.astype(F32)], axis=0)

    proj_m = _inproj(x_main, g1, w_in_b)
    proj_e = _inproj(x_extra, g1, w_in_b)

    lru_e, h_sample, h_meta = _lru_extra(proj_e, state_conv[0].reshape(DEC_BATCH, (CONV_W - 1) * LRU_WIDTH),
                                         state_h[0], cw, cb, wgx, bgx, wga, bga, ap)
    tail = proj_e[EXTRA_ROWS - SUBLANES:, COL_XB:COL_XB + LRU_WIDTH]
    lru_m, h_last = _lru_main(proj_m, tail, h_meta, cw, cb, wgx, bgx, wga, bga, ap)

    k_meta = proj_e[META_ROW0:, COL_K:COL_K + KV_WIDTH]
    v_meta = proj_e[META_ROW0:, COL_V:COL_V + KV_WIDTH]
    attn_m = _attn_main(sinks, proj_m, k_meta, v_meta)
    attn_meta = _attn_meta(sinks, proj_e[META_ROW0:, COL_Q:COL_Q + ATTN_WIDTH], k_meta, v_meta)
    k_new = proj_e[:DEC_BATCH, None, COL_K:COL_K + KV_WIDTH]
    v_new = proj_e[:DEC_BATCH, None, COL_V:COL_V + KV_WIDTH]
    q4 = proj_e[:DEC_BATCH, COL_Q:COL_Q + ATTN_WIDTH].reshape(DEC_BATCH, N_KV_HEADS, GROUP, HEAD_DIM)
    flat = lambda c, n: c[0].reshape(DEC_BATCH, n, KV_WIDTH)
    attn_s, kw_out, vw_out = _attn_sample(
        q4, flat(cache_meta_k, N_META), flat(cache_win_k, WINDOW), k_new,
        flat(cache_meta_v, N_META), flat(cache_win_v, WINDOW), v_new,
        _sample_bias(), sinks.reshape(N_KV_HEADS, GROUP, 1), bb=8)
    attn_e = jnp.concatenate([
        attn_s.reshape(DEC_BATCH, ATTN_WIDTH),
        jnp.zeros((EXTRA_ROWS - DEC_BATCH - N_META, ATTN_WIDTH), F32),
        attn_meta], axis=0).astype(BF16)

    res_m = _outproj(lru_m, attn_m, proj_m, x_main, wl, wa, wo)
    res_e = _outproj(lru_e, attn_e, proj_e, x_extra, wl, wa, wo)
    y_m = _mlp(res_m, g2, gf, wu, wd)
    y_e = _mlp(res_e, g2, gf, wu, wd)

    kv5 = lambda a, n: a.reshape(1, -1, n, N_KV_HEADS, HEAD_DIM)
    return (
        y_m.reshape(1, SEQ, D_MODEL),
        y_e[:DEC_BATCH].reshape(DEC_BATCH, 1, D_MODEL),
        kv5(k_meta, N_META), kv5(v_meta, N_META),
        kv5(proj_m[SEQ - WINDOW:, COL_K:COL_K + KV_WIDTH], WINDOW),
        kv5(proj_m[SEQ - WINDOW:, COL_V:COL_V + KV_WIDTH], WINDOW),
        proj_m[SEQ - (CONV_W - 1):, COL_XB:COL_XB + LRU_WIDTH].reshape(1, 1, CONV_W - 1, LRU_WIDTH),
        h_last.reshape(1, 1, LRU_WIDTH),
        kv5(kw_out, WINDOW), kv5(vw_out, WINDOW),
        jnp.concatenate([state_conv[0][:, 1:], proj_e[:DEC_BATCH, None, COL_XB:COL_XB + LRU_WIDTH]], axis=1)[None],
        h_sample[None],
    )
```

```python
import functools

import numpy as np
import jax
import jax.numpy as jnp
from jax import lax
from jax.experimental import pallas as pl
from jax.experimental.pallas import tpu as pltpu

D_MODEL = 2048
SEQ = 16384
DEC_BATCH = 128
N_META = 16
LRU_WIDTH = 1024
N_LRU_BLOCKS = 8
LRU_BLOCK = LRU_WIDTH // N_LRU_BLOCKS
CONV_W = 4
LRU_C = 8.0
N_HEADS = 16
N_KV_HEADS = 4
HEAD_DIM = 64
GROUP = N_HEADS // N_KV_HEADS
ATTN_WIDTH = N_HEADS * HEAD_DIM
KV_WIDTH = N_KV_HEADS * HEAD_DIM
WINDOW = 128
BLOCK = 128
D_FF = 4 * D_MODEL
EPS = 1e-6
NEG_INF = -1e30
IN_WIDTH = 2 * LRU_WIDTH + ATTN_WIDTH + 2 * KV_WIDTH + 2 * D_MODEL
COL_XB, COL_YB, COL_Q = 0, LRU_WIDTH, 2 * LRU_WIDTH
COL_K = COL_Q + ATTN_WIDTH
COL_V = COL_K + KV_WIDTH
COL_GL = COL_V + KV_WIDTH
COL_GA = COL_GL + D_MODEL
GATE_CHUNK = 512

EXTRA_ROWS = 256
META_ROW0 = EXTRA_ROWS - N_META
SUBLANES = 8
LANES = 128
MIB = 1024 * 1024

F32 = jnp.float32
BF16 = jnp.bfloat16


def _slopes():
    return [2.0 ** (-8.0 * (h + 1) / N_HEADS) for h in range(N_HEADS)]


V7X_VMEM_REQUEST_CAP = 60 * MIB


def _nbytes(shape, dtype):
    return int(np.prod(shape)) * jnp.dtype(dtype).itemsize


def _cparams(sem, pipelined, resident=0):
    estimate = 2 * pipelined + resident
    limit = min(V7X_VMEM_REQUEST_CAP, estimate + estimate // 4 + 2 * MIB)
    return pltpu.CompilerParams(dimension_semantics=sem, vmem_limit_bytes=limit)


def _tile_plan(rows):
    if rows == EXTRA_ROWS:
        return dict(inproj=(EXTRA_ROWS, 1536), outproj=EXTRA_ROWS, mlp=(EXTRA_ROWS, 512))
    return dict(inproj=(1024, 1536), lru=512, outproj=256, mlp=(1024, 512))


def _inproj_kernel(x_ref, g_ref, w_ref, o_ref, xs_ref):
    @pl.when(pl.program_id(1) == 0)
    def _():
        x = x_ref[...]
        ms = jnp.mean(x * x, axis=-1, keepdims=True)
        xs_ref[...] = (x * lax.rsqrt(ms + EPS) * g_ref[...]).astype(BF16)

    o_ref[...] = jnp.dot(xs_ref[...], w_ref[...], preferred_element_type=F32)


def _inproj(x, g, w):
    m = x.shape[0]
    tm, tn = _tile_plan(m)["inproj"]
    windows = _nbytes((tm, D_MODEL), F32) + _nbytes((D_MODEL, tn), BF16) + _nbytes((tm, tn), F32)
    return pl.pallas_call(
        _inproj_kernel,
        grid=(m // tm, IN_WIDTH // tn),
        in_specs=[
            pl.BlockSpec((tm, D_MODEL), lambda i, j: (i, 0)),
            pl.BlockSpec((1, D_MODEL), lambda i, j: (0, 0)),
            pl.BlockSpec((D_MODEL, tn), lambda i, j: (0, j)),
        ],
        out_specs=pl.BlockSpec((tm, tn), lambda i, j: (i, j)),
        out_shape=jax.ShapeDtypeStruct((m, IN_WIDTH), F32),
        scratch_shapes=[pltpu.VMEM((tm, D_MODEL), BF16)],
        compiler_params=_cparams(("arbitrary", "arbitrary"), windows, _nbytes((tm, D_MODEL), BF16)),
        name=f"inproj_{m}",
    )(x, g, w)


def _gate_ab(xc_n, n, wgx_ref, bgx_ref, wga_ref, bga_ref, ap_ref):
    cols = slice(n * LRU_BLOCK, (n + 1) * LRU_BLOCK)
    xcb = xc_n.astype(BF16)
    gx = jnp.dot(xcb, wgx_ref[n], preferred_element_type=F32) + bgx_ref[:, cols]
    ga = jnp.dot(xcb, wga_ref[n], preferred_element_type=F32) + bga_ref[:, cols]
    gate_x = jax.nn.sigmoid(gx)
    gate_a = jax.nn.sigmoid(ga)
    log_a = -LRU_C * gate_a * jax.nn.softplus(-ap_ref[:, cols])
    a = jnp.exp(log_a)
    z = -jnp.tanh(log_a) * (a * a + 1.0)
    root = jnp.where(z > 0.0, z * lax.rsqrt(z), 0.0)
    b = root * gate_x * xc_n
    return a, b


def _lru_rows(rows, ext_ref, a3_ref, b3_ref, h3_ref, xb, yb_ref, cw_ref, cb_ref,
              wgx_ref, bgx_ref, wga_ref, bga_ref, ap_ref, lru_ref, h_init, g_lo, g_hi):
    groups = rows // SUBLANES
    ext_ref[SUBLANES:SUBLANES + rows, :] = xb
    x_ext = ext_ref[...]
    xc = x_ext * cw_ref[0:1, :]
    for t in range(1, CONV_W):
        xc = pltpu.roll(xc, 1, axis=0) + x_ext * cw_ref[t:t + 1, :]
    xc = xc[SUBLANES:, :] + cb_ref[...]
    for n in range(N_LRU_BLOCKS):
        a, b = _gate_ab(xc[:, n * LRU_BLOCK:(n + 1) * LRU_BLOCK], n, wgx_ref, bgx_ref, wga_ref, bga_ref, ap_ref)
        a3_ref[:, n * SUBLANES:(n + 1) * SUBLANES, :] = a.reshape(groups, SUBLANES, LANES)
        b3_ref[:, n * SUBLANES:(n + 1) * SUBLANES, :] = b.reshape(groups, SUBLANES, LANES)

    def group_step(i, h):
        for r in range(SUBLANES):
            step = pl.ds(r, N_LRU_BLOCKS, stride=SUBLANES)
            h = a3_ref[i, step, :] * h + b3_ref[i, step, :]
            h3_ref[i, step, :] = h
        return h

    h = lax.fori_loop(g_lo, g_hi, group_step, h_init)
    for n in range(N_LRU_BLOCKS):
        cols = slice(n * LRU_BLOCK, (n + 1) * LRU_BLOCK)
        hs = h3_ref[:, n * SUBLANES:(n + 1) * SUBLANES, :].reshape(rows, LANES)
        lru_ref[:, cols] = (jax.nn.gelu(yb_ref[:, cols]) * hs).astype(BF16)
    return h


def _lru_main_kernel(xb_ref, yb_ref, tail_ref, h0_ref, cw_ref, cb_ref, wgx_ref, bgx_ref, wga_ref, bga_ref,
                     ap_ref, lru_ref, hlast_ref, ext_ref, a3_ref, b3_ref, h3_ref, hstate_ref, *, rows):
    @pl.when(pl.program_id(0) == 0)
    def _():
        ext_ref[0:SUBLANES, :] = tail_ref[...]
        hstate_ref[...] = h0_ref[...]

    h = _lru_rows(rows, ext_ref, a3_ref, b3_ref, h3_ref, xb_ref[...], yb_ref, cw_ref, cb_ref,
                  wgx_ref, bgx_ref, wga_ref, bga_ref, ap_ref, lru_ref, hstate_ref[...], 0, rows // SUBLANES)
    hstate_ref[...] = h
    hlast_ref[...] = h
    ext_ref[0:SUBLANES, :] = ext_ref[rows:rows + SUBLANES, :]


def _lru_main(proj, tail, h0, cw, cb, wgx, bgx, wga, bga, ap):
    m = proj.shape[0]
    tt = _tile_plan(m)["lru"]
    const = lambda shape: pl.BlockSpec(shape, lambda i: (0,) * len(shape))
    scan_shape = (tt // SUBLANES, N_LRU_BLOCKS * SUBLANES, LANES)
    tile = _nbytes((tt, LRU_WIDTH), F32)
    windows = 3 * tile + 2 * _nbytes(wgx.shape, BF16)
    return pl.pallas_call(
        functools.partial(_lru_main_kernel, rows=tt),
        grid=(m // tt,),
        in_specs=[
            pl.BlockSpec((tt, LRU_WIDTH), lambda i: (i, COL_XB // LRU_WIDTH)),
            pl.BlockSpec((tt, LRU_WIDTH), lambda i: (i, COL_YB // LRU_WIDTH)),
            const((SUBLANES, LRU_WIDTH)),
            const((SUBLANES, LANES)),
            const((CONV_W, LRU_WIDTH)),
            const((1, LRU_WIDTH)),
            const((N_LRU_BLOCKS, LRU_BLOCK, LRU_BLOCK)),
            const((1, LRU_WIDTH)),
            const((N_LRU_BLOCKS, LRU_BLOCK, LRU_BLOCK)),
            const((1, LRU_WIDTH)),
            const((1, LRU_WIDTH)),
        ],
        out_specs=[
            pl.BlockSpec((tt, LRU_WIDTH), lambda i: (i, 0)),
            const((SUBLANES, LANES)),
        ],
        out_shape=[
            jax.ShapeDtypeStruct((m, LRU_WIDTH), BF16),
            jax.ShapeDtypeStruct((SUBLANES, LANES), F32),
        ],
        scratch_shapes=[
            pltpu.VMEM((tt + SUBLANES, LRU_WIDTH), F32),
            pltpu.VMEM(scan_shape, F32),
            pltpu.VMEM(scan_shape, F32),
            pltpu.VMEM(scan_shape, F32),
            pltpu.VMEM((SUBLANES, LANES), F32),
        ],
        compiler_params=_cparams(("arbitrary",), windows, 8 * tile),
        name="lru_main",
    )(proj, proj, tail, h0, cw, cb, wgx, bgx, wga, bga, ap)


def _lru_extra_kernel(xb_ref, yb_ref, sconv_ref, sh_ref, cw_ref, cb_ref, wgx_ref, bgx_ref, wga_ref, bga_ref,
                      ap_ref, lru_ref, hs_ref, hmeta_ref, ext_ref, a3_ref, b3_ref, h3_ref):
    nb = DEC_BATCH
    xb = xb_ref[0:nb, :]
    xc = sconv_ref[:, 0:LRU_WIDTH] * cw_ref[0:1, :]
    xc = xc + sconv_ref[:, LRU_WIDTH:2 * LRU_WIDTH] * cw_ref[1:2, :]
    xc = xc + sconv_ref[:, 2 * LRU_WIDTH:3 * LRU_WIDTH] * cw_ref[2:3, :]
    xc = xc + xb * cw_ref[3:4, :]
    xc = xc + cb_ref[...]
    for n in range(N_LRU_BLOCKS):
        cols = slice(n * LRU_BLOCK, (n + 1) * LRU_BLOCK)
        a, b = _gate_ab(xc[:, cols], n, wgx_ref, bgx_ref, wga_ref, bga_ref, ap_ref)
        h = a * sh_ref[:, cols] + b
        hs_ref[:, cols] = h
        lru_ref[0:nb, cols] = (jax.nn.gelu(yb_ref[0:nb, cols]) * h).astype(BF16)
    rows = EXTRA_ROWS - nb
    ext_ref[0:SUBLANES, :] = jnp.zeros((SUBLANES, LRU_WIDTH), F32)
    h3_ref[...] = jnp.zeros(h3_ref.shape, F32)
    first_group = (rows - N_META) // SUBLANES
    hmeta_ref[...] = _lru_rows(rows, ext_ref, a3_ref, b3_ref, h3_ref, xb_ref[nb:, :], yb_ref.at[nb:, :], cw_ref,
                               cb_ref, wgx_ref, bgx_ref, wga_ref, bga_ref, ap_ref, lru_ref.at[nb:, :],
                               jnp.zeros((SUBLANES, LANES), F32), first_group, rows // SUBLANES)


def _lru_extra(proj_e, sconv, sh, cw, cb, wgx, bgx, wga, bga, ap):
    const = lambda shape: pl.BlockSpec(shape, lambda i: (0,) * len(shape))
    rows = EXTRA_ROWS - DEC_BATCH
    scan_shape = (rows // SUBLANES, N_LRU_BLOCKS * SUBLANES, LANES)
    return pl.pallas_call(
        _lru_extra_kernel,
        grid=(1,),
        in_specs=[
            pl.BlockSpec((EXTRA_ROWS, LRU_WIDTH), lambda i: (0, COL_XB // LRU_WIDTH)),
            pl.BlockSpec((EXTRA_ROWS, LRU_WIDTH), lambda i: (0, COL_YB // LRU_WIDTH)),
            const((DEC_BATCH, (CONV_W - 1) * LRU_WIDTH)),
            const((DEC_BATCH, LRU_WIDTH)),
            const((CONV_W, LRU_WIDTH)),
            const((1, LRU_WIDTH)),
            const((N_LRU_BLOCKS, LRU_BLOCK, LRU_BLOCK)),
            const((1, LRU_WIDTH)),
            const((N_LRU_BLOCKS, LRU_BLOCK, LRU_BLOCK)),
            const((1, LRU_WIDTH)),
            const((1, LRU_WIDTH)),
        ],
        out_specs=[
            const((EXTRA_ROWS, LRU_WIDTH)),
            const((DEC_BATCH, LRU_WIDTH)),
            const((SUBLANES, LANES)),
        ],
        out_shape=[
            jax.ShapeDtypeStruct((EXTRA_ROWS, LRU_WIDTH), BF16),
            jax.ShapeDtypeStruct((DEC_BATCH, LRU_WIDTH), F32),
            jax.ShapeDtypeStruct((SUBLANES, LANES), F32),
        ],
        scratch_shapes=[
            pltpu.VMEM((rows + SUBLANES, LRU_WIDTH), F32),
            pltpu.VMEM(scan_shape, F32),
            pltpu.VMEM(scan_shape, F32),
            pltpu.VMEM(scan_shape, F32),
        ],
        compiler_params=_cparams(("arbitrary",), 8 * _nbytes((DEC_BATCH, LRU_WIDTH), F32),
                                 8 * _nbytes((rows, LRU_WIDTH), F32)),
        name="lru_extra",
    )(proj_e, proj_e, sconv, sh, cw, cb, wgx, bgx, wga, bga, ap)


def _attend_kv_head(q_h, kk, vv, bias, sinks):
    r = q_h.shape[0]
    qs = jnp.concatenate([q_h[:, g * HEAD_DIM:(g + 1) * HEAD_DIM] for g in range(GROUP)], axis=0)
    qs = (qs * (HEAD_DIM ** -0.5)).astype(BF16)
    s = lax.dot_general(kk.astype(BF16), qs, (((1,), (1,)), ((), ())), preferred_element_type=F32) + bias
    sink = jnp.concatenate([jnp.full((1, r), sinks[g], F32) for g in range(GROUP)], axis=1)
    m = jnp.maximum(jnp.max(s, axis=0, keepdims=True), sink)
    e = jnp.exp(s - m)
    denom = jnp.sum(e, axis=0, keepdims=True) + jnp.exp(sink - m)
    o_t = jnp.dot(vv.T.astype(BF16), e.astype(BF16), preferred_element_type=F32) / denom
    pairs = [jnp.concatenate([o_t[:, g * r:(g + 1) * r], o_t[:, (g + 1) * r:(g + 2) * r]], axis=0).T
             for g in range(0, GROUP, 2)]
    return jnp.concatenate(pairs, axis=-1)


ATTN_KEYS = N_META + 2 * BLOCK


ATTN_BLOCKS_PER_STEP = 2


def _attn_main_kernel(sink_ref, q_ref, kc_ref, kp_ref, vc_ref, vp_ref, km_ref, vm_ref, o_ref, bias_ref):
    step = pl.program_id(0)
    slopes = _slopes()

    @pl.when(step == 0)
    def _():
        key = lax.broadcasted_iota(jnp.int32, (ATTN_KEYS, BLOCK), 0)
        qry = lax.broadcasted_iota(jnp.int32, (ATTN_KEYS, BLOCK), 1)
        d = qry + BLOCK - (key - N_META)
        dist = jnp.where(key < N_META, 0, d).astype(F32)
        for has_prev in range(2):
            band_ok = (d >= 0) & (d <= WINDOW) & ((key >= N_META + BLOCK) | bool(has_prev))
            valid = (key < N_META) | band_ok
            for hd in range(N_HEADS):
                lanes = slice((hd % GROUP) * BLOCK, (hd % GROUP + 1) * BLOCK)
                bias_ref[has_prev, hd // GROUP, :, lanes] = jnp.where(valid, -(slopes[hd] * dist), NEG_INF)

    for sub in range(ATTN_BLOCKS_PER_STEP):
        rows = slice(sub * BLOCK, (sub + 1) * BLOCK)
        prev_k, prev_v = (kp_ref, vp_ref) if sub == 0 else (kc_ref.at[(sub - 1) * BLOCK:sub * BLOCK],
                                                             vc_ref.at[(sub - 1) * BLOCK:sub * BLOCK])
        table = jnp.where(step == 0, 0, 1) if sub == 0 else 1
        for h in range(N_KV_HEADS):
            hc = slice(h * HEAD_DIM, (h + 1) * HEAD_DIM)
            qc = slice(h * GROUP * HEAD_DIM, (h + 1) * GROUP * HEAD_DIM)
            kk = jnp.concatenate([km_ref[:, hc], prev_k[:, hc], kc_ref[rows, hc]], axis=0)
            vv = jnp.concatenate([vm_ref[:, hc], prev_v[:, hc], vc_ref[rows, hc]], axis=0)
            o = _attend_kv_head(q_ref[rows, qc], kk, vv, bias_ref[table, h],
                                [sink_ref[h * GROUP + g] for g in range(GROUP)])
            o_ref[rows, qc] = o.astype(BF16)


def _attn_main(sinks, proj, k_meta, v_meta):
    m = proj.shape[0]
    rows = ATTN_BLOCKS_PER_STEP * BLOCK
    kv = lambda col: (pl.BlockSpec((rows, KV_WIDTH), lambda i: (i, col // KV_WIDTH)),
                      pl.BlockSpec((BLOCK, KV_WIDTH),
                                   lambda i: (jnp.maximum(ATTN_BLOCKS_PER_STEP * i - 1, 0), col // KV_WIDTH)))
    kc, kp = kv(COL_K)
    vc, vp = kv(COL_V)
    bias_shape = (2, N_KV_HEADS, ATTN_KEYS, GROUP * BLOCK)
    return pl.pallas_call(
        _attn_main_kernel,
        grid=(m // rows,),
        in_specs=[
            pl.BlockSpec(memory_space=pltpu.SMEM),
            pl.BlockSpec((rows, ATTN_WIDTH), lambda i: (i, COL_Q // ATTN_WIDTH)),
            kc, kp, vc, vp,
            pl.BlockSpec((N_META, KV_WIDTH), lambda i: (0, 0)),
            pl.BlockSpec((N_META, KV_WIDTH), lambda i: (0, 0)),
        ],
        out_specs=pl.BlockSpec((rows, ATTN_WIDTH), lambda i: (i, 0)),
        out_shape=jax.ShapeDtypeStruct((m, ATTN_WIDTH), BF16),
        scratch_shapes=[pltpu.VMEM(bias_shape, F32)],
        compiler_params=_cparams(("arbitrary",), 3 * _nbytes((rows, ATTN_WIDTH), F32),
                                 2 * _nbytes(bias_shape, F32)),
        name="attn_main",
    )(sinks, proj, proj, proj, proj, proj, k_meta, v_meta)


def _attn_meta_kernel(sink_ref, q_ref, km_ref, vm_ref, o_ref):
    key = lax.broadcasted_iota(jnp.int32, (N_META, GROUP * N_META), 0)
    qry = lax.broadcasted_iota(jnp.int32, (N_META, GROUP * N_META), 1) % N_META
    bias = jnp.where(qry >= key, 0.0, NEG_INF)
    for h in range(N_KV_HEADS):
        hc = slice(h * HEAD_DIM, (h + 1) * HEAD_DIM)
        q_h = q_ref[:, h * GROUP * HEAD_DIM:(h + 1) * GROUP * HEAD_DIM]
        o = _attend_kv_head(q_h, km_ref[:, hc], vm_ref[:, hc], bias,
                            [sink_ref[h * GROUP + g] for g in range(GROUP)])
        o_ref[:, h * GROUP * HEAD_DIM:(h + 1) * GROUP * HEAD_DIM] = o


def _attn_meta(sinks, q_meta, k_meta, v_meta):
    const = lambda shape: pl.BlockSpec(shape, lambda i: (0,) * len(shape))
    return pl.pallas_call(
        _attn_meta_kernel,
        grid=(1,),
        in_specs=[pl.BlockSpec(memory_space=pltpu.SMEM), const((N_META, ATTN_WIDTH)),
                  const((N_META, KV_WIDTH)), const((N_META, KV_WIDTH))],
        out_specs=const((N_META, ATTN_WIDTH)),
        out_shape=jax.ShapeDtypeStruct((N_META, ATTN_WIDTH), F32),
        compiler_params=_cparams(("arbitrary",), 4 * _nbytes((N_META, ATTN_WIDTH), F32)),
        name="attn_meta",
    )(sinks, q_meta, k_meta, v_meta)


def _attn_sample_kernel(q_ref, km_ref, kw_ref, kn_ref, vm_ref, vw_ref, vn_ref, bias_ref, sink_ref,
                        o_ref, kwo_ref, vwo_ref):
    for src_ref, new_ref, dst_ref in ((kw_ref, kn_ref, kwo_ref), (vw_ref, vn_ref, vwo_ref)):
        dst_ref[:, 0:WINDOW - 1, :] = src_ref[:, 1:WINDOW, :]
        dst_ref[:, WINDOW - 1:WINDOW, :] = new_ref[...]
    dot_qk = functools.partial(jnp.einsum, "bgd,bjd->bgj", preferred_element_type=F32)
    dot_pv = functools.partial(jnp.einsum, "bgj,bjd->bgd", preferred_element_type=F32)
    for h in range(N_KV_HEADS):
        hc = slice(h * HEAD_DIM, (h + 1) * HEAD_DIM)
        q = (q_ref[:, h] * (HEAD_DIM ** -0.5)).astype(BF16)
        k_new = kn_ref[:, :, hc].astype(BF16).astype(F32)
        v_new = vn_ref[:, :, hc].astype(BF16).astype(F32)
        s_m = dot_qk(q, km_ref[:, :, hc].astype(BF16))
        s_w = dot_qk(q, kw_ref[:, :, hc].astype(BF16)) + bias_ref[h]
        s_n = jnp.sum(q.astype(F32) * k_new, axis=-1, keepdims=True)
        sink = sink_ref[h]
        m = jnp.maximum(jnp.maximum(jnp.max(s_m, axis=-1, keepdims=True), jnp.max(s_w, axis=-1, keepdims=True)),
                        jnp.maximum(s_n, sink))
        e_m, e_w, e_n = jnp.exp(s_m - m), jnp.exp(s_w - m), jnp.exp(s_n - m)
        denom = (jnp.sum(e_m, axis=-1, keepdims=True) + jnp.sum(e_w, axis=-1, keepdims=True) + e_n
                 + jnp.exp(sink - m))
        o = (dot_pv(e_m.astype(BF16), vm_ref[:, :, hc].astype(BF16))
             + dot_pv(e_w.astype(BF16), vw_ref[:, :, hc].astype(BF16))
             + e_n.astype(BF16).astype(F32) * v_new)
        o_ref[:, h] = o / denom


def _attn_sample(q4, k_meta, k_win, k_new, v_meta, v_win, v_new, bias, sinks4, bb):
    b3 = lambda rows: pl.BlockSpec((bb, rows, KV_WIDTH), lambda i: (i, 0, 0))
    q_spec = pl.BlockSpec((bb, N_KV_HEADS, GROUP, HEAD_DIM), lambda i: (i, 0, 0, 0))
    win_shape = jax.ShapeDtypeStruct((DEC_BATCH, WINDOW, KV_WIDTH), F32)
    return pl.pallas_call(
        _attn_sample_kernel,
        grid=(DEC_BATCH // bb,),
        in_specs=[
            q_spec,
            b3(N_META), b3(WINDOW), b3(1),
            b3(N_META), b3(WINDOW), b3(1),
            pl.BlockSpec((N_KV_HEADS, GROUP, WINDOW), lambda i: (0, 0, 0)),
            pl.BlockSpec((N_KV_HEADS, GROUP, 1), lambda i: (0, 0, 0)),
        ],
        out_specs=[q_spec, b3(WINDOW), b3(WINDOW)],
        out_shape=[jax.ShapeDtypeStruct((DEC_BATCH, N_KV_HEADS, GROUP, HEAD_DIM), F32), win_shape, win_shape],
        compiler_params=_cparams(("arbitrary",), 5 * _nbytes((bb, WINDOW, KV_WIDTH), F32),
                                 4 * _nbytes((bb, WINDOW, KV_WIDTH), F32)),
        name="attn_sample",
    )(q4, k_meta, k_win, k_new, v_meta, v_win, v_new, bias, sinks4)


N_GATE_CHUNKS = D_MODEL // GATE_CHUNK


def _outproj_kernel(lru_ref, attn_ref, *refs):
    gl_refs, ga_refs = refs[:N_GATE_CHUNKS], refs[N_GATE_CHUNKS:2 * N_GATE_CHUNKS]
    x_ref, wl_ref, wa_ref, wo_ref, o_ref = refs[2 * N_GATE_CHUNKS:]
    l = jnp.dot(lru_ref[...], wl_ref[...], preferred_element_type=F32)
    a = jnp.dot(attn_ref[...], wa_ref[...], preferred_element_type=F32)
    merged = jnp.concatenate(
        [jax.nn.sigmoid(gl_refs[c][...]) * l[:, c * GATE_CHUNK:(c + 1) * GATE_CHUNK]
         + jax.nn.sigmoid(ga_refs[c][...]) * a[:, c * GATE_CHUNK:(c + 1) * GATE_CHUNK]
         for c in range(N_GATE_CHUNKS)], axis=1)
    o_ref[...] = x_ref[...] + jnp.dot(merged.astype(BF16), wo_ref[...], preferred_element_type=F32)


def _outproj(lru, attn, proj, x, wl, wa, wo):
    m = x.shape[0]
    tm = _tile_plan(m)["outproj"]
    const = lambda shape: pl.BlockSpec(shape, lambda i: (0,) * len(shape))
    tile = _nbytes((tm, D_MODEL), F32)
    windows = 5 * tile + _nbytes(wl.shape, BF16) + _nbytes(wa.shape, BF16) + _nbytes(wo.shape, BF16)
    return pl.pallas_call(
        _outproj_kernel,
        grid=(m // tm,),
        in_specs=[
            pl.BlockSpec((tm, LRU_WIDTH), lambda i: (i, 0)),
            pl.BlockSpec((tm, ATTN_WIDTH), lambda i: (i, 0)),
            *[pl.BlockSpec((tm, GATE_CHUNK), functools.partial(lambda i, cb: (i, cb), cb=col // GATE_CHUNK + c))
              for col in (COL_GL, COL_GA) for c in range(N_GATE_CHUNKS)],
            pl.BlockSpec((tm, D_MODEL), lambda i: (i, 0)),
            const((LRU_WIDTH, D_MODEL)),
            const((ATTN_WIDTH, D_MODEL)),
            const((D_MODEL, D_MODEL)),
        ],
        out_specs=pl.BlockSpec((tm, D_MODEL), lambda i: (i, 0)),
        out_shape=jax.ShapeDtypeStruct((m, D_MODEL), F32),
        compiler_params=_cparams(("arbitrary",), windows, 3 * tile),
        name=f"outproj_{m}",
    )(lru, attn, *([proj] * (2 * N_GATE_CHUNKS)), x, wl, wa, wo)


def _mlp_kernel(h_ref, g2_ref, gf_ref, wu_ref, wd_ref, o_ref, hn_ref):
    f = pl.program_id(1)

    @pl.when(f == 0)
    def _():
        h = h_ref[...]
        ms = jnp.mean(h * h, axis=-1, keepdims=True)
        hn_ref[...] = (h * lax.rsqrt(ms + EPS) * g2_ref[...]).astype(BF16)
        o_ref[...] = jnp.zeros(o_ref.shape, F32)

    u = jnp.dot(hn_ref[...], wu_ref[...], preferred_element_type=F32)
    u = jnp.square(jnp.maximum(u, 0.0))
    o_ref[...] += jnp.dot(u.astype(BF16), wd_ref[...], preferred_element_type=F32)

    @pl.when(f == pl.num_programs(1) - 1)
    def _():
        out = h_ref[...] + o_ref[...]
        ms = jnp.mean(out * out, axis=-1, keepdims=True)
        o_ref[...] = out * lax.rsqrt(ms + EPS) * gf_ref[...]


def _mlp(h, g2, gf, wu, wd):
    m = h.shape[0]
    tm, tf = _tile_plan(m)["mlp"]
    windows = 2 * _nbytes((tm, D_MODEL), F32) + 2 * _nbytes((D_MODEL, tf), BF16)
    resident = _nbytes((tm, D_MODEL), BF16) + 2 * _nbytes((tm, tf), F32)
    return pl.pallas_call(
        _mlp_kernel,
        grid=(m // tm, D_FF // tf),
        in_specs=[
            pl.BlockSpec((tm, D_MODEL), lambda i, f: (i, 0)),
            pl.BlockSpec((1, D_MODEL), lambda i, f: (0, 0)),
            pl.BlockSpec((1, D_MODEL), lambda i, f: (0, 0)),
            pl.BlockSpec((D_MODEL, tf), lambda i, f: (0, f)),
            pl.BlockSpec((tf, D_MODEL), lambda i, f: (f, 0)),
        ],
        out_specs=pl.BlockSpec((tm, D_MODEL), lambda i, f: (i, 0)),
        out_shape=jax.ShapeDtypeStruct((m, D_MODEL), F32),
        scratch_shapes=[pltpu.VMEM((tm, D_MODEL), BF16)],
        compiler_params=_cparams(("arbitrary", "arbitrary"), windows, resident),
        name=f"mlp_{m}",
    )(h, g2, gf, wu, wd)


def _sample_bias():
    dist = (WINDOW - np.arange(WINDOW)).astype(np.float32)
    slopes = np.asarray(_slopes(), np.float32).reshape(N_KV_HEADS, GROUP, 1)
    return jnp.asarray(-(slopes * dist[None, None, :]))


def kernel(x_prompt, x_sample, cache_meta_k, cache_meta_v, cache_win_k, cache_win_v, state_conv, state_h,
           meta_tokens, norm1_g, w_in, conv_w, conv_b, w_gate_x, b_gate_x, w_gate_a, b_gate_a, lru_a_param,
           attn_sinks, w_lru_out, w_attn_out, w_o, norm2_g, w_mlp_up, w_mlp_down, final_norm_g):
    row = lambda v: v.reshape(1, -1)
    g1, g2, gf = row(norm1_g[0]), row(norm2_g[0]), row(final_norm_g)
    w_in_b = w_in[0].astype(BF16)
    wgx, wga = w_gate_x[0].astype(BF16), w_gate_a[0].astype(BF16)
    wl, wa, wo = w_lru_out[0].astype(BF16), w_attn_out[0].astype(BF16), w_o[0].astype(BF16)
    wu, wd = w_mlp_up[0].astype(BF16), w_mlp_down[0].astype(BF16)
    cw, cb = conv_w[0], row(conv_b[0])
    bgx, bga, ap = row(b_gate_x[0]), row(b_gate_a[0]), row(lru_a_param[0])
    sinks = attn_sinks[0]

    x_main = x_prompt.reshape(SEQ, D_MODEL)
    x_extra = jnp.concatenate([
        x_sample.reshape(DEC_BATCH, D_MODEL),
        jnp.zeros((EXTRA_ROWS - DEC_BATCH - N_META, D_MODEL), F32),
        meta_tokens.astype(F32)], axis=0)

    proj_m = _inproj(x_main, g1, w_in_b)
    proj_e = _inproj(x_extra, g1, w_in_b)

    lru_e, h_sample, h_meta = _lru_extra(proj_e, state_conv[0].reshape(DEC_BATCH, (CONV_W - 1) * LRU_WIDTH),
                                         state_h[0], cw, cb, wgx, bgx, wga, bga, ap)
    tail = proj_e[EXTRA_ROWS - SUBLANES:, COL_XB:COL_XB + LRU_WIDTH]
    lru_m, h_last = _lru_main(proj_m, tail, h_meta, cw, cb, wgx, bgx, wga, bga, ap)

    k_meta = proj_e[META_ROW0:, COL_K:COL_K + KV_WIDTH]
    v_meta = proj_e[META_ROW0:, COL_V:COL_V + KV_WIDTH]
    attn_m = _attn_main(sinks, proj_m, k_meta, v_meta)
    attn_meta = _attn_meta(sinks, proj_e[META_ROW0:, COL_Q:COL_Q + ATTN_WIDTH], k_meta, v_meta)
    k_new = proj_e[:DEC_BATCH, None, COL_K:COL_K + KV_WIDTH]
    v_new = proj_e[:DEC_BATCH, None, COL_V:COL_V + KV_WIDTH]
    q4 = proj_e[:DEC_BATCH, COL_Q:COL_Q + ATTN_WIDTH].reshape(DEC_BATCH, N_KV_HEADS, GROUP, HEAD_DIM)
    flat = lambda c, n: c[0].reshape(DEC_BATCH, n, KV_WIDTH)
    attn_s, kw_out, vw_out = _attn_sample(
        q4, flat(cache_meta_k, N_META), flat(cache_win_k, WINDOW), k_new,
        flat(cache_meta_v, N_META), flat(cache_win_v, WINDOW), v_new,
        _sample_bias(), sinks.reshape(N_KV_HEADS, GROUP, 1), bb=8)
    attn_e = jnp.concatenate([
        attn_s.reshape(DEC_BATCH, ATTN_WIDTH),
        jnp.zeros((EXTRA_ROWS - DEC_BATCH - N_META, ATTN_WIDTH), F32),
        attn_meta], axis=0).astype(BF16)

    res_m = _outproj(lru_m, attn_m, proj_m, x_main, wl, wa, wo)
    res_e = _outproj(lru_e, attn_e, proj_e, x_extra, wl, wa, wo)
    y_m = _mlp(res_m, g2, gf, wu, wd)
    y_e = _mlp(res_e, g2, gf, wu, wd)

    kv5 = lambda a, n: a.reshape(1, -1, n, N_KV_HEADS, HEAD_DIM)
    return (
        y_m.reshape(1, SEQ, D_MODEL),
        y_e[:DEC_BATCH].reshape(DEC_BATCH, 1, D_MODEL),
        kv5(k_meta, N_META), kv5(v_meta, N_META),
        kv5(proj_m[SEQ - WINDOW:, COL_K:COL_K + KV_WIDTH], WINDOW),
        kv5(proj_m[SEQ - WINDOW:, COL_V:COL_V + KV_WIDTH], WINDOW),
        proj_m[SEQ - (CONV_W - 1):, COL_XB:COL_XB + LRU_WIDTH].reshape(1, 1, CONV_W - 1, LRU_WIDTH),
        h_last.reshape(1, 1, LRU_WIDTH),
        kv5(kw_out, WINDOW), kv5(vw_out, WINDOW),
        jnp.concatenate([state_conv[0][:, 1:], proj_e[:DEC_BATCH, None, COL_XB:COL_XB + LRU_WIDTH]], axis=1)[None],
        h_sample[None],
    )
```

```python
import functools

import numpy as np
import jax
import jax.numpy as jnp
from jax import lax
from jax.experimental import pallas as pl
from jax.experimental.pallas import tpu as pltpu

D_MODEL = 2048
SEQ = 16384
DEC_BATCH = 128
N_META = 16
LRU_WIDTH = 1024
N_LRU_BLOCKS = 8
LRU_BLOCK = LRU_WIDTH // N_LRU_BLOCKS
CONV_W = 4
LRU_C = 8.0
N_HEADS = 16
N_KV_HEADS = 4
HEAD_DIM = 64
GROUP = N_HEADS // N_KV_HEADS
ATTN_WIDTH = N_HEADS * HEAD_DIM
KV_WIDTH = N_KV_HEADS * HEAD_DIM
WINDOW = 128
BLOCK = 128
D_FF = 4 * D_MODEL
EPS = 1e-6
NEG_INF = -1e30
IN_WIDTH = 2 * LRU_WIDTH + ATTN_WIDTH + 2 * KV_WIDTH + 2 * D_MODEL
COL_XB, COL_YB, COL_Q = 0, LRU_WIDTH, 2 * LRU_WIDTH
COL_K = COL_Q + ATTN_WIDTH
COL_V = COL_K + KV_WIDTH
COL_GL = COL_V + KV_WIDTH
COL_GA = COL_GL + D_MODEL
GATE_CHUNK = 512

EXTRA_ROWS = 256
META_ROW0 = EXTRA_ROWS - N_META
SUBLANES = 8
LANES = 128
MIB = 1024 * 1024

F32 = jnp.float32
BF16 = jnp.bfloat16


def _slopes():
    return [2.0 ** (-8.0 * (h + 1) / N_HEADS) for h in range(N_HEADS)]


V7X_VMEM_REQUEST_CAP = 60 * MIB


def _nbytes(shape, dtype):
    return int(np.prod(shape)) * jnp.dtype(dtype).itemsize


def _cparams(sem, pipelined, resident=0):
    estimate = 2 * pipelined + resident
    limit = min(V7X_VMEM_REQUEST_CAP, estimate + estimate // 4 + 2 * MIB)
    return pltpu.CompilerParams(dimension_semantics=sem, vmem_limit_bytes=limit)


def _tile_plan(rows):
    if rows == EXTRA_ROWS:
        return dict(inproj=(EXTRA_ROWS, 1536), outproj=EXTRA_ROWS, mlp=(EXTRA_ROWS, 512))
    return dict(inproj=(1024, 1536), lru=512, outproj=256, mlp=(1024, 512))


CAST_STEPS = 64


def _cast_specs(weights, step_of):
    specs, shapes, nbytes = [], [], 0
    for w in weights:
        block = (w.shape[0] // CAST_STEPS, w.shape[1])
        specs.append(pl.BlockSpec(block, lambda *idx: (jnp.minimum(step_of(*idx), CAST_STEPS - 1), 0)))
        shapes.append(jax.ShapeDtypeStruct(w.shape, BF16))
        nbytes += _nbytes(block, F32) + _nbytes(block, BF16)
    return specs, shapes, nbytes


def _cast_chunks(src_refs, dst_refs):
    for src_ref, dst_ref in zip(src_refs, dst_refs):
        dst_ref[...] = src_ref[...].astype(BF16)


def _inproj_kernel(x_ref, g_ref, w_ref, *refs, n_cast):
    cast_src, o_ref, cast_dst, xs_ref = refs[:n_cast], refs[n_cast], refs[n_cast + 1:-1], refs[-1]

    @pl.when(pl.program_id(1) == 0)
    def _():
        x = x_ref[...]
        ms = jnp.mean(x * x, axis=-1, keepdims=True)
        xs_ref[...] = (x * lax.rsqrt(ms + EPS) * g_ref[...]).astype(BF16)

    o_ref[...] = jnp.dot(xs_ref[...], w_ref[...], preferred_element_type=F32)
    _cast_chunks(cast_src, cast_dst)


def _inproj(x, g, w, cast_weights=()):
    m = x.shape[0]
    tm, tn = _tile_plan(m)["inproj"]
    n_col = IN_WIDTH // tn
    cast_specs, cast_shapes, cast_bytes = _cast_specs(cast_weights, lambda i, j: i * n_col + j)
    assert not cast_weights or (m // tm) * n_col >= CAST_STEPS
    windows = _nbytes((tm, D_MODEL), F32) + _nbytes((D_MODEL, tn), BF16) + _nbytes((tm, tn), F32) + cast_bytes
    return pl.pallas_call(
        functools.partial(_inproj_kernel, n_cast=len(cast_weights)),
        grid=(m // tm, n_col),
        in_specs=[
            pl.BlockSpec((tm, D_MODEL), lambda i, j: (i, 0)),
            pl.BlockSpec((1, D_MODEL), lambda i, j: (0, 0)),
            pl.BlockSpec((D_MODEL, tn), lambda i, j: (0, j)),
            *cast_specs,
        ],
        out_specs=[pl.BlockSpec((tm, tn), lambda i, j: (i, j)), *cast_specs],
        out_shape=[jax.ShapeDtypeStruct((m, IN_WIDTH), F32), *cast_shapes],
        scratch_shapes=[pltpu.VMEM((tm, D_MODEL), BF16)],
        compiler_params=_cparams(("arbitrary", "arbitrary"), windows, _nbytes((tm, D_MODEL), BF16)),
        name=f"inproj_{m}",
    )(x, g, w, *cast_weights)


def _gate_ab(xc_n, n, wgx_ref, bgx_ref, wga_ref, bga_ref, ap_ref):
    cols = slice(n * LRU_BLOCK, (n + 1) * LRU_BLOCK)
    xcb = xc_n.astype(BF16)
    gx = jnp.dot(xcb, wgx_ref[n], preferred_element_type=F32) + bgx_ref[:, cols]
    ga = jnp.dot(xcb, wga_ref[n], preferred_element_type=F32) + bga_ref[:, cols]
    gate_x = jax.nn.sigmoid(gx)
    gate_a = jax.nn.sigmoid(ga)
    log_a = -LRU_C * gate_a * jax.nn.softplus(-ap_ref[:, cols])
    a = jnp.exp(log_a)
    z = -jnp.tanh(log_a) * (a * a + 1.0)
    root = jnp.where(z > 0.0, z * lax.rsqrt(z), 0.0)
    b = root * gate_x * xc_n
    return a, b


def _lru_rows(rows, ext_ref, a3_ref, b3_ref, h3_ref, xb, yb_ref, cw_ref, cb_ref,
              wgx_ref, bgx_ref, wga_ref, bga_ref, ap_ref, lru_ref, h_init, g_lo, g_hi):
    groups = rows // SUBLANES
    ext_ref[SUBLANES:SUBLANES + rows, :] = xb
    x_ext = ext_ref[...]
    xc = x_ext * cw_ref[0:1, :]
    for t in range(1, CONV_W):
        xc = pltpu.roll(xc, 1, axis=0) + x_ext * cw_ref[t:t + 1, :]
    xc = xc[SUBLANES:, :] + cb_ref[...]
    for n in range(N_LRU_BLOCKS):
        a, b = _gate_ab(xc[:, n * LRU_BLOCK:(n + 1) * LRU_BLOCK], n, wgx_ref, bgx_ref, wga_ref, bga_ref, ap_ref)
        a3_ref[:, n * SUBLANES:(n + 1) * SUBLANES, :] = a.reshape(groups, SUBLANES, LANES)
        b3_ref[:, n * SUBLANES:(n + 1) * SUBLANES, :] = b.reshape(groups, SUBLANES, LANES)

    def group_step(i, h):
        for r in range(SUBLANES):
            step = pl.ds(r, N_LRU_BLOCKS, stride=SUBLANES)
            h = a3_ref[i, step, :] * h + b3_ref[i, step, :]
            h3_ref[i, step, :] = h
        return h

    h = lax.fori_loop(g_lo, g_hi, group_step, h_init)
    for n in range(N_LRU_BLOCKS):
        cols = slice(n * LRU_BLOCK, (n + 1) * LRU_BLOCK)
        hs = h3_ref[:, n * SUBLANES:(n + 1) * SUBLANES, :].reshape(rows, LANES)
        lru_ref[:, cols] = (jax.nn.gelu(yb_ref[:, cols]) * hs).astype(BF16)
    return h


def _lru_main_kernel(xb_ref, yb_ref, tail_ref, h0_ref, cw_ref, cb_ref, wgx_ref, bgx_ref, wga_ref, bga_ref,
                     ap_ref, lru_ref, hlast_ref, ext_ref, a3_ref, b3_ref, h3_ref, hstate_ref, *, rows):
    @pl.when(pl.program_id(0) == 0)
    def _():
        ext_ref[0:SUBLANES, :] = tail_ref[...]
        hstate_ref[...] = h0_ref[...]

    h = _lru_rows(rows, ext_ref, a3_ref, b3_ref, h3_ref, xb_ref[...], yb_ref, cw_ref, cb_ref,
                  wgx_ref, bgx_ref, wga_ref, bga_ref, ap_ref, lru_ref, hstate_ref[...], 0, rows // SUBLANES)
    hstate_ref[...] = h
    hlast_ref[...] = h
    ext_ref[0:SUBLANES, :] = ext_ref[rows:rows + SUBLANES, :]


def _lru_main(proj, tail, h0, cw, cb, wgx, bgx, wga, bga, ap):
    m = proj.shape[0]
    tt = _tile_plan(m)["lru"]
    const = lambda shape: pl.BlockSpec(shape, lambda i: (0,) * len(shape))
    scan_shape = (tt // SUBLANES, N_LRU_BLOCKS * SUBLANES, LANES)
    tile = _nbytes((tt, LRU_WIDTH), F32)
    windows = 3 * tile + 2 * _nbytes(wgx.shape, BF16)
    return pl.pallas_call(
        functools.partial(_lru_main_kernel, rows=tt),
        grid=(m // tt,),
        in_specs=[
            pl.BlockSpec((tt, LRU_WIDTH), lambda i: (i, COL_XB // LRU_WIDTH)),
            pl.BlockSpec((tt, LRU_WIDTH), lambda i: (i, COL_YB // LRU_WIDTH)),
            const((SUBLANES, LRU_WIDTH)),
            const((SUBLANES, LANES)),
            const((CONV_W, LRU_WIDTH)),
            const((1, LRU_WIDTH)),
            const((N_LRU_BLOCKS, LRU_BLOCK, LRU_BLOCK)),
            const((1, LRU_WIDTH)),
            const((N_LRU_BLOCKS, LRU_BLOCK, LRU_BLOCK)),
            const((1, LRU_WIDTH)),
            const((1, LRU_WIDTH)),
        ],
        out_specs=[
            pl.BlockSpec((tt, LRU_WIDTH), lambda i: (i, 0)),
            const((SUBLANES, LANES)),
        ],
        out_shape=[
            jax.ShapeDtypeStruct((m, LRU_WIDTH), BF16),
            jax.ShapeDtypeStruct((SUBLANES, LANES), F32),
        ],
        scratch_shapes=[
            pltpu.VMEM((tt + SUBLANES, LRU_WIDTH), F32),
            pltpu.VMEM(scan_shape, F32),
            pltpu.VMEM(scan_shape, F32),
            pltpu.VMEM(scan_shape, F32),
            pltpu.VMEM((SUBLANES, LANES), F32),
        ],
        compiler_params=_cparams(("arbitrary",), windows, 8 * tile),
        name="lru_main",
    )(proj, proj, tail, h0, cw, cb, wgx, bgx, wga, bga, ap)


def _lru_extra_kernel(xb_ref, yb_ref, sconv_ref, sh_ref, cw_ref, cb_ref, wgx_ref, bgx_ref, wga_ref, bga_ref,
                      ap_ref, lru_ref, hs_ref, hmeta_ref, ext_ref, a3_ref, b3_ref, h3_ref):
    nb = DEC_BATCH
    xb = xb_ref[0:nb, :]
    xc = sconv_ref[:, 0:LRU_WIDTH] * cw_ref[0:1, :]
    xc = xc + sconv_ref[:, LRU_WIDTH:2 * LRU_WIDTH] * cw_ref[1:2, :]
    xc = xc + sconv_ref[:, 2 * LRU_WIDTH:3 * LRU_WIDTH] * cw_ref[2:3, :]
    xc = xc + xb * cw_ref[3:4, :]
    xc = xc + cb_ref[...]
    for n in range(N_LRU_BLOCKS):
        cols = slice(n * LRU_BLOCK, (n + 1) * LRU_BLOCK)
        a, b = _gate_ab(xc[:, cols], n, wgx_ref, bgx_ref, wga_ref, bga_ref, ap_ref)
        h = a * sh_ref[:, cols] + b
        hs_ref[:, cols] = h
        lru_ref[0:nb, cols] = (jax.nn.gelu(yb_ref[0:nb, cols]) * h).astype(BF16)
    rows = EXTRA_ROWS - nb
    ext_ref[0:SUBLANES, :] = jnp.zeros((SUBLANES, LRU_WIDTH), F32)
    h3_ref[...] = jnp.zeros(h3_ref.shape, F32)
    first_group = (rows - N_META) // SUBLANES
    hmeta_ref[...] = _lru_rows(rows, ext_ref, a3_ref, b3_ref, h3_ref, xb_ref[nb:, :], yb_ref.at[nb:, :], cw_ref,
                               cb_ref, wgx_ref, bgx_ref, wga_ref, bga_ref, ap_ref, lru_ref.at[nb:, :],
                               jnp.zeros((SUBLANES, LANES), F32), first_group, rows // SUBLANES)


def _lru_extra(proj_e, sconv, sh, cw, cb, wgx, bgx, wga, bga, ap):
    const = lambda shape: pl.BlockSpec(shape, lambda i: (0,) * len(shape))
    rows = EXTRA_ROWS - DEC_BATCH
    scan_shape = (rows // SUBLANES, N_LRU_BLOCKS * SUBLANES, LANES)
    return pl.pallas_call(
        _lru_extra_kernel,
        grid=(1,),
        in_specs=[
            pl.BlockSpec((EXTRA_ROWS, LRU_WIDTH), lambda i: (0, COL_XB // LRU_WIDTH)),
            pl.BlockSpec((EXTRA_ROWS, LRU_WIDTH), lambda i: (0, COL_YB // LRU_WIDTH)),
            const((DEC_BATCH, (CONV_W - 1) * LRU_WIDTH)),
            const((DEC_BATCH, LRU_WIDTH)),
            const((CONV_W, LRU_WIDTH)),
            const((1, LRU_WIDTH)),
            const((N_LRU_BLOCKS, LRU_BLOCK, LRU_BLOCK)),
            const((1, LRU_WIDTH)),
            const((N_LRU_BLOCKS, LRU_BLOCK, LRU_BLOCK)),
            const((1, LRU_WIDTH)),
            const((1, LRU_WIDTH)),
        ],
        out_specs=[
            const((EXTRA_ROWS, LRU_WIDTH)),
            const((DEC_BATCH, LRU_WIDTH)),
            const((SUBLANES, LANES)),
        ],
        out_shape=[
            jax.ShapeDtypeStruct((EXTRA_ROWS, LRU_WIDTH), BF16),
            jax.ShapeDtypeStruct((DEC_BATCH, LRU_WIDTH), F32),
            jax.ShapeDtypeStruct((SUBLANES, LANES), F32),
        ],
        scratch_shapes=[
            pltpu.VMEM((rows + SUBLANES, LRU_WIDTH), F32),
            pltpu.VMEM(scan_shape, F32),
            pltpu.VMEM(scan_shape, F32),
            pltpu.VMEM(scan_shape, F32),
        ],
        compiler_params=_cparams(("arbitrary",), 8 * _nbytes((DEC_BATCH, LRU_WIDTH), F32),
                                 8 * _nbytes((rows, LRU_WIDTH), F32)),
        name="lru_extra",
    )(proj_e, proj_e, sconv, sh, cw, cb, wgx, bgx, wga, bga, ap)


def _attend_kv_head(q_h, kk, vv, bias, sinks):
    r = q_h.shape[0]
    qs = jnp.concatenate([q_h[:, g * HEAD_DIM:(g + 1) * HEAD_DIM] for g in range(GROUP)], axis=0)
    qs = (qs * (HEAD_DIM ** -0.5)).astype(BF16)
    s = lax.dot_general(kk.astype(BF16), qs, (((1,), (1,)), ((), ())), preferred_element_type=F32) + bias
    sink = jnp.concatenate([jnp.full((1, r), sinks[g], F32) for g in range(GROUP)], axis=1)
    m = jnp.maximum(jnp.max(s, axis=0, keepdims=True), sink)
    e = jnp.exp(s - m)
    denom = jnp.sum(e, axis=0, keepdims=True) + jnp.exp(sink - m)
    o_t = jnp.dot(vv.T.astype(BF16), e.astype(BF16), preferred_element_type=F32) / denom
    pairs = [jnp.concatenate([o_t[:, g * r:(g + 1) * r], o_t[:, (g + 1) * r:(g + 2) * r]], axis=0).T
             for g in range(0, GROUP, 2)]
    return jnp.concatenate(pairs, axis=-1)


ATTN_KEYS = N_META + 2 * BLOCK


ATTN_BLOCKS_PER_STEP = 2


def _attn_main_kernel(sink_ref, q_ref, kc_ref, kp_ref, vc_ref, vp_ref, km_ref, vm_ref, o_ref, bias_ref):
    step = pl.program_id(0)
    slopes = _slopes()

    @pl.when(step == 0)
    def _():
        key = lax.broadcasted_iota(jnp.int32, (ATTN_KEYS, BLOCK), 0)
        qry = lax.broadcasted_iota(jnp.int32, (ATTN_KEYS, BLOCK), 1)
        d = qry + BLOCK - (key - N_META)
        dist = jnp.where(key < N_META, 0, d).astype(F32)
        for has_prev in range(2):
            band_ok = (d >= 0) & (d <= WINDOW) & ((key >= N_META + BLOCK) | bool(has_prev))
            valid = (key < N_META) | band_ok
            for hd in range(N_HEADS):
                lanes = slice((hd % GROUP) * BLOCK, (hd % GROUP + 1) * BLOCK)
                bias_ref[has_prev, hd // GROUP, :, lanes] = jnp.where(valid, -(slopes[hd] * dist), NEG_INF)

    for sub in range(ATTN_BLOCKS_PER_STEP):
        rows = slice(sub * BLOCK, (sub + 1) * BLOCK)
        prev_k, prev_v = (kp_ref, vp_ref) if sub == 0 else (kc_ref.at[(sub - 1) * BLOCK:sub * BLOCK],
                                                             vc_ref.at[(sub - 1) * BLOCK:sub * BLOCK])
        table = jnp.where(step == 0, 0, 1) if sub == 0 else 1
        for h in range(N_KV_HEADS):
            hc = slice(h * HEAD_DIM, (h + 1) * HEAD_DIM)
            qc = slice(h * GROUP * HEAD_DIM, (h + 1) * GROUP * HEAD_DIM)
            kk = jnp.concatenate([km_ref[:, hc], prev_k[:, hc], kc_ref[rows, hc]], axis=0)
            vv = jnp.concatenate([vm_ref[:, hc], prev_v[:, hc], vc_ref[rows, hc]], axis=0)
            o = _attend_kv_head(q_ref[rows, qc], kk, vv, bias_ref[table, h],
                                [sink_ref[h * GROUP + g] for g in range(GROUP)])
            o_ref[rows, qc] = o.astype(BF16)


def _attn_main(sinks, proj, k_meta, v_meta):
    m = proj.shape[0]
    rows = ATTN_BLOCKS_PER_STEP * BLOCK
    kv = lambda col: (pl.BlockSpec((rows, KV_WIDTH), lambda i: (i, col // KV_WIDTH)),
                      pl.BlockSpec((BLOCK, KV_WIDTH),
                                   lambda i: (jnp.maximum(ATTN_BLOCKS_PER_STEP * i - 1, 0), col // KV_WIDTH)))
    kc, kp = kv(COL_K)
    vc, vp = kv(COL_V)
    bias_shape = (2, N_KV_HEADS, ATTN_KEYS, GROUP * BLOCK)
    return pl.pallas_call(
        _attn_main_kernel,
        grid=(m // rows,),
        in_specs=[
            pl.BlockSpec(memory_space=pltpu.SMEM),
            pl.BlockSpec((rows, ATTN_WIDTH), lambda i: (i, COL_Q // ATTN_WIDTH)),
            kc, kp, vc, vp,
            pl.BlockSpec((N_META, KV_WIDTH), lambda i: (0, 0)),
            pl.BlockSpec((N_META, KV_WIDTH), lambda i: (0, 0)),
        ],
        out_specs=pl.BlockSpec((rows, ATTN_WIDTH), lambda i: (i, 0)),
        out_shape=jax.ShapeDtypeStruct((m, ATTN_WIDTH), BF16),
        scratch_shapes=[pltpu.VMEM(bias_shape, F32)],
        compiler_params=_cparams(("arbitrary",), 3 * _nbytes((rows, ATTN_WIDTH), F32),
                                 2 * _nbytes(bias_shape, F32)),
        name="attn_main",
    )(sinks, proj, proj, proj, proj, proj, k_meta, v_meta)


def _attn_meta_kernel(sink_ref, q_ref, km_ref, vm_ref, o_ref):
    key = lax.broadcasted_iota(jnp.int32, (N_META, GROUP * N_META), 0)
    qry = lax.broadcasted_iota(jnp.int32, (N_META, GROUP * N_META), 1) % N_META
    bias = jnp.where(qry >= key, 0.0, NEG_INF)
    for h in range(N_KV_HEADS):
        hc = slice(h * HEAD_DIM, (h + 1) * HEAD_DIM)
        q_h = q_ref[:, h * GROUP * HEAD_DIM:(h + 1) * GROUP * HEAD_DIM]
        o = _attend_kv_head(q_h, km_ref[:, hc], vm_ref[:, hc], bias,
                            [sink_ref[h * GROUP + g] for g in range(GROUP)])
        o_ref[:, h * GROUP * HEAD_DIM:(h + 1) * GROUP * HEAD_DIM] = o


def _attn_meta(sinks, q_meta, k_meta, v_meta):
    const = lambda shape: pl.BlockSpec(shape, lambda i: (0,) * len(shape))
    return pl.pallas_call(
        _attn_meta_kernel,
        grid=(1,),
        in_specs=[pl.BlockSpec(memory_space=pltpu.SMEM), const((N_META, ATTN_WIDTH)),
                  const((N_META, KV_WIDTH)), const((N_META, KV_WIDTH))],
        out_specs=const((N_META, ATTN_WIDTH)),
        out_shape=jax.ShapeDtypeStruct((N_META, ATTN_WIDTH), F32),
        compiler_params=_cparams(("arbitrary",), 4 * _nbytes((N_META, ATTN_WIDTH), F32)),
        name="attn_meta",
    )(sinks, q_meta, k_meta, v_meta)


def _attn_sample_kernel(q_ref, km_ref, kw_ref, kn_ref, vm_ref, vw_ref, vn_ref, bias_ref, sink_ref,
                        o_ref, kwo_ref, vwo_ref):
    for src_ref, new_ref, dst_ref in ((kw_ref, kn_ref, kwo_ref), (vw_ref, vn_ref, vwo_ref)):
        dst_ref[:, 0:WINDOW - 1, :] = src_ref[:, 1:WINDOW, :]
        dst_ref[:, WINDOW - 1:WINDOW, :] = new_ref[...]
    dot_qk = functools.partial(jnp.einsum, "bgd,bjd->bgj", preferred_element_type=F32)
    dot_pv = functools.partial(jnp.einsum, "bgj,bjd->bgd", preferred_element_type=F32)
    for h in range(N_KV_HEADS):
        hc = slice(h * HEAD_DIM, (h + 1) * HEAD_DIM)
        q = (q_ref[:, h] * (HEAD_DIM ** -0.5)).astype(BF16)
        k_new = kn_ref[:, :, hc].astype(BF16).astype(F32)
        v_new = vn_ref[:, :, hc].astype(BF16).astype(F32)
        s_m = dot_qk(q, km_ref[:, :, hc].astype(BF16))
        s_w = dot_qk(q, kw_ref[:, :, hc].astype(BF16)) + bias_ref[h]
        s_n = jnp.sum(q.astype(F32) * k_new, axis=-1, keepdims=True)
        sink = sink_ref[h]
        m = jnp.maximum(jnp.maximum(jnp.max(s_m, axis=-1, keepdims=True), jnp.max(s_w, axis=-1, keepdims=True)),
                        jnp.maximum(s_n, sink))
        e_m, e_w, e_n = jnp.exp(s_m - m), jnp.exp(s_w - m), jnp.exp(s_n - m)
        denom = (jnp.sum(e_m, axis=-1, keepdims=True) + jnp.sum(e_w, axis=-1, keepdims=True) + e_n
                 + jnp.exp(sink - m))
        o = (dot_pv(e_m.astype(BF16), vm_ref[:, :, hc].astype(BF16))
             + dot_pv(e_w.astype(BF16), vw_ref[:, :, hc].astype(BF16))
             + e_n.astype(BF16).astype(F32) * v_new)
        o_ref[:, h] = o / denom


def _attn_sample(q4, k_meta, k_win, k_new, v_meta, v_win, v_new, bias, sinks4, bb):
    b3 = lambda rows: pl.BlockSpec((bb, rows, KV_WIDTH), lambda i: (i, 0, 0))
    q_spec = pl.BlockSpec((bb, N_KV_HEADS, GROUP, HEAD_DIM), lambda i: (i, 0, 0, 0))
    win_shape = jax.ShapeDtypeStruct((DEC_BATCH, WINDOW, KV_WIDTH), F32)
    return pl.pallas_call(
        _attn_sample_kernel,
        grid=(DEC_BATCH // bb,),
        in_specs=[
            q_spec,
            b3(N_META), b3(WINDOW), b3(1),
            b3(N_META), b3(WINDOW), b3(1),
            pl.BlockSpec((N_KV_HEADS, GROUP, WINDOW), lambda i: (0, 0, 0)),
            pl.BlockSpec((N_KV_HEADS, GROUP, 1), lambda i: (0, 0, 0)),
        ],
        out_specs=[q_spec, b3(WINDOW), b3(WINDOW)],
        out_shape=[jax.ShapeDtypeStruct((DEC_BATCH, N_KV_HEADS, GROUP, HEAD_DIM), F32), win_shape, win_shape],
        compiler_params=_cparams(("arbitrary",), 5 * _nbytes((bb, WINDOW, KV_WIDTH), F32),
                                 4 * _nbytes((bb, WINDOW, KV_WIDTH), F32)),
        name="attn_sample",
    )(q4, k_meta, k_win, k_new, v_meta, v_win, v_new, bias, sinks4)


N_GATE_CHUNKS = D_MODEL // GATE_CHUNK


def _outproj_kernel(lru_ref, attn_ref, *refs, n_cast):
    gl_refs, ga_refs = refs[:N_GATE_CHUNKS], refs[N_GATE_CHUNKS:2 * N_GATE_CHUNKS]
    x_ref, g2_ref, wl_ref, wa_ref, wo_ref = refs[2 * N_GATE_CHUNKS:2 * N_GATE_CHUNKS + 5]
    rest = refs[2 * N_GATE_CHUNKS + 5:]
    cast_src, (h_ref, hn_ref), cast_dst = rest[:n_cast], rest[n_cast:n_cast + 2], rest[n_cast + 2:]
    l = jnp.dot(lru_ref[...], wl_ref[...], preferred_element_type=F32)
    a = jnp.dot(attn_ref[...], wa_ref[...], preferred_element_type=F32)
    merged = jnp.concatenate(
        [jax.nn.sigmoid(gl_refs[c][...]) * l[:, c * GATE_CHUNK:(c + 1) * GATE_CHUNK]
         + jax.nn.sigmoid(ga_refs[c][...]) * a[:, c * GATE_CHUNK:(c + 1) * GATE_CHUNK]
         for c in range(N_GATE_CHUNKS)], axis=1)
    h = x_ref[...] + jnp.dot(merged.astype(BF16), wo_ref[...], preferred_element_type=F32)
    h_ref[...] = h
    ms = jnp.mean(h * h, axis=-1, keepdims=True)
    hn_ref[...] = (h * lax.rsqrt(ms + EPS) * g2_ref[...]).astype(BF16)
    _cast_chunks(cast_src, cast_dst)


def _outproj(lru, attn, proj, x, g2, wl, wa, wo, cast_weights=()):
    m = x.shape[0]
    tm = _tile_plan(m)["outproj"]
    const = lambda shape: pl.BlockSpec(shape, lambda i: (0,) * len(shape))
    cast_specs, cast_shapes, cast_bytes = _cast_specs(cast_weights, lambda i: i)
    assert not cast_weights or m // tm >= CAST_STEPS
    tile = _nbytes((tm, D_MODEL), F32)
    windows = 6 * tile + _nbytes(wl.shape, BF16) + _nbytes(wa.shape, BF16) + _nbytes(wo.shape, BF16) + cast_bytes
    row_spec = lambda width: pl.BlockSpec((tm, width), lambda i: (i, 0))
    return pl.pallas_call(
        functools.partial(_outproj_kernel, n_cast=len(cast_weights)),
        grid=(m // tm,),
        in_specs=[
            row_spec(LRU_WIDTH),
            row_spec(ATTN_WIDTH),
            *[pl.BlockSpec((tm, GATE_CHUNK), functools.partial(lambda i, cb: (i, cb), cb=col // GATE_CHUNK + c))
              for col in (COL_GL, COL_GA) for c in range(N_GATE_CHUNKS)],
            row_spec(D_MODEL),
            const((1, D_MODEL)),
            const((LRU_WIDTH, D_MODEL)),
            const((ATTN_WIDTH, D_MODEL)),
            const((D_MODEL, D_MODEL)),
            *cast_specs,
        ],
        out_specs=[row_spec(D_MODEL), row_spec(D_MODEL), *cast_specs],
        out_shape=[jax.ShapeDtypeStruct((m, D_MODEL), F32), jax.ShapeDtypeStruct((m, D_MODEL), BF16), *cast_shapes],
        compiler_params=_cparams(("arbitrary",), windows, 3 * tile),
        name=f"outproj_{m}",
    )(lru, attn, *([proj] * (2 * N_GATE_CHUNKS)), x, g2, wl, wa, wo, *cast_weights)


def _mlp_kernel(h_ref, hn_ref, gf_ref, wu_ref, wd_ref, o_ref):
    f = pl.program_id(1)

    @pl.when(f == 0)
    def _():
        o_ref[...] = jnp.zeros(o_ref.shape, F32)

    u = jnp.dot(hn_ref[...], wu_ref[...], preferred_element_type=F32)
    u = jnp.square(jnp.maximum(u, 0.0))
    o_ref[...] += jnp.dot(u.astype(BF16), wd_ref[...], preferred_element_type=F32)

    @pl.when(f == pl.num_programs(1) - 1)
    def _():
        out = h_ref[...] + o_ref[...]
        ms = jnp.mean(out * out, axis=-1, keepdims=True)
        o_ref[...] = out * lax.rsqrt(ms + EPS) * gf_ref[...]


def _mlp(h, hn, gf, wu, wd):
    m = h.shape[0]
    tm, tf = _tile_plan(m)["mlp"]
    windows = (2 * _nbytes((tm, D_MODEL), F32) + _nbytes((tm, D_MODEL), BF16) + 2 * _nbytes((D_MODEL, tf), BF16))
    return pl.pallas_call(
        _mlp_kernel,
        grid=(m // tm, D_FF // tf),
        in_specs=[
            pl.BlockSpec((tm, D_MODEL), lambda i, f: (i, 0)),
            pl.BlockSpec((tm, D_MODEL), lambda i, f: (i, 0)),
            pl.BlockSpec((1, D_MODEL), lambda i, f: (0, 0)),
            pl.BlockSpec((D_MODEL, tf), lambda i, f: (0, f)),
            pl.BlockSpec((tf, D_MODEL), lambda i, f: (f, 0)),
        ],
        out_specs=pl.BlockSpec((tm, D_MODEL), lambda i, f: (i, 0)),
        out_shape=jax.ShapeDtypeStruct((m, D_MODEL), F32),
        compiler_params=_cparams(("arbitrary", "arbitrary"), windows, 2 * _nbytes((tm, tf), F32)),
        name=f"mlp_{m}",
    )(h, hn, gf, wu, wd)


def _sample_bias():
    dist = (WINDOW - np.arange(WINDOW)).astype(np.float32)
    slopes = np.asarray(_slopes(), np.float32).reshape(N_KV_HEADS, GROUP, 1)
    return jnp.asarray(-(slopes * dist[None, None, :]))


def kernel(x_prompt, x_sample, cache_meta_k, cache_meta_v, cache_win_k, cache_win_v, state_conv, state_h,
           meta_tokens, norm1_g, w_in, conv_w, conv_b, w_gate_x, b_gate_x, w_gate_a, b_gate_a, lru_a_param,
           attn_sinks, w_lru_out, w_attn_out, w_o, norm2_g, w_mlp_up, w_mlp_down, final_norm_g):
    row = lambda v: v.reshape(1, -1)
    g1, g2, gf = row(norm1_g[0]), row(norm2_g[0]), row(final_norm_g)
    w_in_b = w_in[0].astype(BF16)
    wgx, wga = w_gate_x[0].astype(BF16), w_gate_a[0].astype(BF16)
    cw, cb = conv_w[0], row(conv_b[0])
    bgx, bga, ap = row(b_gate_x[0]), row(b_gate_a[0]), row(lru_a_param[0])
    sinks = attn_sinks[0]

    x_main = x_prompt.reshape(SEQ, D_MODEL)
    x_extra = jnp.concatenate([
        x_sample.reshape(DEC_BATCH, D_MODEL),
        jnp.zeros((EXTRA_ROWS - DEC_BATCH - N_META, D_MODEL), F32),
        meta_tokens.astype(F32)], axis=0)

    proj_m, wl, wa, wo = _inproj(x_main, g1, w_in_b, (w_lru_out[0], w_attn_out[0], w_o[0]))
    proj_e, = _inproj(x_extra, g1, w_in_b)

    lru_e, h_sample, h_meta = _lru_extra(proj_e, state_conv[0].reshape(DEC_BATCH, (CONV_W - 1) * LRU_WIDTH),
                                         state_h[0], cw, cb, wgx, bgx, wga, bga, ap)
    tail = proj_e[EXTRA_ROWS - SUBLANES:, COL_XB:COL_XB + LRU_WIDTH]
    lru_m, h_last = _lru_main(proj_m, tail, h_meta, cw, cb, wgx, bgx, wga, bga, ap)

    k_meta = proj_e[META_ROW0:, COL_K:COL_K + KV_WIDTH]
    v_meta = proj_e[META_ROW0:, COL_V:COL_V + KV_WIDTH]
    attn_m = _attn_main(sinks, proj_m, k_meta, v_meta)
    attn_meta = _attn_meta(sinks, proj_e[META_ROW0:, COL_Q:COL_Q + ATTN_WIDTH], k_meta, v_meta)
    k_new = proj_e[:DEC_BATCH, None, COL_K:COL_K + KV_WIDTH]
    v_new = proj_e[:DEC_BATCH, None, COL_V:COL_V + KV_WIDTH]
    q4 = proj_e[:DEC_BATCH, COL_Q:COL_Q + ATTN_WIDTH].reshape(DEC_BATCH, N_KV_HEADS, GROUP, HEAD_DIM)
    flat = lambda c, n: c[0].reshape(DEC_BATCH, n, KV_WIDTH)
    attn_s, kw_out, vw_out = _attn_sample(
        q4, flat(cache_meta_k, N_META), flat(cache_win_k, WINDOW), k_new,
        flat(cache_meta_v, N_META), flat(cache_win_v, WINDOW), v_new,
        _sample_bias(), sinks.reshape(N_KV_HEADS, GROUP, 1), bb=8)
    attn_e = jnp.concatenate([
        attn_s.reshape(DEC_BATCH, ATTN_WIDTH),
        jnp.zeros((EXTRA_ROWS - DEC_BATCH - N_META, ATTN_WIDTH), F32),
        attn_meta], axis=0).astype(BF16)

    res_m, resn_m, wu, wd = _outproj(lru_m, attn_m, proj_m, x_main, g2, wl, wa, wo, (w_mlp_up[0], w_mlp_down[0]))
    res_e, resn_e = _outproj(lru_e, attn_e, proj_e, x_extra, g2, wl, wa, wo)
    y_m = _mlp(res_m, resn_m, gf, wu, wd)
    y_e = _mlp(res_e, resn_e, gf, wu, wd)

    kv5 = lambda a, n: a.reshape(1, -1, n, N_KV_HEADS, HEAD_DIM)
    return (
        y_m.reshape(1, SEQ, D_MODEL),
        y_e[:DEC_BATCH].reshape(DEC_BATCH, 1, D_MODEL),
        kv5(k_meta, N_META), kv5(v_meta, N_META),
        kv5(proj_m[SEQ - WINDOW:, COL_K:COL_K + KV_WIDTH], WINDOW),
        kv5(proj_m[SEQ - WINDOW:, COL_V:COL_V + KV_WIDTH], WINDOW),
        proj_m[SEQ - (CONV_W - 1):, COL_XB:COL_XB + LRU_WIDTH].reshape(1, 1, CONV_W - 1, LRU_WIDTH),
        h_last.reshape(1, 1, LRU_WIDTH),
        kv5(kw_out, WINDOW), kv5(vw_out, WINDOW),
        jnp.concatenate([state_conv[0][:, 1:], proj_e[:DEC_BATCH, None, COL_XB:COL_XB + LRU_WIDTH]], axis=1)[None],
        h_sample[None],
    )
```

```python
import functools

import numpy as np
import jax
import jax.numpy as jnp
from jax import lax
from jax.experimental import pallas as pl
from jax.experimental.pallas import tpu as pltpu

D_MODEL = 2048
SEQ = 16384
DEC_BATCH = 128
N_META = 16
LRU_WIDTH = 1024
N_LRU_BLOCKS = 8
LRU_BLOCK = LRU_WIDTH // N_LRU_BLOCKS
CONV_W = 4
LRU_C = 8.0
N_HEADS = 16
N_KV_HEADS = 4
HEAD_DIM = 64
GROUP = N_HEADS // N_KV_HEADS
ATTN_WIDTH = N_HEADS * HEAD_DIM
KV_WIDTH = N_KV_HEADS * HEAD_DIM
WINDOW = 128
BLOCK = 128
D_FF = 4 * D_MODEL
EPS = 1e-6
NEG_INF = -1e30
IN_WIDTH = 2 * LRU_WIDTH + ATTN_WIDTH + 2 * KV_WIDTH + 2 * D_MODEL
COL_XB, COL_YB, COL_Q = 0, LRU_WIDTH, 2 * LRU_WIDTH
COL_K = COL_Q + ATTN_WIDTH
COL_V = COL_K + KV_WIDTH
COL_GL = COL_V + KV_WIDTH
COL_GA = COL_GL + D_MODEL
GATE_CHUNK = 512

EXTRA_ROWS = 256
META_ROW0 = EXTRA_ROWS - N_META
SUBLANES = 8
LANES = 128
MIB = 1024 * 1024

F32 = jnp.float32
BF16 = jnp.bfloat16


def _slopes():
    return [2.0 ** (-8.0 * (h + 1) / N_HEADS) for h in range(N_HEADS)]


V7X_VMEM_REQUEST_CAP = 60 * MIB


def _nbytes(shape, dtype):
    return int(np.prod(shape)) * jnp.dtype(dtype).itemsize


def _cparams(sem, pipelined, resident=0):
    estimate = 2 * pipelined + resident
    limit = min(V7X_VMEM_REQUEST_CAP, estimate + estimate // 4 + 2 * MIB)
    return pltpu.CompilerParams(dimension_semantics=sem, vmem_limit_bytes=limit)


def _tile_plan(rows):
    if rows == EXTRA_ROWS:
        return dict(inproj=(EXTRA_ROWS, 1536), outproj=EXTRA_ROWS, mlp=(EXTRA_ROWS, 512))
    return dict(inproj=(1024, 1536), lru=512, outproj=256, mlp=(1024, 512))


CAST_STEPS = 64


def _cast_specs(weights, step_of):
    specs, shapes, nbytes = [], [], 0
    for w in weights:
        block = (w.shape[0] // CAST_STEPS, w.shape[1])
        specs.append(pl.BlockSpec(block, lambda *idx: (jnp.minimum(step_of(*idx), CAST_STEPS - 1), 0)))
        shapes.append(jax.ShapeDtypeStruct(w.shape, BF16))
        nbytes += _nbytes(block, F32) + _nbytes(block, BF16)
    return specs, shapes, nbytes


def _cast_chunks(src_refs, dst_refs):
    for src_ref, dst_ref in zip(src_refs, dst_refs):
        dst_ref[...] = src_ref[...].astype(BF16)


def _inproj_kernel(x_ref, g_ref, w_ref, *refs, n_cast):
    cast_src, o_ref, cast_dst, xs_ref = refs[:n_cast], refs[n_cast], refs[n_cast + 1:-1], refs[-1]

    @pl.when(pl.program_id(1) == 0)
    def _():
        x = x_ref[...]
        ms = jnp.mean(x * x, axis=-1, keepdims=True)
        xs_ref[...] = (x * lax.rsqrt(ms + EPS) * g_ref[...]).astype(BF16)

    o_ref[...] = jnp.dot(xs_ref[...], w_ref[...], preferred_element_type=F32)
    _cast_chunks(cast_src, cast_dst)


def _inproj(x, g, w, cast_weights=()):
    m = x.shape[0]
    tm, tn = _tile_plan(m)["inproj"]
    n_col = IN_WIDTH // tn
    cast_specs, cast_shapes, cast_bytes = _cast_specs(cast_weights, lambda i, j: i * n_col + j)
    assert not cast_weights or (m // tm) * n_col >= CAST_STEPS
    windows = _nbytes((tm, D_MODEL), F32) + _nbytes((D_MODEL, tn), BF16) + _nbytes((tm, tn), F32) + cast_bytes
    return pl.pallas_call(
        functools.partial(_inproj_kernel, n_cast=len(cast_weights)),
        grid=(m // tm, n_col),
        in_specs=[
            pl.BlockSpec((tm, D_MODEL), lambda i, j: (i, 0)),
            pl.BlockSpec((1, D_MODEL), lambda i, j: (0, 0)),
            pl.BlockSpec((D_MODEL, tn), lambda i, j: (0, j)),
            *cast_specs,
        ],
        out_specs=[pl.BlockSpec((tm, tn), lambda i, j: (i, j)), *cast_specs],
        out_shape=[jax.ShapeDtypeStruct((m, IN_WIDTH), F32), *cast_shapes],
        scratch_shapes=[pltpu.VMEM((tm, D_MODEL), BF16)],
        compiler_params=_cparams(("arbitrary", "arbitrary"), windows, _nbytes((tm, D_MODEL), BF16)),
        name=f"inproj_{m}",
    )(x, g, w, *cast_weights)


def _gate_ab(xc_n, n, wgx_ref, bgx_ref, wga_ref, bga_ref, ap_ref):
    cols = slice(n * LRU_BLOCK, (n + 1) * LRU_BLOCK)
    xcb = xc_n.astype(BF16)
    gx = jnp.dot(xcb, wgx_ref[n], preferred_element_type=F32) + bgx_ref[:, cols]
    ga = jnp.dot(xcb, wga_ref[n], preferred_element_type=F32) + bga_ref[:, cols]
    gate_x = jax.nn.sigmoid(gx)
    gate_a = jax.nn.sigmoid(ga)
    log_a = -LRU_C * gate_a * jax.nn.softplus(-ap_ref[:, cols])
    a = jnp.exp(log_a)
    z = -jnp.tanh(log_a) * (a * a + 1.0)
    root = jnp.where(z > 0.0, z * lax.rsqrt(z), 0.0)
    b = root * gate_x * xc_n
    return a, b


def _lru_rows(rows, ext_ref, a3_ref, b3_ref, h3_ref, xb, yb_ref, cw_ref, cb_ref,
              wgx_ref, bgx_ref, wga_ref, bga_ref, ap_ref, lru_ref, h_init, g_lo, g_hi):
    groups = rows // SUBLANES
    ext_ref[SUBLANES:SUBLANES + rows, :] = xb
    x_ext = ext_ref[...]
    xc = x_ext * cw_ref[0:1, :]
    for t in range(1, CONV_W):
        xc = pltpu.roll(xc, 1, axis=0) + x_ext * cw_ref[t:t + 1, :]
    xc = xc[SUBLANES:, :] + cb_ref[...]
    for n in range(N_LRU_BLOCKS):
        a, b = _gate_ab(xc[:, n * LRU_BLOCK:(n + 1) * LRU_BLOCK], n, wgx_ref, bgx_ref, wga_ref, bga_ref, ap_ref)
        a3_ref[:, n * SUBLANES:(n + 1) * SUBLANES, :] = a.reshape(groups, SUBLANES, LANES)
        b3_ref[:, n * SUBLANES:(n + 1) * SUBLANES, :] = b.reshape(groups, SUBLANES, LANES)

    def group_step(i, h):
        for r in range(SUBLANES):
            step = pl.ds(r, N_LRU_BLOCKS, stride=SUBLANES)
            h = a3_ref[i, step, :] * h + b3_ref[i, step, :]
            h3_ref[i, step, :] = h
        return h

    h = lax.fori_loop(g_lo, g_hi, group_step, h_init)
    for n in range(N_LRU_BLOCKS):
        cols = slice(n * LRU_BLOCK, (n + 1) * LRU_BLOCK)
        hs = h3_ref[:, n * SUBLANES:(n + 1) * SUBLANES, :].reshape(rows, LANES)
        lru_ref[:, cols] = (jax.nn.gelu(yb_ref[:, cols]) * hs).astype(BF16)
    return h


def _lru_main_kernel(xb_ref, yb_ref, tail_ref, h0_ref, cw_ref, cb_ref, wgx_ref, bgx_ref, wga_ref, bga_ref,
                     ap_ref, lru_ref, hlast_ref, ext_ref, a3_ref, b3_ref, h3_ref, hstate_ref, *, rows):
    @pl.when(pl.program_id(0) == 0)
    def _():
        ext_ref[0:SUBLANES, :] = tail_ref[...]
        hstate_ref[...] = h0_ref[...]

    h = _lru_rows(rows, ext_ref, a3_ref, b3_ref, h3_ref, xb_ref[...], yb_ref, cw_ref, cb_ref,
                  wgx_ref, bgx_ref, wga_ref, bga_ref, ap_ref, lru_ref, hstate_ref[...], 0, rows // SUBLANES)
    hstate_ref[...] = h
    hlast_ref[...] = h
    ext_ref[0:SUBLANES, :] = ext_ref[rows:rows + SUBLANES, :]


def _lru_main(proj, tail, h0, cw, cb, wgx, bgx, wga, bga, ap):
    m = proj.shape[0]
    tt = _tile_plan(m)["lru"]
    const = lambda shape: pl.BlockSpec(shape, lambda i: (0,) * len(shape))
    scan_shape = (tt // SUBLANES, N_LRU_BLOCKS * SUBLANES, LANES)
    tile = _nbytes((tt, LRU_WIDTH), F32)
    windows = 3 * tile + 2 * _nbytes(wgx.shape, BF16)
    return pl.pallas_call(
        functools.partial(_lru_main_kernel, rows=tt),
        grid=(m // tt,),
        in_specs=[
            pl.BlockSpec((tt, LRU_WIDTH), lambda i: (i, COL_XB // LRU_WIDTH)),
            pl.BlockSpec((tt, LRU_WIDTH), lambda i: (i, COL_YB // LRU_WIDTH)),
            const((SUBLANES, LRU_WIDTH)),
            const((SUBLANES, LANES)),
            const((CONV_W, LRU_WIDTH)),
            const((1, LRU_WIDTH)),
            const((N_LRU_BLOCKS, LRU_BLOCK, LRU_BLOCK)),
            const((1, LRU_WIDTH)),
            const((N_LRU_BLOCKS, LRU_BLOCK, LRU_BLOCK)),
            const((1, LRU_WIDTH)),
            const((1, LRU_WIDTH)),
        ],
        out_specs=[
            pl.BlockSpec((tt, LRU_WIDTH), lambda i: (i, 0)),
            const((SUBLANES, LANES)),
        ],
        out_shape=[
            jax.ShapeDtypeStruct((m, LRU_WIDTH), BF16),
            jax.ShapeDtypeStruct((SUBLANES, LANES), F32),
        ],
        scratch_shapes=[
            pltpu.VMEM((tt + SUBLANES, LRU_WIDTH), F32),
            pltpu.VMEM(scan_shape, F32),
            pltpu.VMEM(scan_shape, F32),
            pltpu.VMEM(scan_shape, F32),
            pltpu.VMEM((SUBLANES, LANES), F32),
        ],
        compiler_params=_cparams(("arbitrary",), windows, 8 * tile),
        name="lru_main",
    )(proj, proj, tail, h0, cw, cb, wgx, bgx, wga, bga, ap)


def _lru_extra_kernel(xb_ref, yb_ref, sconv_ref, sh_ref, cw_ref, cb_ref, wgx_ref, bgx_ref, wga_ref, bga_ref,
                      ap_ref, lru_ref, hs_ref, hmeta_ref, ext_ref, a3_ref, b3_ref, h3_ref):
    nb = DEC_BATCH
    xb = xb_ref[0:nb, :]
    xc = sconv_ref[:, 0:LRU_WIDTH] * cw_ref[0:1, :]
    xc = xc + sconv_ref[:, LRU_WIDTH:2 * LRU_WIDTH] * cw_ref[1:2, :]
    xc = xc + sconv_ref[:, 2 * LRU_WIDTH:3 * LRU_WIDTH] * cw_ref[2:3, :]
    xc = xc + xb * cw_ref[3:4, :]
    xc = xc + cb_ref[...]
    for n in range(N_LRU_BLOCKS):
        cols = slice(n * LRU_BLOCK, (n + 1) * LRU_BLOCK)
        a, b = _gate_ab(xc[:, cols], n, wgx_ref, bgx_ref, wga_ref, bga_ref, ap_ref)
        h = a * sh_ref[:, cols] + b
        hs_ref[:, cols] = h
        lru_ref[0:nb, cols] = (jax.nn.gelu(yb_ref[0:nb, cols]) * h).astype(BF16)
    rows = EXTRA_ROWS - nb
    ext_ref[0:SUBLANES, :] = jnp.zeros((SUBLANES, LRU_WIDTH), F32)
    h3_ref[...] = jnp.zeros(h3_ref.shape, F32)
    first_group = (rows - N_META) // SUBLANES
    hmeta_ref[...] = _lru_rows(rows, ext_ref, a3_ref, b3_ref, h3_ref, xb_ref[nb:, :], yb_ref.at[nb:, :], cw_ref,
                               cb_ref, wgx_ref, bgx_ref, wga_ref, bga_ref, ap_ref, lru_ref.at[nb:, :],
                               jnp.zeros((SUBLANES, LANES), F32), first_group, rows // SUBLANES)


def _lru_extra(proj_e, sconv, sh, cw, cb, wgx, bgx, wga, bga, ap):
    const = lambda shape: pl.BlockSpec(shape, lambda i: (0,) * len(shape))
    rows = EXTRA_ROWS - DEC_BATCH
    scan_shape = (rows // SUBLANES, N_LRU_BLOCKS * SUBLANES, LANES)
    return pl.pallas_call(
        _lru_extra_kernel,
        grid=(1,),
        in_specs=[
            pl.BlockSpec((EXTRA_ROWS, LRU_WIDTH), lambda i: (0, COL_XB // LRU_WIDTH)),
            pl.BlockSpec((EXTRA_ROWS, LRU_WIDTH), lambda i: (0, COL_YB // LRU_WIDTH)),
            const((DEC_BATCH, (CONV_W - 1) * LRU_WIDTH)),
            const((DEC_BATCH, LRU_WIDTH)),
            const((CONV_W, LRU_WIDTH)),
            const((1, LRU_WIDTH)),
            const((N_LRU_BLOCKS, LRU_BLOCK, LRU_BLOCK)),
            const((1, LRU_WIDTH)),
            const((N_LRU_BLOCKS, LRU_BLOCK, LRU_BLOCK)),
            const((1, LRU_WIDTH)),
            const((1, LRU_WIDTH)),
        ],
        out_specs=[
            const((EXTRA_ROWS, LRU_WIDTH)),
            const((DEC_BATCH, LRU_WIDTH)),
            const((SUBLANES, LANES)),
        ],
        out_shape=[
            jax.ShapeDtypeStruct((EXTRA_ROWS, LRU_WIDTH), BF16),
            jax.ShapeDtypeStruct((DEC_BATCH, LRU_WIDTH), F32),
            jax.ShapeDtypeStruct((SUBLANES, LANES), F32),
        ],
        scratch_shapes=[
            pltpu.VMEM((rows + SUBLANES, LRU_WIDTH), F32),
            pltpu.VMEM(scan_shape, F32),
            pltpu.VMEM(scan_shape, F32),
            pltpu.VMEM(scan_shape, F32),
        ],
        compiler_params=_cparams(("arbitrary",), 8 * _nbytes((DEC_BATCH, LRU_WIDTH), F32),
                                 8 * _nbytes((rows, LRU_WIDTH), F32)),
        name="lru_extra",
    )(proj_e, proj_e, sconv, sh, cw, cb, wgx, bgx, wga, bga, ap)


def _attend_kv_head(q_h, kk, vv, bias, sinks):
    return _attend_values(*_attend_weights(q_h, kk, bias, sinks), vv)


def _attend_weights(q_h, kk, bias, sinks):
    r = q_h.shape[0]
    qs = jnp.concatenate([q_h[:, g * HEAD_DIM:(g + 1) * HEAD_DIM] for g in range(GROUP)], axis=0)
    qs = (qs * (HEAD_DIM ** -0.5)).astype(BF16)
    s = lax.dot_general(kk.astype(BF16), qs, (((1,), (1,)), ((), ())), preferred_element_type=F32) + bias
    sink = jnp.concatenate([jnp.full((1, r), sinks[g], F32) for g in range(GROUP)], axis=1)
    m = jnp.maximum(jnp.max(s, axis=0, keepdims=True), sink)
    e = jnp.exp(s - m)
    denom = jnp.sum(e, axis=0, keepdims=True) + jnp.exp(sink - m)
    return e.astype(BF16), denom


def _attend_values(e, denom, vv):
    r = e.shape[1] // GROUP
    o_t = jnp.dot(vv.T.astype(BF16), e, preferred_element_type=F32) / denom
    pairs = [jnp.concatenate([o_t[:, g * r:(g + 1) * r], o_t[:, (g + 1) * r:(g + 2) * r]], axis=0).T
             for g in range(0, GROUP, 2)]
    return jnp.concatenate(pairs, axis=-1)


ATTN_KEYS = N_META + 2 * BLOCK


ATTN_BLOCKS_PER_STEP = 2


ATTN_BIAS_SHAPE = (2, N_KV_HEADS, ATTN_KEYS, GROUP * BLOCK)


def _build_attn_bias(bias_ref):
    slopes = _slopes()
    key = lax.broadcasted_iota(jnp.int32, (ATTN_KEYS, BLOCK), 0)
    qry = lax.broadcasted_iota(jnp.int32, (ATTN_KEYS, BLOCK), 1)
    d = qry + BLOCK - (key - N_META)
    dist = jnp.where(key < N_META, 0, d).astype(F32)
    for has_prev in range(2):
        band_ok = (d >= 0) & (d <= WINDOW) & ((key >= N_META + BLOCK) | bool(has_prev))
        valid = (key < N_META) | band_ok
        for hd in range(N_HEADS):
            lanes = slice((hd % GROUP) * BLOCK, (hd % GROUP + 1) * BLOCK)
            bias_ref[has_prev, hd // GROUP, :, lanes] = jnp.where(valid, -(slopes[hd] * dist), NEG_INF)


def _attend_stages(tile, sink_ref, q_ref, kc_ref, kp_ref, vc_ref, vp_ref, km_ref, vm_ref, bias_ref, o_ref):
    def pair(sub, h):
        rows = slice(sub * BLOCK, (sub + 1) * BLOCK)
        prev_k, prev_v = (kp_ref, vp_ref) if sub == 0 else (kc_ref.at[(sub - 1) * BLOCK:sub * BLOCK],
                                                             vc_ref.at[(sub - 1) * BLOCK:sub * BLOCK])
        table = jnp.where(tile == 0, 0, 1) if sub == 0 else 1
        hc = slice(h * HEAD_DIM, (h + 1) * HEAD_DIM)
        qc = slice(h * GROUP * HEAD_DIM, (h + 1) * GROUP * HEAD_DIM)
        v = {}

        def weights():
            kk = jnp.concatenate([km_ref[:, hc], prev_k[:, hc], kc_ref[rows, hc]], axis=0)
            v["w"] = _attend_weights(q_ref[rows, qc], kk, bias_ref[table, h],
                                     [sink_ref[h * GROUP + g] for g in range(GROUP)])

        def values():
            vv = jnp.concatenate([vm_ref[:, hc], prev_v[:, hc], vc_ref[rows, hc]], axis=0)
            o_ref[rows, qc] = _attend_values(*v["w"], vv).astype(BF16)

        return weights, values

    return [pair(sub, h) for sub in range(ATTN_BLOCKS_PER_STEP) for h in range(N_KV_HEADS)]


def _attn_meta_kernel(sink_ref, q_ref, km_ref, vm_ref, o_ref):
    key = lax.broadcasted_iota(jnp.int32, (N_META, GROUP * N_META), 0)
    qry = lax.broadcasted_iota(jnp.int32, (N_META, GROUP * N_META), 1) % N_META
    bias = jnp.where(qry >= key, 0.0, NEG_INF)
    for h in range(N_KV_HEADS):
        hc = slice(h * HEAD_DIM, (h + 1) * HEAD_DIM)
        q_h = q_ref[:, h * GROUP * HEAD_DIM:(h + 1) * GROUP * HEAD_DIM]
        o = _attend_kv_head(q_h, km_ref[:, hc], vm_ref[:, hc], bias,
                            [sink_ref[h * GROUP + g] for g in range(GROUP)])
        o_ref[:, h * GROUP * HEAD_DIM:(h + 1) * GROUP * HEAD_DIM] = o


def _attn_meta(sinks, q_meta, k_meta, v_meta):
    const = lambda shape: pl.BlockSpec(shape, lambda i: (0,) * len(shape))
    return pl.pallas_call(
        _attn_meta_kernel,
        grid=(1,),
        in_specs=[pl.BlockSpec(memory_space=pltpu.SMEM), const((N_META, ATTN_WIDTH)),
                  const((N_META, KV_WIDTH)), const((N_META, KV_WIDTH))],
        out_specs=const((N_META, ATTN_WIDTH)),
        out_shape=jax.ShapeDtypeStruct((N_META, ATTN_WIDTH), F32),
        compiler_params=_cparams(("arbitrary",), 4 * _nbytes((N_META, ATTN_WIDTH), F32)),
        name="attn_meta",
    )(sinks, q_meta, k_meta, v_meta)


def _attn_sample_kernel(q_ref, km_ref, kw_ref, kn_ref, vm_ref, vw_ref, vn_ref, bias_ref, sink_ref,
                        o_ref, kwo_ref, vwo_ref):
    for src_ref, new_ref, dst_ref in ((kw_ref, kn_ref, kwo_ref), (vw_ref, vn_ref, vwo_ref)):
        dst_ref[:, 0:WINDOW - 1, :] = src_ref[:, 1:WINDOW, :]
        dst_ref[:, WINDOW - 1:WINDOW, :] = new_ref[...]
    dot_qk = functools.partial(jnp.einsum, "bgd,bjd->bgj", preferred_element_type=F32)
    dot_pv = functools.partial(jnp.einsum, "bgj,bjd->bgd", preferred_element_type=F32)
    for h in range(N_KV_HEADS):
        hc = slice(h * HEAD_DIM, (h + 1) * HEAD_DIM)
        q = (q_ref[:, h] * (HEAD_DIM ** -0.5)).astype(BF16)
        k_new = kn_ref[:, :, hc].astype(BF16).astype(F32)
        v_new = vn_ref[:, :, hc].astype(BF16).astype(F32)
        s_m = dot_qk(q, km_ref[:, :, hc].astype(BF16))
        s_w = dot_qk(q, kw_ref[:, :, hc].astype(BF16)) + bias_ref[h]
        s_n = jnp.sum(q.astype(F32) * k_new, axis=-1, keepdims=True)
        sink = sink_ref[h]
        m = jnp.maximum(jnp.maximum(jnp.max(s_m, axis=-1, keepdims=True), jnp.max(s_w, axis=-1, keepdims=True)),
                        jnp.maximum(s_n, sink))
        e_m, e_w, e_n = jnp.exp(s_m - m), jnp.exp(s_w - m), jnp.exp(s_n - m)
        denom = (jnp.sum(e_m, axis=-1, keepdims=True) + jnp.sum(e_w, axis=-1, keepdims=True) + e_n
                 + jnp.exp(sink - m))
        o = (dot_pv(e_m.astype(BF16), vm_ref[:, :, hc].astype(BF16))
             + dot_pv(e_w.astype(BF16), vw_ref[:, :, hc].astype(BF16))
             + e_n.astype(BF16).astype(F32) * v_new)
        o_ref[:, h] = o / denom


def _attn_sample(q4, k_meta, k_win, k_new, v_meta, v_win, v_new, bias, sinks4, bb):
    b3 = lambda rows: pl.BlockSpec((bb, rows, KV_WIDTH), lambda i: (i, 0, 0))
    q_spec = pl.BlockSpec((bb, N_KV_HEADS, GROUP, HEAD_DIM), lambda i: (i, 0, 0, 0))
    win_shape = jax.ShapeDtypeStruct((DEC_BATCH, WINDOW, KV_WIDTH), F32)
    return pl.pallas_call(
        _attn_sample_kernel,
        grid=(DEC_BATCH // bb,),
        in_specs=[
            q_spec,
            b3(N_META), b3(WINDOW), b3(1),
            b3(N_META), b3(WINDOW), b3(1),
            pl.BlockSpec((N_KV_HEADS, GROUP, WINDOW), lambda i: (0, 0, 0)),
            pl.BlockSpec((N_KV_HEADS, GROUP, 1), lambda i: (0, 0, 0)),
        ],
        out_specs=[q_spec, b3(WINDOW), b3(WINDOW)],
        out_shape=[jax.ShapeDtypeStruct((DEC_BATCH, N_KV_HEADS, GROUP, HEAD_DIM), F32), win_shape, win_shape],
        compiler_params=_cparams(("arbitrary",), 5 * _nbytes((bb, WINDOW, KV_WIDTH), F32),
                                 4 * _nbytes((bb, WINDOW, KV_WIDTH), F32)),
        name="attn_sample",
    )(q4, k_meta, k_win, k_new, v_meta, v_win, v_new, bias, sinks4)


N_GATE_CHUNKS = D_MODEL // GATE_CHUNK


def _outproj_stages(lru_ref, attn_ref, gl_refs, ga_refs, x_ref, g2_ref, wl_ref, wa_ref, wo_ref, h_ref, hn_ref):
    v = {"a": [], "l": [], "ss": 0.0}
    half = D_MODEL // 2

    def branch_proj(key, src_ref, w_ref, part):
        if part == 0:
            v[key + "_in"] = src_ref[...]
        v[key].append(jnp.dot(v[key + "_in"], w_ref[:, part * half:(part + 1) * half], preferred_element_type=F32))

    def merge():
        a, l = (jnp.concatenate(v[key], axis=1) for key in ("a", "l"))
        v["m"] = jnp.concatenate(
            [jax.nn.sigmoid(gl_refs[c][...]) * l[:, c * GATE_CHUNK:(c + 1) * GATE_CHUNK]
             + jax.nn.sigmoid(ga_refs[c][...]) * a[:, c * GATE_CHUNK:(c + 1) * GATE_CHUNK]
             for c in range(N_GATE_CHUNKS)], axis=1).astype(BF16)

    def out_chunk(c):
        cols = slice(c * GATE_CHUNK, (c + 1) * GATE_CHUNK)
        h = x_ref[:, cols] + jnp.dot(v["m"], wo_ref[:, cols], preferred_element_type=F32)
        h_ref[:, cols] = h
        v["ss"] = v["ss"] + jnp.sum(h * h, axis=-1, keepdims=True)

    def norm():
        scale = lax.rsqrt(v["ss"] * (1.0 / D_MODEL) + EPS)
        hn_ref[...] = (h_ref[...] * scale * g2_ref[...]).astype(BF16)

    def seq(*stages):
        return lambda: [stage() for stage in stages]

    return [functools.partial(branch_proj, "a", attn_ref, wa_ref, 0),
            functools.partial(branch_proj, "a", attn_ref, wa_ref, 1),
            functools.partial(branch_proj, "l", lru_ref, wl_ref, 0),
            functools.partial(branch_proj, "l", lru_ref, wl_ref, 1),
            seq(merge, functools.partial(out_chunk, 0)),
            functools.partial(out_chunk, 1),
            functools.partial(out_chunk, 2),
            seq(functools.partial(out_chunk, 3), norm)]


def _run_interleaved(project, attend):
    for i in range(max(len(project), len(attend))):
        if i < len(attend):
            attend[i][0]()
        if i < len(project):
            project[i]()
        if i < len(attend):
            attend[i][1]()


def _split_outproj_refs(refs, n_cast):
    n = 2 * N_GATE_CHUNKS
    tile_refs = (refs[:N_GATE_CHUNKS], refs[N_GATE_CHUNKS:n], *refs[n:n + 5])
    rest = refs[n + 5:]
    cast_src, outs, cast_dst = rest[:n_cast], rest[n_cast:n_cast + 2], rest[n_cast + 2:2 * n_cast + 2]
    return tile_refs, cast_src, outs, cast_dst, rest[2 * n_cast + 2:]


def _outproj_kernel(lru_ref, attn_ref, *refs, n_cast):
    tile_refs, cast_src, outs, cast_dst, _ = _split_outproj_refs(refs, n_cast)
    _run_interleaved(_outproj_stages(lru_ref, attn_ref, *tile_refs, *outs), [])
    _cast_chunks(cast_src, cast_dst)


def _attn_outproj_kernel(sink_ref, q_ref, kc_ref, kp_ref, vc_ref, vp_ref, km_ref, vm_ref, lru_ref, *refs, n_cast):
    tile_refs, cast_src, outs, cast_dst, (bias_ref, attn_ref) = _split_outproj_refs(refs, n_cast)
    t = pl.program_id(0)
    last = pl.num_programs(0) - 1

    attend = lambda: _attend_stages(t, sink_ref, q_ref, kc_ref, kp_ref, vc_ref, vp_ref, km_ref, vm_ref,
                                    bias_ref, attn_ref)
    project = lambda: _outproj_stages(lru_ref, attn_ref, *tile_refs, *outs)

    @pl.when(t == 0)
    def _():
        _build_attn_bias(bias_ref)
        _run_interleaved([], attend())

    @pl.when((t > 0) & (t < last))
    def _():
        _run_interleaved(project(), attend())
        _cast_chunks(cast_src, cast_dst)

    @pl.when(t == last)
    def _():
        _run_interleaved(project(), [])
        _cast_chunks(cast_src, cast_dst)


def _attn_outproj(sinks, proj, k_meta, v_meta, lru, x, g2, wl, wa, wo, cast_weights):
    m = x.shape[0]
    tm = ATTN_BLOCKS_PER_STEP * BLOCK
    tiles = m // tm
    assert tiles >= CAST_STEPS
    cur = lambda t: jnp.minimum(t, tiles - 1)
    prv = lambda t: jnp.maximum(t - 1, 0)
    const = lambda shape: pl.BlockSpec(shape, lambda t: (0,) * len(shape))
    kv = lambda col: (pl.BlockSpec((tm, KV_WIDTH), lambda t: (cur(t), col // KV_WIDTH)),
                      pl.BlockSpec((BLOCK, KV_WIDTH),
                                   lambda t: (jnp.maximum(ATTN_BLOCKS_PER_STEP * cur(t) - 1, 0), col // KV_WIDTH)))
    kc, kp = kv(COL_K)
    vc, vp = kv(COL_V)
    prv_spec = lambda width, cb=0: pl.BlockSpec((tm, width), lambda t: (prv(t), cb))
    cast_specs, cast_shapes, cast_bytes = _cast_specs(cast_weights, prv)
    tile = _nbytes((tm, D_MODEL), F32)
    windows = 8 * tile + _nbytes(wl.shape, BF16) + _nbytes(wa.shape, BF16) + _nbytes(wo.shape, BF16) + cast_bytes
    return pl.pallas_call(
        functools.partial(_attn_outproj_kernel, n_cast=len(cast_weights)),
        grid=(tiles + 1,),
        in_specs=[
            pl.BlockSpec(memory_space=pltpu.SMEM),
            pl.BlockSpec((tm, ATTN_WIDTH), lambda t: (cur(t), COL_Q // ATTN_WIDTH)),
            kc, kp, vc, vp,
            const((N_META, KV_WIDTH)),
            const((N_META, KV_WIDTH)),
            prv_spec(LRU_WIDTH),
            *[prv_spec(GATE_CHUNK, col // GATE_CHUNK + c) for col in (COL_GL, COL_GA) for c in range(N_GATE_CHUNKS)],
            prv_spec(D_MODEL),
            const((1, D_MODEL)),
            const((LRU_WIDTH, D_MODEL)),
            const((ATTN_WIDTH, D_MODEL)),
            const((D_MODEL, D_MODEL)),
            *cast_specs,
        ],
        out_specs=[prv_spec(D_MODEL), prv_spec(D_MODEL), *cast_specs],
        out_shape=[jax.ShapeDtypeStruct((m, D_MODEL), F32), jax.ShapeDtypeStruct((m, D_MODEL), BF16), *cast_shapes],
        scratch_shapes=[pltpu.VMEM(ATTN_BIAS_SHAPE, F32), pltpu.VMEM((tm, ATTN_WIDTH), BF16)],
        compiler_params=_cparams(("arbitrary",), windows, 3 * tile + 2 * _nbytes(ATTN_BIAS_SHAPE, F32)),
        name="attn_outproj",
    )(sinks, proj, proj, proj, proj, proj, k_meta, v_meta, lru, *([proj] * (2 * N_GATE_CHUNKS)), x, g2, wl, wa, wo,
      *cast_weights)


def _outproj(lru, attn, proj, x, g2, wl, wa, wo, cast_weights=()):
    m = x.shape[0]
    tm = _tile_plan(m)["outproj"]
    const = lambda shape: pl.BlockSpec(shape, lambda i: (0,) * len(shape))
    cast_specs, cast_shapes, cast_bytes = _cast_specs(cast_weights, lambda i: i)
    assert not cast_weights or m // tm >= CAST_STEPS
    tile = _nbytes((tm, D_MODEL), F32)
    windows = 6 * tile + _nbytes(wl.shape, BF16) + _nbytes(wa.shape, BF16) + _nbytes(wo.shape, BF16) + cast_bytes
    row_spec = lambda width: pl.BlockSpec((tm, width), lambda i: (i, 0))
    return pl.pallas_call(
        functools.partial(_outproj_kernel, n_cast=len(cast_weights)),
        grid=(m // tm,),
        in_specs=[
            row_spec(LRU_WIDTH),
            row_spec(ATTN_WIDTH),
            *[pl.BlockSpec((tm, GATE_CHUNK), functools.partial(lambda i, cb: (i, cb), cb=col // GATE_CHUNK + c))
              for col in (COL_GL, COL_GA) for c in range(N_GATE_CHUNKS)],
            row_spec(D_MODEL),
            const((1, D_MODEL)),
            const((LRU_WIDTH, D_MODEL)),
            const((ATTN_WIDTH, D_MODEL)),
            const((D_MODEL, D_MODEL)),
            *cast_specs,
        ],
        out_specs=[row_spec(D_MODEL), row_spec(D_MODEL), *cast_specs],
        out_shape=[jax.ShapeDtypeStruct((m, D_MODEL), F32), jax.ShapeDtypeStruct((m, D_MODEL), BF16), *cast_shapes],
        compiler_params=_cparams(("arbitrary",), windows, 3 * tile),
        name=f"outproj_{m}",
    )(lru, attn, *([proj] * (2 * N_GATE_CHUNKS)), x, g2, wl, wa, wo, *cast_weights)


def _mlp_kernel(h_ref, hn_ref, gf_ref, wu_ref, wd_ref, o_ref):
    f = pl.program_id(1)

    @pl.when(f == 0)
    def _():
        o_ref[...] = jnp.zeros(o_ref.shape, F32)

    u = jnp.dot(hn_ref[...], wu_ref[...], preferred_element_type=F32)
    u = jnp.square(jnp.maximum(u, 0.0))
    o_ref[...] += jnp.dot(u.astype(BF16), wd_ref[...], preferred_element_type=F32)

    @pl.when(f == pl.num_programs(1) - 1)
    def _():
        out = h_ref[...] + o_ref[...]
        ms = jnp.mean(out * out, axis=-1, keepdims=True)
        o_ref[...] = out * lax.rsqrt(ms + EPS) * gf_ref[...]


def _mlp(h, hn, gf, wu, wd):
    m = h.shape[0]
    tm, tf = _tile_plan(m)["mlp"]
    windows = (2 * _nbytes((tm, D_MODEL), F32) + _nbytes((tm, D_MODEL), BF16) + 2 * _nbytes((D_MODEL, tf), BF16))
    return pl.pallas_call(
        _mlp_kernel,
        grid=(m // tm, D_FF // tf),
        in_specs=[
            pl.BlockSpec((tm, D_MODEL), lambda i, f: (i, 0)),
            pl.BlockSpec((tm, D_MODEL), lambda i, f: (i, 0)),
            pl.BlockSpec((1, D_MODEL), lambda i, f: (0, 0)),
            pl.BlockSpec((D_MODEL, tf), lambda i, f: (0, f)),
            pl.BlockSpec((tf, D_MODEL), lambda i, f: (f, 0)),
        ],
        out_specs=pl.BlockSpec((tm, D_MODEL), lambda i, f: (i, 0)),
        out_shape=jax.ShapeDtypeStruct((m, D_MODEL), F32),
        compiler_params=_cparams(("arbitrary", "arbitrary"), windows, 2 * _nbytes((tm, tf), F32)),
        name=f"mlp_{m}",
    )(h, hn, gf, wu, wd)


def _sample_bias():
    dist = (WINDOW - np.arange(WINDOW)).astype(np.float32)
    slopes = np.asarray(_slopes(), np.float32).reshape(N_KV_HEADS, GROUP, 1)
    return jnp.asarray(-(slopes * dist[None, None, :]))


def kernel(x_prompt, x_sample, cache_meta_k, cache_meta_v, cache_win_k, cache_win_v, state_conv, state_h,
           meta_tokens, norm1_g, w_in, conv_w, conv_b, w_gate_x, b_gate_x, w_gate_a, b_gate_a, lru_a_param,
           attn_sinks, w_lru_out, w_attn_out, w_o, norm2_g, w_mlp_up, w_mlp_down, final_norm_g):
    row = lambda v: v.reshape(1, -1)
    g1, g2, gf = row(norm1_g[0]), row(norm2_g[0]), row(final_norm_g)
    w_in_b = w_in[0].astype(BF16)
    wgx, wga = w_gate_x[0].astype(BF16), w_gate_a[0].astype(BF16)
    cw, cb = conv_w[0], row(conv_b[0])
    bgx, bga, ap = row(b_gate_x[0]), row(b_gate_a[0]), row(lru_a_param[0])
    sinks = attn_sinks[0]

    x_main = x_prompt.reshape(SEQ, D_MODEL)
    x_extra = jnp.concatenate([
        x_sample.reshape(DEC_BATCH, D_MODEL),
        jnp.zeros((EXTRA_ROWS - DEC_BATCH - N_META, D_MODEL), F32),
        meta_tokens.astype(F32)], axis=0)

    proj_m, wl, wa, wo = _inproj(x_main, g1, w_in_b, (w_lru_out[0], w_attn_out[0], w_o[0]))
    proj_e, = _inproj(x_extra, g1, w_in_b)

    lru_e, h_sample, h_meta = _lru_extra(proj_e, state_conv[0].reshape(DEC_BATCH, (CONV_W - 1) * LRU_WIDTH),
                                         state_h[0], cw, cb, wgx, bgx, wga, bga, ap)
    tail = proj_e[EXTRA_ROWS - SUBLANES:, COL_XB:COL_XB + LRU_WIDTH]
    lru_m, h_last = _lru_main(proj_m, tail, h_meta, cw, cb, wgx, bgx, wga, bga, ap)

    k_meta = proj_e[META_ROW0:, COL_K:COL_K + KV_WIDTH]
    v_meta = proj_e[META_ROW0:, COL_V:COL_V + KV_WIDTH]
    attn_meta = _attn_meta(sinks, proj_e[META_ROW0:, COL_Q:COL_Q + ATTN_WIDTH], k_meta, v_meta)
    k_new = proj_e[:DEC_BATCH, None, COL_K:COL_K + KV_WIDTH]
    v_new = proj_e[:DEC_BATCH, None, COL_V:COL_V + KV_WIDTH]
    q4 = proj_e[:DEC_BATCH, COL_Q:COL_Q + ATTN_WIDTH].reshape(DEC_BATCH, N_KV_HEADS, GROUP, HEAD_DIM)
    flat = lambda c, n: c[0].reshape(DEC_BATCH, n, KV_WIDTH)
    attn_s, kw_out, vw_out = _attn_sample(
        q4, flat(cache_meta_k, N_META), flat(cache_win_k, WINDOW), k_new,
        flat(cache_meta_v, N_META), flat(cache_win_v, WINDOW), v_new,
        _sample_bias(), sinks.reshape(N_KV_HEADS, GROUP, 1), bb=8)
    attn_e = jnp.concatenate([
        attn_s.reshape(DEC_BATCH, ATTN_WIDTH),
        jnp.zeros((EXTRA_ROWS - DEC_BATCH - N_META, ATTN_WIDTH), F32),
        attn_meta], axis=0).astype(BF16)

    res_m, resn_m, wu, wd = _attn_outproj(sinks, proj_m, k_meta, v_meta, lru_m, x_main, g2, wl, wa, wo,
                                          (w_mlp_up[0], w_mlp_down[0]))
    res_e, resn_e = _outproj(lru_e, attn_e, proj_e, x_extra, g2, wl, wa, wo)
    y_m = _mlp(res_m, resn_m, gf, wu, wd)
    y_e = _mlp(res_e, resn_e, gf, wu, wd)

    kv5 = lambda a, n: a.reshape(1, -1, n, N_KV_HEADS, HEAD_DIM)
    return (
        y_m.reshape(1, SEQ, D_MODEL),
        y_e[:DEC_BATCH].reshape(DEC_BATCH, 1, D_MODEL),
        kv5(k_meta, N_META), kv5(v_meta, N_META),
        kv5(proj_m[SEQ - WINDOW:, COL_K:COL_K + KV_WIDTH], WINDOW),
        kv5(proj_m[SEQ - WINDOW:, COL_V:COL_V + KV_WIDTH], WINDOW),
        proj_m[SEQ - (CONV_W - 1):, COL_XB:COL_XB + LRU_WIDTH].reshape(1, 1, CONV_W - 1, LRU_WIDTH),
        h_last.reshape(1, 1, LRU_WIDTH),
        kv5(kw_out, WINDOW), kv5(vw_out, WINDOW),
        jnp.concatenate([state_conv[0][:, 1:], proj_e[:DEC_BATCH, None, COL_XB:COL_XB + LRU_WIDTH]], axis=1)[None],
        h_sample[None],
    )
```

```python
import functools

import numpy as np
import jax
import jax.numpy as jnp
from jax import lax
from jax.experimental import pallas as pl
from jax.experimental.pallas import tpu as pltpu

D_MODEL = 2048
SEQ = 16384
DEC_BATCH = 128
N_META = 16
LRU_WIDTH = 1024
N_LRU_BLOCKS = 8
LRU_BLOCK = LRU_WIDTH // N_LRU_BLOCKS
CONV_W = 4
LRU_C = 8.0
N_HEADS = 16
N_KV_HEADS = 4
HEAD_DIM = 64
GROUP = N_HEADS // N_KV_HEADS
ATTN_WIDTH = N_HEADS * HEAD_DIM
KV_WIDTH = N_KV_HEADS * HEAD_DIM
WINDOW = 128
BLOCK = 128
D_FF = 4 * D_MODEL
EPS = 1e-6
NEG_INF = -1e30
IN_WIDTH = 2 * LRU_WIDTH + ATTN_WIDTH + 2 * KV_WIDTH + 2 * D_MODEL
COL_XB, COL_YB, COL_Q = 0, LRU_WIDTH, 2 * LRU_WIDTH
COL_K = COL_Q + ATTN_WIDTH
COL_V = COL_K + KV_WIDTH
COL_GL = COL_V + KV_WIDTH
COL_GA = COL_GL + D_MODEL
GATE_CHUNK = 512

EXTRA_ROWS = 256
META_ROW0 = EXTRA_ROWS - N_META
SUBLANES = 8
LANES = 128
MIB = 1024 * 1024

F32 = jnp.float32
BF16 = jnp.bfloat16


def _slopes():
    return [2.0 ** (-8.0 * (h + 1) / N_HEADS) for h in range(N_HEADS)]


V7X_VMEM_REQUEST_CAP = 60 * MIB


def _nbytes(shape, dtype):
    return int(np.prod(shape)) * jnp.dtype(dtype).itemsize


def _cparams(sem, pipelined, resident=0):
    estimate = 2 * pipelined + resident
    limit = min(V7X_VMEM_REQUEST_CAP, estimate + estimate // 4 + 2 * MIB)
    return pltpu.CompilerParams(dimension_semantics=sem, vmem_limit_bytes=limit)


def _tile_plan(rows):
    if rows == EXTRA_ROWS:
        return dict(inproj=(EXTRA_ROWS, 1536), outproj=EXTRA_ROWS, mlp=(EXTRA_ROWS, 512))
    return dict(inproj=(1024, 1536), lru=512, outproj=256, mlp=(1024, 512))


CAST_STEPS = 64


def _cast_specs(weights, step_of):
    specs, shapes, nbytes = [], [], 0
    for w in weights:
        block = (w.shape[0] // CAST_STEPS, w.shape[1])
        specs.append(pl.BlockSpec(block, lambda *idx: (jnp.minimum(step_of(*idx), CAST_STEPS - 1), 0)))
        shapes.append(jax.ShapeDtypeStruct(w.shape, BF16))
        nbytes += _nbytes(block, F32) + _nbytes(block, BF16)
    return specs, shapes, nbytes


def _cast_chunks(src_refs, dst_refs):
    for src_ref, dst_ref in zip(src_refs, dst_refs):
        dst_ref[...] = src_ref[...].astype(BF16)


def _inproj_kernel(x_ref, g_ref, w_ref, *refs, n_cast):
    cast_src, o_ref, cast_dst, xs_ref = refs[:n_cast], refs[n_cast], refs[n_cast + 1:-1], refs[-1]

    @pl.when(pl.program_id(1) == 0)
    def _():
        x = x_ref[...]
        ms = jnp.mean(x * x, axis=-1, keepdims=True)
        xs_ref[...] = (x * lax.rsqrt(ms + EPS) * g_ref[...]).astype(BF16)

    o_ref[...] = jnp.dot(xs_ref[...], w_ref[...], preferred_element_type=F32)
    _cast_chunks(cast_src, cast_dst)


def _inproj(x, g, w, cast_weights=()):
    m = x.shape[0]
    tm, tn = _tile_plan(m)["inproj"]
    n_col = IN_WIDTH // tn
    cast_specs, cast_shapes, cast_bytes = _cast_specs(cast_weights, lambda i, j: i * n_col + j)
    assert not cast_weights or (m // tm) * n_col >= CAST_STEPS
    windows = _nbytes((tm, D_MODEL), F32) + _nbytes((D_MODEL, tn), BF16) + _nbytes((tm, tn), F32) + cast_bytes
    return pl.pallas_call(
        functools.partial(_inproj_kernel, n_cast=len(cast_weights)),
        grid=(m // tm, n_col),
        in_specs=[
            pl.BlockSpec((tm, D_MODEL), lambda i, j: (i, 0)),
            pl.BlockSpec((1, D_MODEL), lambda i, j: (0, 0)),
            pl.BlockSpec((D_MODEL, tn), lambda i, j: (0, j)),
            *cast_specs,
        ],
        out_specs=[pl.BlockSpec((tm, tn), lambda i, j: (i, j)), *cast_specs],
        out_shape=[jax.ShapeDtypeStruct((m, IN_WIDTH), F32), *cast_shapes],
        scratch_shapes=[pltpu.VMEM((tm, D_MODEL), BF16)],
        compiler_params=_cparams(("arbitrary", "arbitrary"), windows, _nbytes((tm, D_MODEL), BF16)),
        name=f"inproj_{m}",
    )(x, g, w, *cast_weights)


def _gate_ab(xc_n, n, wgx_ref, bgx_ref, wga_ref, bga_ref, ap_ref):
    cols = slice(n * LRU_BLOCK, (n + 1) * LRU_BLOCK)
    xcb = xc_n.astype(BF16)
    gx = jnp.dot(xcb, wgx_ref[n], preferred_element_type=F32) + bgx_ref[:, cols]
    ga = jnp.dot(xcb, wga_ref[n], preferred_element_type=F32) + bga_ref[:, cols]
    gate_x = jax.nn.sigmoid(gx)
    gate_a = jax.nn.sigmoid(ga)
    log_a = -LRU_C * gate_a * jax.nn.softplus(-ap_ref[:, cols])
    a = jnp.exp(log_a)
    z = -jnp.tanh(log_a) * (a * a + 1.0)
    root = jnp.where(z > 0.0, z * lax.rsqrt(z), 0.0)
    b = root * gate_x * xc_n
    return a, b


def _lru_coeffs(rows, ext_ref, a3_ref, b3_ref, xb, cw_ref, cb_ref, wgx_ref, bgx_ref, wga_ref, bga_ref, ap_ref):
    for n in range(N_LRU_BLOCKS):
        _lru_coeffs_block(n, rows, ext_ref, a3_ref, b3_ref, xb, cw_ref, cb_ref, wgx_ref, bgx_ref, wga_ref, bga_ref,
                          ap_ref)


def _lru_coeffs_block(n, rows, ext_ref, a3_ref, b3_ref, xb, cw_ref, cb_ref, wgx_ref, bgx_ref, wga_ref, bga_ref,
                      ap_ref):
    groups = rows // SUBLANES
    cols = slice(n * LRU_BLOCK, (n + 1) * LRU_BLOCK)
    ext_ref[SUBLANES:SUBLANES + rows, cols] = xb[:, cols]
    x_ext = ext_ref[:, cols]
    xc = x_ext * cw_ref[0:1, cols]
    for t in range(1, CONV_W):
        xc = pltpu.roll(xc, 1, axis=0) + x_ext * cw_ref[t:t + 1, cols]
    xc = xc[SUBLANES:, :] + cb_ref[:, cols]
    a, b = _gate_ab(xc, n, wgx_ref, bgx_ref, wga_ref, bga_ref, ap_ref)
    a3_ref[:, n * SUBLANES:(n + 1) * SUBLANES, :] = a.reshape(groups, SUBLANES, LANES)
    b3_ref[:, n * SUBLANES:(n + 1) * SUBLANES, :] = b.reshape(groups, SUBLANES, LANES)


def _lru_scan(a3_ref, b3_ref, h3_ref, h_init, g_lo, g_hi):
    def group_step(i, h):
        for r in range(SUBLANES):
            step = pl.ds(r, N_LRU_BLOCKS, stride=SUBLANES)
            h = a3_ref[i, step, :] * h + b3_ref[i, step, :]
            h3_ref[i, step, :] = h
        return h

    return lax.fori_loop(g_lo, g_hi, group_step, h_init)


def _lru_gate_out(rows, h3_ref, yb_ref, lru_ref, blocks=range(N_LRU_BLOCKS)):
    for n in blocks:
        cols = slice(n * LRU_BLOCK, (n + 1) * LRU_BLOCK)
        hs = h3_ref[:, n * SUBLANES:(n + 1) * SUBLANES, :].reshape(rows, LANES)
        lru_ref[:, cols] = (jax.nn.gelu(yb_ref[:, cols]) * hs).astype(BF16)


def _lru_rows(rows, ext_ref, a3_ref, b3_ref, h3_ref, xb, yb_ref, cw_ref, cb_ref,
              wgx_ref, bgx_ref, wga_ref, bga_ref, ap_ref, lru_ref, h_init, g_lo, g_hi):
    _lru_coeffs(rows, ext_ref, a3_ref, b3_ref, xb, cw_ref, cb_ref, wgx_ref, bgx_ref, wga_ref, bga_ref, ap_ref)
    h = _lru_scan(a3_ref, b3_ref, h3_ref, h_init, g_lo, g_hi)
    _lru_gate_out(rows, h3_ref, yb_ref, lru_ref)
    return h


def _lru_extra_kernel(xb_ref, yb_ref, sconv_ref, sh_ref, cw_ref, cb_ref, wgx_ref, bgx_ref, wga_ref, bga_ref,
                      ap_ref, lru_ref, hs_ref, hmeta_ref, ext_ref, a3_ref, b3_ref, h3_ref):
    nb = DEC_BATCH
    xb = xb_ref[0:nb, :]
    xc = sconv_ref[:, 0:LRU_WIDTH] * cw_ref[0:1, :]
    xc = xc + sconv_ref[:, LRU_WIDTH:2 * LRU_WIDTH] * cw_ref[1:2, :]
    xc = xc + sconv_ref[:, 2 * LRU_WIDTH:3 * LRU_WIDTH] * cw_ref[2:3, :]
    xc = xc + xb * cw_ref[3:4, :]
    xc = xc + cb_ref[...]
    for n in range(N_LRU_BLOCKS):
        cols = slice(n * LRU_BLOCK, (n + 1) * LRU_BLOCK)
        a, b = _gate_ab(xc[:, cols], n, wgx_ref, bgx_ref, wga_ref, bga_ref, ap_ref)
        h = a * sh_ref[:, cols] + b
        hs_ref[:, cols] = h
        lru_ref[0:nb, cols] = (jax.nn.gelu(yb_ref[0:nb, cols]) * h).astype(BF16)
    rows = EXTRA_ROWS - nb
    ext_ref[0:SUBLANES, :] = jnp.zeros((SUBLANES, LRU_WIDTH), F32)
    h3_ref[...] = jnp.zeros(h3_ref.shape, F32)
    first_group = (rows - N_META) // SUBLANES
    hmeta_ref[...] = _lru_rows(rows, ext_ref, a3_ref, b3_ref, h3_ref, xb_ref[nb:, :], yb_ref.at[nb:, :], cw_ref,
                               cb_ref, wgx_ref, bgx_ref, wga_ref, bga_ref, ap_ref, lru_ref.at[nb:, :],
                               jnp.zeros((SUBLANES, LANES), F32), first_group, rows // SUBLANES)


def _lru_extra(proj_e, sconv, sh, cw, cb, wgx, bgx, wga, bga, ap):
    const = lambda shape: pl.BlockSpec(shape, lambda i: (0,) * len(shape))
    rows = EXTRA_ROWS - DEC_BATCH
    scan_shape = (rows // SUBLANES, N_LRU_BLOCKS * SUBLANES, LANES)
    return pl.pallas_call(
        _lru_extra_kernel,
        grid=(1,),
        in_specs=[
            pl.BlockSpec((EXTRA_ROWS, LRU_WIDTH), lambda i: (0, COL_XB // LRU_WIDTH)),
            pl.BlockSpec((EXTRA_ROWS, LRU_WIDTH), lambda i: (0, COL_YB // LRU_WIDTH)),
            const((DEC_BATCH, (CONV_W - 1) * LRU_WIDTH)),
            const((DEC_BATCH, LRU_WIDTH)),
            const((CONV_W, LRU_WIDTH)),
            const((1, LRU_WIDTH)),
            const((N_LRU_BLOCKS, LRU_BLOCK, LRU_BLOCK)),
            const((1, LRU_WIDTH)),
            const((N_LRU_BLOCKS, LRU_BLOCK, LRU_BLOCK)),
            const((1, LRU_WIDTH)),
            const((1, LRU_WIDTH)),
        ],
        out_specs=[
            const((EXTRA_ROWS, LRU_WIDTH)),
            const((DEC_BATCH, LRU_WIDTH)),
            const((SUBLANES, LANES)),
        ],
        out_shape=[
            jax.ShapeDtypeStruct((EXTRA_ROWS, LRU_WIDTH), BF16),
            jax.ShapeDtypeStruct((DEC_BATCH, LRU_WIDTH), F32),
            jax.ShapeDtypeStruct((SUBLANES, LANES), F32),
        ],
        scratch_shapes=[
            pltpu.VMEM((rows + SUBLANES, LRU_WIDTH), F32),
            pltpu.VMEM(scan_shape, F32),
            pltpu.VMEM(scan_shape, F32),
            pltpu.VMEM(scan_shape, F32),
        ],
        compiler_params=_cparams(("arbitrary",), 8 * _nbytes((DEC_BATCH, LRU_WIDTH), F32),
                                 8 * _nbytes((rows, LRU_WIDTH), F32)),
        name="lru_extra",
    )(proj_e, proj_e, sconv, sh, cw, cb, wgx, bgx, wga, bga, ap)


def _attend_kv_head(q_h, kk, vv, bias, sinks):
    return _attend_values(*_attend_weights(q_h, kk, bias, sinks), vv)


def _attend_weights(q_h, kk, bias, sinks):
    r = q_h.shape[0]
    qs = jnp.concatenate([q_h[:, g * HEAD_DIM:(g + 1) * HEAD_DIM] for g in range(GROUP)], axis=0)
    qs = (qs * (HEAD_DIM ** -0.5)).astype(BF16)
    s = lax.dot_general(kk.astype(BF16), qs, (((1,), (1,)), ((), ())), preferred_element_type=F32) + bias
    sink = jnp.concatenate([jnp.full((1, r), sinks[g], F32) for g in range(GROUP)], axis=1)
    m = jnp.maximum(jnp.max(s, axis=0, keepdims=True), sink)
    e = jnp.exp(s - m)
    denom = jnp.sum(e, axis=0, keepdims=True) + jnp.exp(sink - m)
    return e.astype(BF16), denom


def _attend_values(e, denom, vv):
    r = e.shape[1] // GROUP
    o_t = jnp.dot(vv.T.astype(BF16), e, preferred_element_type=F32) / denom
    pairs = [jnp.concatenate([o_t[:, g * r:(g + 1) * r], o_t[:, (g + 1) * r:(g + 2) * r]], axis=0).T
             for g in range(0, GROUP, 2)]
    return jnp.concatenate(pairs, axis=-1)


ATTN_KEYS = N_META + 2 * BLOCK


ATTN_BLOCKS_PER_STEP = 2


ATTN_BIAS_SHAPE = (2, N_KV_HEADS, ATTN_KEYS, GROUP * BLOCK)


def _build_attn_bias(bias_ref):
    slopes = _slopes()
    key = lax.broadcasted_iota(jnp.int32, (ATTN_KEYS, BLOCK), 0)
    qry = lax.broadcasted_iota(jnp.int32, (ATTN_KEYS, BLOCK), 1)
    d = qry + BLOCK - (key - N_META)
    dist = jnp.where(key < N_META, 0, d).astype(F32)
    for has_prev in range(2):
        band_ok = (d >= 0) & (d <= WINDOW) & ((key >= N_META + BLOCK) | bool(has_prev))
        valid = (key < N_META) | band_ok
        for hd in range(N_HEADS):
            lanes = slice((hd % GROUP) * BLOCK, (hd % GROUP + 1) * BLOCK)
            bias_ref[has_prev, hd // GROUP, :, lanes] = jnp.where(valid, -(slopes[hd] * dist), NEG_INF)


def _attend_stages(tile, sink_ref, q_ref, kc_ref, kp_ref, vc_ref, vp_ref, km_ref, vm_ref, bias_ref, o_ref):
    def pair(sub, h):
        rows = slice(sub * BLOCK, (sub + 1) * BLOCK)
        prev_k, prev_v = (kp_ref, vp_ref) if sub == 0 else (kc_ref.at[(sub - 1) * BLOCK:sub * BLOCK],
                                                             vc_ref.at[(sub - 1) * BLOCK:sub * BLOCK])
        table = jnp.where(tile == 0, 0, 1) if sub == 0 else 1
        hc = slice(h * HEAD_DIM, (h + 1) * HEAD_DIM)
        qc = slice(h * GROUP * HEAD_DIM, (h + 1) * GROUP * HEAD_DIM)
        v = {}

        def weights():
            kk = jnp.concatenate([km_ref[:, hc], prev_k[:, hc], kc_ref[rows, hc]], axis=0)
            v["w"] = _attend_weights(q_ref[rows, qc], kk, bias_ref[table, h],
                                     [sink_ref[h * GROUP + g] for g in range(GROUP)])

        def values():
            vv = jnp.concatenate([vm_ref[:, hc], prev_v[:, hc], vc_ref[rows, hc]], axis=0)
            o_ref[rows, qc] = _attend_values(*v["w"], vv).astype(BF16)

        return weights, values

    return [pair(sub, h) for sub in range(ATTN_BLOCKS_PER_STEP) for h in range(N_KV_HEADS)]


def _attn_meta_kernel(sink_ref, q_ref, km_ref, vm_ref, o_ref):
    key = lax.broadcasted_iota(jnp.int32, (N_META, GROUP * N_META), 0)
    qry = lax.broadcasted_iota(jnp.int32, (N_META, GROUP * N_META), 1) % N_META
    bias = jnp.where(qry >= key, 0.0, NEG_INF)
    for h in range(N_KV_HEADS):
        hc = slice(h * HEAD_DIM, (h + 1) * HEAD_DIM)
        q_h = q_ref[:, h * GROUP * HEAD_DIM:(h + 1) * GROUP * HEAD_DIM]
        o = _attend_kv_head(q_h, km_ref[:, hc], vm_ref[:, hc], bias,
                            [sink_ref[h * GROUP + g] for g in range(GROUP)])
        o_ref[:, h * GROUP * HEAD_DIM:(h + 1) * GROUP * HEAD_DIM] = o


def _attn_meta(sinks, q_meta, k_meta, v_meta):
    const = lambda shape: pl.BlockSpec(shape, lambda i: (0,) * len(shape))
    return pl.pallas_call(
        _attn_meta_kernel,
        grid=(1,),
        in_specs=[pl.BlockSpec(memory_space=pltpu.SMEM), const((N_META, ATTN_WIDTH)),
                  const((N_META, KV_WIDTH)), const((N_META, KV_WIDTH))],
        out_specs=const((N_META, ATTN_WIDTH)),
        out_shape=jax.ShapeDtypeStruct((N_META, ATTN_WIDTH), F32),
        compiler_params=_cparams(("arbitrary",), 4 * _nbytes((N_META, ATTN_WIDTH), F32)),
        name="attn_meta",
    )(sinks, q_meta, k_meta, v_meta)


def _attn_sample_kernel(q_ref, km_ref, kw_ref, kn_ref, vm_ref, vw_ref, vn_ref, bias_ref, sink_ref,
                        o_ref, kwo_ref, vwo_ref):
    for src_ref, new_ref, dst_ref in ((kw_ref, kn_ref, kwo_ref), (vw_ref, vn_ref, vwo_ref)):
        dst_ref[:, 0:WINDOW - 1, :] = src_ref[:, 1:WINDOW, :]
        dst_ref[:, WINDOW - 1:WINDOW, :] = new_ref[...]
    dot_qk = functools.partial(jnp.einsum, "bgd,bjd->bgj", preferred_element_type=F32)
    dot_pv = functools.partial(jnp.einsum, "bgj,bjd->bgd", preferred_element_type=F32)
    for h in range(N_KV_HEADS):
        hc = slice(h * HEAD_DIM, (h + 1) * HEAD_DIM)
        q = (q_ref[:, h] * (HEAD_DIM ** -0.5)).astype(BF16)
        k_new = kn_ref[:, :, hc].astype(BF16).astype(F32)
        v_new = vn_ref[:, :, hc].astype(BF16).astype(F32)
        s_m = dot_qk(q, km_ref[:, :, hc].astype(BF16))
        s_w = dot_qk(q, kw_ref[:, :, hc].astype(BF16)) + bias_ref[h]
        s_n = jnp.sum(q.astype(F32) * k_new, axis=-1, keepdims=True)
        sink = sink_ref[h]
        m = jnp.maximum(jnp.maximum(jnp.max(s_m, axis=-1, keepdims=True), jnp.max(s_w, axis=-1, keepdims=True)),
                        jnp.maximum(s_n, sink))
        e_m, e_w, e_n = jnp.exp(s_m - m), jnp.exp(s_w - m), jnp.exp(s_n - m)
        denom = (jnp.sum(e_m, axis=-1, keepdims=True) + jnp.sum(e_w, axis=-1, keepdims=True) + e_n
                 + jnp.exp(sink - m))
        o = (dot_pv(e_m.astype(BF16), vm_ref[:, :, hc].astype(BF16))
             + dot_pv(e_w.astype(BF16), vw_ref[:, :, hc].astype(BF16))
             + e_n.astype(BF16).astype(F32) * v_new)
        o_ref[:, h] = o / denom


def _attn_sample(q4, k_meta, k_win, k_new, v_meta, v_win, v_new, bias, sinks4, bb):
    b3 = lambda rows: pl.BlockSpec((bb, rows, KV_WIDTH), lambda i: (i, 0, 0))
    q_spec = pl.BlockSpec((bb, N_KV_HEADS, GROUP, HEAD_DIM), lambda i: (i, 0, 0, 0))
    win_shape = jax.ShapeDtypeStruct((DEC_BATCH, WINDOW, KV_WIDTH), F32)
    return pl.pallas_call(
        _attn_sample_kernel,
        grid=(DEC_BATCH // bb,),
        in_specs=[
            q_spec,
            b3(N_META), b3(WINDOW), b3(1),
            b3(N_META), b3(WINDOW), b3(1),
            pl.BlockSpec((N_KV_HEADS, GROUP, WINDOW), lambda i: (0, 0, 0)),
            pl.BlockSpec((N_KV_HEADS, GROUP, 1), lambda i: (0, 0, 0)),
        ],
        out_specs=[q_spec, b3(WINDOW), b3(WINDOW)],
        out_shape=[jax.ShapeDtypeStruct((DEC_BATCH, N_KV_HEADS, GROUP, HEAD_DIM), F32), win_shape, win_shape],
        compiler_params=_cparams(("arbitrary",), 5 * _nbytes((bb, WINDOW, KV_WIDTH), F32),
                                 4 * _nbytes((bb, WINDOW, KV_WIDTH), F32)),
        name="attn_sample",
    )(q4, k_meta, k_win, k_new, v_meta, v_win, v_new, bias, sinks4)


N_GATE_CHUNKS = D_MODEL // GATE_CHUNK


def _outproj_stages(lru_ref, attn_ref, gl_refs, ga_refs, x_ref, g2_ref, wl_ref, wa_ref, wo_ref, h_ref, hn_ref):
    v = {"a": [], "l": [], "ss": 0.0}
    half = D_MODEL // 2

    def branch_proj(key, src_ref, w_ref, part):
        if part == 0:
            v[key + "_in"] = src_ref[...]
        v[key].append(jnp.dot(v[key + "_in"], w_ref[:, part * half:(part + 1) * half], preferred_element_type=F32))

    def merge():
        a, l = (jnp.concatenate(v[key], axis=1) for key in ("a", "l"))
        v["m"] = jnp.concatenate(
            [jax.nn.sigmoid(gl_refs[c][...]) * l[:, c * GATE_CHUNK:(c + 1) * GATE_CHUNK]
             + jax.nn.sigmoid(ga_refs[c][...]) * a[:, c * GATE_CHUNK:(c + 1) * GATE_CHUNK]
             for c in range(N_GATE_CHUNKS)], axis=1).astype(BF16)

    def out_chunk(c):
        cols = slice(c * GATE_CHUNK, (c + 1) * GATE_CHUNK)
        h = x_ref[:, cols] + jnp.dot(v["m"], wo_ref[:, cols], preferred_element_type=F32)
        h_ref[:, cols] = h
        v["ss"] = v["ss"] + jnp.sum(h * h, axis=-1, keepdims=True)

    def norm():
        scale = lax.rsqrt(v["ss"] * (1.0 / D_MODEL) + EPS)
        hn_ref[...] = (h_ref[...] * scale * g2_ref[...]).astype(BF16)

    def seq(*stages):
        return lambda: [stage() for stage in stages]

    return [functools.partial(branch_proj, "a", attn_ref, wa_ref, 0),
            functools.partial(branch_proj, "a", attn_ref, wa_ref, 1),
            functools.partial(branch_proj, "l", lru_ref, wl_ref, 0),
            functools.partial(branch_proj, "l", lru_ref, wl_ref, 1),
            seq(merge, functools.partial(out_chunk, 0)),
            functools.partial(out_chunk, 1),
            functools.partial(out_chunk, 2),
            seq(functools.partial(out_chunk, 3), norm)]


def _run_interleaved(project, attend, before=()):
    for i in range(max(len(project), len(attend), len(before))):
        if i < len(attend):
            attend[i][0]()
        if i < len(before):
            before[i]()
        if i < len(project):
            project[i]()
        if i < len(attend):
            attend[i][1]()


def _split_outproj_refs(refs, n_cast):
    n = 2 * N_GATE_CHUNKS
    tile_refs = (refs[:N_GATE_CHUNKS], refs[N_GATE_CHUNKS:n], *refs[n:n + 5])
    rest = refs[n + 5:]
    cast_src, outs, cast_dst = rest[:n_cast], rest[n_cast:n_cast + 2], rest[n_cast + 2:2 * n_cast + 2]
    return tile_refs, cast_src, outs, cast_dst, rest[2 * n_cast + 2:]


def _outproj_kernel(lru_ref, attn_ref, *refs, n_cast):
    tile_refs, cast_src, outs, cast_dst, _ = _split_outproj_refs(refs, n_cast)
    _run_interleaved(_outproj_stages(lru_ref, attn_ref, *tile_refs, *outs), [])
    _cast_chunks(cast_src, cast_dst)


MIXER_TILE = ATTN_BLOCKS_PER_STEP * BLOCK
N_MIXER_INPUTS = 19


def _mixers_outproj_kernel(*refs, n_cast):
    (sink_ref, q_ref, kc_ref, kp_ref, vc_ref, vp_ref, km_ref, vm_ref,
     xb_ref, yb_ref, tail_ref, h0_ref, *lru_params) = refs[:N_MIXER_INPUTS]
    n = 2 * N_GATE_CHUNKS
    rest = refs[N_MIXER_INPUTS:]
    tile_refs = (rest[:N_GATE_CHUNKS], rest[N_GATE_CHUNKS:n], *rest[n:n + 5])
    rest = rest[n + 5:]
    cast_src, (h_ref, hn_ref, hlast_ref), cast_dst = rest[:n_cast], rest[n_cast:n_cast + 3], rest[n_cast + 3:2 * n_cast + 3]
    bias_ref, attn_ref, lru_ref, ext_ref, a3_ref, b3_ref, h3_ref, hstate_ref = rest[2 * n_cast + 3:]
    rows = MIXER_TILE
    t = pl.program_id(0)
    last = pl.num_programs(0) - 1

    attend = lambda: _attend_stages(t, sink_ref, q_ref, kc_ref, kp_ref, vc_ref, vp_ref, km_ref, vm_ref,
                                    bias_ref, attn_ref)
    project = lambda: _outproj_stages(lru_ref, attn_ref, *tile_refs, h_ref, hn_ref)

    def coeffs(n_block):
        def stage():
            _lru_coeffs_block(n_block, rows, ext_ref, a3_ref, b3_ref, xb_ref, *lru_params)
            cols = slice(n_block * LRU_BLOCK, (n_block + 1) * LRU_BLOCK)
            ext_ref[0:SUBLANES, cols] = ext_ref[rows:rows + SUBLANES, cols]
        return stage

    def gate_out(blocks):
        return lambda: _lru_gate_out(rows, h3_ref, yb_ref, lru_ref, blocks)

    def seq(*stages):
        return lambda: [stage() for stage in stages]

    def scan():
        h = _lru_scan(a3_ref, b3_ref, h3_ref, hstate_ref[...], 0, rows // SUBLANES)
        hstate_ref[...] = h
        hlast_ref[...] = h

    half = N_LRU_BLOCKS // 2
    all_coeffs = [coeffs(n_block) for n_block in range(N_LRU_BLOCKS)]

    @pl.when(t == 0)
    def _():
        _build_attn_bias(bias_ref)
        ext_ref[0:SUBLANES, :] = tail_ref[...]
        hstate_ref[...] = h0_ref[...]
        _run_interleaved([], attend(), all_coeffs)
        scan()

    @pl.when((t > 0) & (t < last))
    def _():
        before = [seq(gate_out(range(half)), all_coeffs[0]), seq(gate_out(range(half, N_LRU_BLOCKS)), all_coeffs[1]),
                  *all_coeffs[2:]]
        _run_interleaved(project(), attend(), before)
        _cast_chunks(cast_src, cast_dst)
        scan()

    @pl.when(t == last)
    def _():
        gate_out(range(N_LRU_BLOCKS))()
        _run_interleaved(project(), [])
        _cast_chunks(cast_src, cast_dst)


def _mixers_outproj(sinks, proj, k_meta, v_meta, tail, h0, lru_params, x, g2, wl, wa, wo, cast_weights):
    m = x.shape[0]
    tm = MIXER_TILE
    tiles = m // tm
    assert tiles >= CAST_STEPS
    cur = lambda t: jnp.minimum(t, tiles - 1)
    prv = lambda t: jnp.maximum(t - 1, 0)
    const = lambda shape: pl.BlockSpec(shape, lambda t: (0,) * len(shape))
    resident = lambda shape: pl.BlockSpec(shape, lambda t: (0,) * len(shape), pipeline_mode=pl.Buffered(1))
    kv = lambda col: (pl.BlockSpec((tm, KV_WIDTH), lambda t: (cur(t), col // KV_WIDTH)),
                      pl.BlockSpec((BLOCK, KV_WIDTH),
                                   lambda t: (jnp.maximum(ATTN_BLOCKS_PER_STEP * cur(t) - 1, 0), col // KV_WIDTH)))
    kc, kp = kv(COL_K)
    vc, vp = kv(COL_V)
    prv_spec = lambda width, cb=0: pl.BlockSpec((tm, width), lambda t: (prv(t), cb))
    cast_specs, cast_shapes, cast_bytes = _cast_specs(cast_weights, prv)
    tile = _nbytes((tm, D_MODEL), F32)
    scan_shape = (tm // SUBLANES, N_LRU_BLOCKS * SUBLANES, LANES)
    windows = 9 * tile + cast_bytes
    weights = _nbytes(wl.shape, BF16) + _nbytes(wa.shape, BF16) + _nbytes(wo.shape, BF16)
    scratch = weights + 6 * tile + 2 * _nbytes(ATTN_BIAS_SHAPE, F32)
    return pl.pallas_call(
        functools.partial(_mixers_outproj_kernel, n_cast=len(cast_weights)),
        grid=(tiles + 1,),
        in_specs=[
            pl.BlockSpec(memory_space=pltpu.SMEM),
            pl.BlockSpec((tm, ATTN_WIDTH), lambda t: (cur(t), COL_Q // ATTN_WIDTH)),
            kc, kp, vc, vp,
            const((N_META, KV_WIDTH)),
            const((N_META, KV_WIDTH)),
            pl.BlockSpec((tm, LRU_WIDTH), lambda t: (cur(t), COL_XB // LRU_WIDTH)),
            prv_spec(LRU_WIDTH, COL_YB // LRU_WIDTH),
            const((SUBLANES, LRU_WIDTH)),
            const((SUBLANES, LANES)),
            *[const(p.shape) for p in lru_params],
            *[prv_spec(GATE_CHUNK, col // GATE_CHUNK + c) for col in (COL_GL, COL_GA) for c in range(N_GATE_CHUNKS)],
            prv_spec(D_MODEL),
            const((1, D_MODEL)),
            resident((LRU_WIDTH, D_MODEL)),
            resident((ATTN_WIDTH, D_MODEL)),
            resident((D_MODEL, D_MODEL)),
            *cast_specs,
        ],
        out_specs=[prv_spec(D_MODEL), prv_spec(D_MODEL), const((SUBLANES, LANES)), *cast_specs],
        out_shape=[jax.ShapeDtypeStruct((m, D_MODEL), F32), jax.ShapeDtypeStruct((m, D_MODEL), BF16),
                   jax.ShapeDtypeStruct((SUBLANES, LANES), F32), *cast_shapes],
        scratch_shapes=[
            pltpu.VMEM(ATTN_BIAS_SHAPE, F32),
            pltpu.VMEM((tm, ATTN_WIDTH), BF16),
            pltpu.VMEM((tm, LRU_WIDTH), BF16),
            pltpu.VMEM((tm + SUBLANES, LRU_WIDTH), F32),
            pltpu.VMEM(scan_shape, F32),
            pltpu.VMEM(scan_shape, F32),
            pltpu.VMEM(scan_shape, F32),
            pltpu.VMEM((SUBLANES, LANES), F32),
        ],
        compiler_params=_cparams(("arbitrary",), windows, scratch),
        name="mixers_outproj",
    )(sinks, proj, proj, proj, proj, proj, k_meta, v_meta, proj, proj, tail, h0, *lru_params,
      *([proj] * (2 * N_GATE_CHUNKS)), x, g2, wl, wa, wo, *cast_weights)


def _outproj(lru, attn, proj, x, g2, wl, wa, wo, cast_weights=()):
    m = x.shape[0]
    tm = _tile_plan(m)["outproj"]
    const = lambda shape: pl.BlockSpec(shape, lambda i: (0,) * len(shape))
    cast_specs, cast_shapes, cast_bytes = _cast_specs(cast_weights, lambda i: i)
    assert not cast_weights or m // tm >= CAST_STEPS
    tile = _nbytes((tm, D_MODEL), F32)
    windows = 6 * tile + _nbytes(wl.shape, BF16) + _nbytes(wa.shape, BF16) + _nbytes(wo.shape, BF16) + cast_bytes
    row_spec = lambda width: pl.BlockSpec((tm, width), lambda i: (i, 0))
    return pl.pallas_call(
        functools.partial(_outproj_kernel, n_cast=len(cast_weights)),
        grid=(m // tm,),
        in_specs=[
            row_spec(LRU_WIDTH),
            row_spec(ATTN_WIDTH),
            *[pl.BlockSpec((tm, GATE_CHUNK), functools.partial(lambda i, cb: (i, cb), cb=col // GATE_CHUNK + c))
              for col in (COL_GL, COL_GA) for c in range(N_GATE_CHUNKS)],
            row_spec(D_MODEL),
            const((1, D_MODEL)),
            const((LRU_WIDTH, D_MODEL)),
            const((ATTN_WIDTH, D_MODEL)),
            const((D_MODEL, D_MODEL)),
            *cast_specs,
        ],
        out_specs=[row_spec(D_MODEL), row_spec(D_MODEL), *cast_specs],
        out_shape=[jax.ShapeDtypeStruct((m, D_MODEL), F32), jax.ShapeDtypeStruct((m, D_MODEL), BF16), *cast_shapes],
        compiler_params=_cparams(("arbitrary",), windows, 3 * tile),
        name=f"outproj_{m}",
    )(lru, attn, *([proj] * (2 * N_GATE_CHUNKS)), x, g2, wl, wa, wo, *cast_weights)


def _mlp_kernel(h_ref, hn_ref, gf_ref, wu_ref, wd_ref, o_ref):
    f = pl.program_id(1)

    @pl.when(f == 0)
    def _():
        o_ref[...] = jnp.zeros(o_ref.shape, F32)

    u = jnp.dot(hn_ref[...], wu_ref[...], preferred_element_type=F32)
    u = jnp.square(jnp.maximum(u, 0.0))
    o_ref[...] += jnp.dot(u.astype(BF16), wd_ref[...], preferred_element_type=F32)

    @pl.when(f == pl.num_programs(1) - 1)
    def _():
        out = h_ref[...] + o_ref[...]
        ms = jnp.mean(out * out, axis=-1, keepdims=True)
        o_ref[...] = out * lax.rsqrt(ms + EPS) * gf_ref[...]


def _mlp(h, hn, gf, wu, wd):
    m = h.shape[0]
    tm, tf = _tile_plan(m)["mlp"]
    windows = (2 * _nbytes((tm, D_MODEL), F32) + _nbytes((tm, D_MODEL), BF16) + 2 * _nbytes((D_MODEL, tf), BF16))
    return pl.pallas_call(
        _mlp_kernel,
        grid=(m // tm, D_FF // tf),
        in_specs=[
            pl.BlockSpec((tm, D_MODEL), lambda i, f: (i, 0)),
            pl.BlockSpec((tm, D_MODEL), lambda i, f: (i, 0)),
            pl.BlockSpec((1, D_MODEL), lambda i, f: (0, 0)),
            pl.BlockSpec((D_MODEL, tf), lambda i, f: (0, f)),
            pl.BlockSpec((tf, D_MODEL), lambda i, f: (f, 0)),
        ],
        out_specs=pl.BlockSpec((tm, D_MODEL), lambda i, f: (i, 0)),
        out_shape=jax.ShapeDtypeStruct((m, D_MODEL), F32),
        compiler_params=_cparams(("arbitrary", "arbitrary"), windows, 2 * _nbytes((tm, tf), F32)),
        name=f"mlp_{m}",
    )(h, hn, gf, wu, wd)


def _sample_bias():
    dist = (WINDOW - np.arange(WINDOW)).astype(np.float32)
    slopes = np.asarray(_slopes(), np.float32).reshape(N_KV_HEADS, GROUP, 1)
    return jnp.asarray(-(slopes * dist[None, None, :]))


def kernel(x_prompt, x_sample, cache_meta_k, cache_meta_v, cache_win_k, cache_win_v, state_conv, state_h,
           meta_tokens, norm1_g, w_in, conv_w, conv_b, w_gate_x, b_gate_x, w_gate_a, b_gate_a, lru_a_param,
           attn_sinks, w_lru_out, w_attn_out, w_o, norm2_g, w_mlp_up, w_mlp_down, final_norm_g):
    row = lambda v: v.reshape(1, -1)
    g1, g2, gf = row(norm1_g[0]), row(norm2_g[0]), row(final_norm_g)
    w_in_b = w_in[0].astype(BF16)
    wgx, wga = w_gate_x[0].astype(BF16), w_gate_a[0].astype(BF16)
    cw, cb = conv_w[0], row(conv_b[0])
    bgx, bga, ap = row(b_gate_x[0]), row(b_gate_a[0]), row(lru_a_param[0])
    sinks = attn_sinks[0]

    x_main = x_prompt.reshape(SEQ, D_MODEL)
    x_extra = jnp.concatenate([
        x_sample.reshape(DEC_BATCH, D_MODEL),
        jnp.zeros((EXTRA_ROWS - DEC_BATCH - N_META, D_MODEL), F32),
        meta_tokens.astype(F32)], axis=0)

    proj_m, wl, wa, wo, wu, wd = _inproj(
        x_main, g1, w_in_b, (w_lru_out[0], w_attn_out[0], w_o[0], w_mlp_up[0], w_mlp_down[0]))
    proj_e, = _inproj(x_extra, g1, w_in_b)

    lru_e, h_sample, h_meta = _lru_extra(proj_e, state_conv[0].reshape(DEC_BATCH, (CONV_W - 1) * LRU_WIDTH),
                                         state_h[0], cw, cb, wgx, bgx, wga, bga, ap)
    tail = proj_e[EXTRA_ROWS - SUBLANES:, COL_XB:COL_XB + LRU_WIDTH]

    k_meta = proj_e[META_ROW0:, COL_K:COL_K + KV_WIDTH]
    v_meta = proj_e[META_ROW0:, COL_V:COL_V + KV_WIDTH]
    attn_meta = _attn_meta(sinks, proj_e[META_ROW0:, COL_Q:COL_Q + ATTN_WIDTH], k_meta, v_meta)
    k_new = proj_e[:DEC_BATCH, None, COL_K:COL_K + KV_WIDTH]
    v_new = proj_e[:DEC_BATCH, None, COL_V:COL_V + KV_WIDTH]
    q4 = proj_e[:DEC_BATCH, COL_Q:COL_Q + ATTN_WIDTH].reshape(DEC_BATCH, N_KV_HEADS, GROUP, HEAD_DIM)
    flat = lambda c, n: c[0].reshape(DEC_BATCH, n, KV_WIDTH)
    attn_s, kw_out, vw_out = _attn_sample(
        q4, flat(cache_meta_k, N_META), flat(cache_win_k, WINDOW), k_new,
        flat(cache_meta_v, N_META), flat(cache_win_v, WINDOW), v_new,
        _sample_bias(), sinks.reshape(N_KV_HEADS, GROUP, 1), bb=8)
    attn_e = jnp.concatenate([
        attn_s.reshape(DEC_BATCH, ATTN_WIDTH),
        jnp.zeros((EXTRA_ROWS - DEC_BATCH - N_META, ATTN_WIDTH), F32),
        attn_meta], axis=0).astype(BF16)

    res_m, resn_m, h_last = _mixers_outproj(
        sinks, proj_m, k_meta, v_meta, tail, h_meta, (cw, cb, wgx, bgx, wga, bga, ap), x_main, g2, wl, wa, wo, ())
    res_e, resn_e = _outproj(lru_e, attn_e, proj_e, x_extra, g2, wl, wa, wo)
    y_m = _mlp(res_m, resn_m, gf, wu, wd)
    y_e = _mlp(res_e, resn_e, gf, wu, wd)

    kv5 = lambda a, n: a.reshape(1, -1, n, N_KV_HEADS, HEAD_DIM)
    return (
        y_m.reshape(1, SEQ, D_MODEL),
        y_e[:DEC_BATCH].reshape(DEC_BATCH, 1, D_MODEL),
        kv5(k_meta, N_META), kv5(v_meta, N_META),
        kv5(proj_m[SEQ - WINDOW:, COL_K:COL_K + KV_WIDTH], WINDOW),
        kv5(proj_m[SEQ - WINDOW:, COL_V:COL_V + KV_WIDTH], WINDOW),
        proj_m[SEQ - (CONV_W - 1):, COL_XB:COL_XB + LRU_WIDTH].reshape(1, 1, CONV_W - 1, LRU_WIDTH),
        h_last.reshape(1, 1, LRU_WIDTH),
        kv5(kw_out, WINDOW), kv5(vw_out, WINDOW),
        jnp.concatenate([state_conv[0][:, 1:], proj_e[:DEC_BATCH, None, COL_XB:COL_XB + LRU_WIDTH]], axis=1)[None],
        h_sample[None],
    )
```

```python
import functools

import numpy as np
import jax
import jax.numpy as jnp
from jax import lax
from jax.experimental import pallas as pl
from jax.experimental.pallas import tpu as pltpu

D_MODEL = 2048
SEQ = 16384
DEC_BATCH = 128
N_META = 16
LRU_WIDTH = 1024
N_LRU_BLOCKS = 8
LRU_BLOCK = LRU_WIDTH // N_LRU_BLOCKS
CONV_W = 4
LRU_C = 8.0
N_HEADS = 16
N_KV_HEADS = 4
HEAD_DIM = 64
GROUP = N_HEADS // N_KV_HEADS
ATTN_WIDTH = N_HEADS * HEAD_DIM
KV_WIDTH = N_KV_HEADS * HEAD_DIM
WINDOW = 128
BLOCK = 128
D_FF = 4 * D_MODEL
EPS = 1e-6
NEG_INF = -1e30
IN_WIDTH = 2 * LRU_WIDTH + ATTN_WIDTH + 2 * KV_WIDTH + 2 * D_MODEL
COL_XB, COL_YB, COL_Q = 0, LRU_WIDTH, 2 * LRU_WIDTH
COL_K = COL_Q + ATTN_WIDTH
COL_V = COL_K + KV_WIDTH
COL_GL = COL_V + KV_WIDTH
COL_GA = COL_GL + D_MODEL
GATE_CHUNK = 512

EXTRA_ROWS = 256
META_ROW0 = EXTRA_ROWS - N_META
SUBLANES = 8
LANES = 128
MIB = 1024 * 1024

F32 = jnp.float32
BF16 = jnp.bfloat16


def _slopes():
    return [2.0 ** (-8.0 * (h + 1) / N_HEADS) for h in range(N_HEADS)]


V7X_VMEM_REQUEST_CAP = 60 * MIB


def _nbytes(shape, dtype):
    return int(np.prod(shape)) * jnp.dtype(dtype).itemsize


def _cparams(sem, pipelined, resident=0):
    estimate = 2 * pipelined + resident
    limit = min(V7X_VMEM_REQUEST_CAP, estimate + estimate // 4 + 2 * MIB)
    return pltpu.CompilerParams(dimension_semantics=sem, vmem_limit_bytes=limit)


def _tile_plan(rows):
    if rows == EXTRA_ROWS:
        return dict(inproj=(EXTRA_ROWS, 1536), outproj=EXTRA_ROWS, mlp=(EXTRA_ROWS, 512))
    return dict(inproj=(1024, 1536), lru=512, outproj=256, mlp=(1024, 512))


CAST_STEPS = 64


def _cast_specs(weights, step_of):
    specs, shapes, nbytes = [], [], 0
    for w in weights:
        block = (w.shape[0] // CAST_STEPS, w.shape[1])
        specs.append(pl.BlockSpec(block, lambda *idx: (jnp.minimum(step_of(*idx), CAST_STEPS - 1), 0)))
        shapes.append(jax.ShapeDtypeStruct(w.shape, BF16))
        nbytes += _nbytes(block, F32) + _nbytes(block, BF16)
    return specs, shapes, nbytes


def _cast_chunks(src_refs, dst_refs):
    for src_ref, dst_ref in zip(src_refs, dst_refs):
        dst_ref[...] = src_ref[...].astype(BF16)


def _inproj_kernel(x_ref, g_ref, w_ref, *refs, n_cast, emit_w):
    cast_src, o_ref, cast_dst, xs_ref = refs[:n_cast], refs[n_cast], refs[n_cast + 1:2 * n_cast + 1], refs[-1]

    @pl.when(pl.program_id(1) == 0)
    def _():
        x = x_ref[...]
        ms = jnp.mean(x * x, axis=-1, keepdims=True)
        xs_ref[...] = (x * lax.rsqrt(ms + EPS) * g_ref[...]).astype(BF16)

    w = w_ref[...]
    if emit_w:
        w = w.astype(BF16)
        refs[-2][...] = w
    o_ref[...] = jnp.dot(xs_ref[...], w, preferred_element_type=F32)
    _cast_chunks(cast_src, cast_dst)


def _inproj(x, g, w, cast_weights=()):
    m = x.shape[0]
    tm, tn = _tile_plan(m)["inproj"]
    n_col = IN_WIDTH // tn
    emit_w = w.dtype == F32
    assert not emit_w or m == tm, "the bf16 copy of w is written once per column block"
    cast_specs, cast_shapes, cast_bytes = _cast_specs(cast_weights, lambda i, j: i * n_col + j)
    assert not cast_weights or (m // tm) * n_col >= CAST_STEPS
    w_spec = pl.BlockSpec((D_MODEL, tn), lambda i, j: (0, j))
    windows = (_nbytes((tm, D_MODEL), F32) + _nbytes((D_MODEL, tn), w.dtype) + _nbytes((tm, tn), F32) + cast_bytes
               + emit_w * _nbytes((D_MODEL, tn), BF16))
    return pl.pallas_call(
        functools.partial(_inproj_kernel, n_cast=len(cast_weights), emit_w=emit_w),
        grid=(m // tm, n_col),
        in_specs=[
            pl.BlockSpec((tm, D_MODEL), lambda i, j: (i, 0)),
            pl.BlockSpec((1, D_MODEL), lambda i, j: (0, 0)),
            w_spec,
            *cast_specs,
        ],
        out_specs=[pl.BlockSpec((tm, tn), lambda i, j: (i, j)), *cast_specs, *([w_spec] if emit_w else [])],
        out_shape=[jax.ShapeDtypeStruct((m, IN_WIDTH), F32), *cast_shapes,
                   *([jax.ShapeDtypeStruct(w.shape, BF16)] if emit_w else [])],
        scratch_shapes=[pltpu.VMEM((tm, D_MODEL), BF16)],
        compiler_params=_cparams(("arbitrary", "arbitrary"), windows, _nbytes((tm, D_MODEL), BF16)),
        name=f"inproj_{m}",
    )(x, g, w, *cast_weights)


def _gate_ab(xc_n, n, wgx_ref, bgx_ref, wga_ref, bga_ref, ap_ref):
    cols = slice(n * LRU_BLOCK, (n + 1) * LRU_BLOCK)
    xcb = xc_n.astype(BF16)
    gx = jnp.dot(xcb, wgx_ref[n], preferred_element_type=F32) + bgx_ref[:, cols]
    ga = jnp.dot(xcb, wga_ref[n], preferred_element_type=F32) + bga_ref[:, cols]
    gate_x = jax.nn.sigmoid(gx)
    gate_a = jax.nn.sigmoid(ga)
    log_a = -LRU_C * gate_a * jax.nn.softplus(-ap_ref[:, cols])
    a = jnp.exp(log_a)
    z = -jnp.tanh(log_a) * (a * a + 1.0)
    root = jnp.where(z > 0.0, z * lax.rsqrt(z), 0.0)
    b = root * gate_x * xc_n
    return a, b


def _lru_coeffs(rows, ext_ref, a3_ref, b3_ref, xb, cw_ref, cb_ref, wgx_ref, bgx_ref, wga_ref, bga_ref, ap_ref):
    for n in range(N_LRU_BLOCKS):
        _lru_coeffs_block(n, rows, ext_ref, a3_ref, b3_ref, xb, cw_ref, cb_ref, wgx_ref, bgx_ref, wga_ref, bga_ref,
                          ap_ref)


def _lru_coeffs_block(n, rows, ext_ref, a3_ref, b3_ref, xb, cw_ref, cb_ref, wgx_ref, bgx_ref, wga_ref, bga_ref,
                      ap_ref):
    groups = rows // SUBLANES
    cols = slice(n * LRU_BLOCK, (n + 1) * LRU_BLOCK)
    ext_ref[SUBLANES:SUBLANES + rows, cols] = xb[:, cols]
    x_ext = ext_ref[:, cols]
    xc = x_ext * cw_ref[0:1, cols]
    for t in range(1, CONV_W):
        xc = pltpu.roll(xc, 1, axis=0) + x_ext * cw_ref[t:t + 1, cols]
    xc = xc[SUBLANES:, :] + cb_ref[:, cols]
    a, b = _gate_ab(xc, n, wgx_ref, bgx_ref, wga_ref, bga_ref, ap_ref)
    a3_ref[:, n * SUBLANES:(n + 1) * SUBLANES, :] = a.reshape(groups, SUBLANES, LANES)
    b3_ref[:, n * SUBLANES:(n + 1) * SUBLANES, :] = b.reshape(groups, SUBLANES, LANES)


def _lru_scan(a3_ref, b3_ref, h3_ref, h_init, g_lo, g_hi):
    def group_step(i, h):
        for r in range(SUBLANES):
            step = pl.ds(r, N_LRU_BLOCKS, stride=SUBLANES)
            h = a3_ref[i, step, :] * h + b3_ref[i, step, :]
            h3_ref[i, step, :] = h
        return h

    return lax.fori_loop(g_lo, g_hi, group_step, h_init)


def _lru_gate_out(rows, h3_ref, yb_ref, lru_ref, blocks=range(N_LRU_BLOCKS)):
    for n in blocks:
        cols = slice(n * LRU_BLOCK, (n + 1) * LRU_BLOCK)
        hs = h3_ref[:, n * SUBLANES:(n + 1) * SUBLANES, :].reshape(rows, LANES)
        lru_ref[:, cols] = (jax.nn.gelu(yb_ref[:, cols]) * hs).astype(BF16)


def _lru_rows(rows, ext_ref, a3_ref, b3_ref, h3_ref, xb, yb_ref, cw_ref, cb_ref,
              wgx_ref, bgx_ref, wga_ref, bga_ref, ap_ref, lru_ref, h_init, g_lo, g_hi):
    _lru_coeffs(rows, ext_ref, a3_ref, b3_ref, xb, cw_ref, cb_ref, wgx_ref, bgx_ref, wga_ref, bga_ref, ap_ref)
    h = _lru_scan(a3_ref, b3_ref, h3_ref, h_init, g_lo, g_hi)
    _lru_gate_out(rows, h3_ref, yb_ref, lru_ref)
    return h


def _lru_extra_kernel(xb_ref, yb_ref, sconv_ref, sh_ref, cw_ref, cb_ref, wgx_ref, bgx_ref, wga_ref, bga_ref,
                      ap_ref, lru_ref, hs_ref, hmeta_ref, ext_ref, a3_ref, b3_ref, h3_ref):
    nb = DEC_BATCH
    xb = xb_ref[0:nb, :]
    xc = sconv_ref[:, 0:LRU_WIDTH] * cw_ref[0:1, :]
    xc = xc + sconv_ref[:, LRU_WIDTH:2 * LRU_WIDTH] * cw_ref[1:2, :]
    xc = xc + sconv_ref[:, 2 * LRU_WIDTH:3 * LRU_WIDTH] * cw_ref[2:3, :]
    xc = xc + xb * cw_ref[3:4, :]
    xc = xc + cb_ref[...]
    for n in range(N_LRU_BLOCKS):
        cols = slice(n * LRU_BLOCK, (n + 1) * LRU_BLOCK)
        a, b = _gate_ab(xc[:, cols], n, wgx_ref, bgx_ref, wga_ref, bga_ref, ap_ref)
        h = a * sh_ref[:, cols] + b
        hs_ref[:, cols] = h
        lru_ref[0:nb, cols] = (jax.nn.gelu(yb_ref[0:nb, cols]) * h).astype(BF16)
    rows = EXTRA_ROWS - nb
    ext_ref[0:SUBLANES, :] = jnp.zeros((SUBLANES, LRU_WIDTH), F32)
    h3_ref[...] = jnp.zeros(h3_ref.shape, F32)
    first_group = (rows - N_META) // SUBLANES
    hmeta_ref[...] = _lru_rows(rows, ext_ref, a3_ref, b3_ref, h3_ref, xb_ref[nb:, :], yb_ref.at[nb:, :], cw_ref,
                               cb_ref, wgx_ref, bgx_ref, wga_ref, bga_ref, ap_ref, lru_ref.at[nb:, :],
                               jnp.zeros((SUBLANES, LANES), F32), first_group, rows // SUBLANES)


def _lru_extra(proj_e, sconv, sh, cw, cb, wgx, bgx, wga, bga, ap):
    const = lambda shape: pl.BlockSpec(shape, lambda i: (0,) * len(shape))
    rows = EXTRA_ROWS - DEC_BATCH
    scan_shape = (rows // SUBLANES, N_LRU_BLOCKS * SUBLANES, LANES)
    return pl.pallas_call(
        _lru_extra_kernel,
        grid=(1,),
        in_specs=[
            pl.BlockSpec((EXTRA_ROWS, LRU_WIDTH), lambda i: (0, COL_XB // LRU_WIDTH)),
            pl.BlockSpec((EXTRA_ROWS, LRU_WIDTH), lambda i: (0, COL_YB // LRU_WIDTH)),
            const((DEC_BATCH, (CONV_W - 1) * LRU_WIDTH)),
            const((DEC_BATCH, LRU_WIDTH)),
            const((CONV_W, LRU_WIDTH)),
            const((1, LRU_WIDTH)),
            const((N_LRU_BLOCKS, LRU_BLOCK, LRU_BLOCK)),
            const((1, LRU_WIDTH)),
            const((N_LRU_BLOCKS, LRU_BLOCK, LRU_BLOCK)),
            const((1, LRU_WIDTH)),
            const((1, LRU_WIDTH)),
        ],
        out_specs=[
            const((EXTRA_ROWS, LRU_WIDTH)),
            const((DEC_BATCH, LRU_WIDTH)),
            const((SUBLANES, LANES)),
        ],
        out_shape=[
            jax.ShapeDtypeStruct((EXTRA_ROWS, LRU_WIDTH), BF16),
            jax.ShapeDtypeStruct((DEC_BATCH, LRU_WIDTH), F32),
            jax.ShapeDtypeStruct((SUBLANES, LANES), F32),
        ],
        scratch_shapes=[
            pltpu.VMEM((rows + SUBLANES, LRU_WIDTH), F32),
            pltpu.VMEM(scan_shape, F32),
            pltpu.VMEM(scan_shape, F32),
            pltpu.VMEM(scan_shape, F32),
        ],
        compiler_params=_cparams(("arbitrary",), 8 * _nbytes((DEC_BATCH, LRU_WIDTH), F32),
                                 8 * _nbytes((rows, LRU_WIDTH), F32)),
        name="lru_extra",
    )(proj_e, proj_e, sconv, sh, cw, cb, wgx, bgx, wga, bga, ap)


def _attend_kv_head(q_h, kk, vv, bias, sinks):
    return _attend_values(*_attend_weights(q_h, kk, bias, sinks), vv)


def _attend_weights(q_h, kk, bias, sinks):
    r = q_h.shape[0]
    qs = jnp.concatenate([q_h[:, g * HEAD_DIM:(g + 1) * HEAD_DIM] for g in range(GROUP)], axis=0)
    qs = (qs * (HEAD_DIM ** -0.5)).astype(BF16)
    s = lax.dot_general(kk.astype(BF16), qs, (((1,), (1,)), ((), ())), preferred_element_type=F32) + bias
    sink = jnp.concatenate([jnp.full((1, r), sinks[g], F32) for g in range(GROUP)], axis=1)
    m = jnp.maximum(jnp.max(s, axis=0, keepdims=True), sink)
    e = jnp.exp(s - m)
    denom = jnp.sum(e, axis=0, keepdims=True) + jnp.exp(sink - m)
    return e.astype(BF16), denom


def _attend_values(e, denom, vv):
    r = e.shape[1] // GROUP
    o_t = jnp.dot(vv.T.astype(BF16), e, preferred_element_type=F32) / denom
    pairs = [jnp.concatenate([o_t[:, g * r:(g + 1) * r], o_t[:, (g + 1) * r:(g + 2) * r]], axis=0).T
             for g in range(0, GROUP, 2)]
    return jnp.concatenate(pairs, axis=-1)


ATTN_KEYS = N_META + 2 * BLOCK


ATTN_BLOCKS_PER_STEP = 2


ATTN_BIAS_SHAPE = (2, N_KV_HEADS, ATTN_KEYS, GROUP * BLOCK)


def _build_attn_bias(bias_ref):
    slopes = _slopes()
    key = lax.broadcasted_iota(jnp.int32, (ATTN_KEYS, BLOCK), 0)
    qry = lax.broadcasted_iota(jnp.int32, (ATTN_KEYS, BLOCK), 1)
    d = qry + BLOCK - (key - N_META)
    dist = jnp.where(key < N_META, 0, d).astype(F32)
    for has_prev in range(2):
        band_ok = (d >= 0) & (d <= WINDOW) & ((key >= N_META + BLOCK) | bool(has_prev))
        valid = (key < N_META) | band_ok
        for hd in range(N_HEADS):
            lanes = slice((hd % GROUP) * BLOCK, (hd % GROUP + 1) * BLOCK)
            bias_ref[has_prev, hd // GROUP, :, lanes] = jnp.where(valid, -(slopes[hd] * dist), NEG_INF)


def _attend_stages(tile, sink_ref, q_ref, kc_ref, kp_ref, vc_ref, vp_ref, km_ref, vm_ref, bias_ref, o_ref):
    def pair(sub, h):
        rows = slice(sub * BLOCK, (sub + 1) * BLOCK)
        prev_k, prev_v = (kp_ref, vp_ref) if sub == 0 else (kc_ref.at[(sub - 1) * BLOCK:sub * BLOCK],
                                                             vc_ref.at[(sub - 1) * BLOCK:sub * BLOCK])
        table = jnp.where(tile == 0, 0, 1) if sub == 0 else 1
        hc = slice(h * HEAD_DIM, (h + 1) * HEAD_DIM)
        qc = slice(h * GROUP * HEAD_DIM, (h + 1) * GROUP * HEAD_DIM)
        v = {}

        def weights():
            kk = jnp.concatenate([km_ref[:, hc], prev_k[:, hc], kc_ref[rows, hc]], axis=0)
            v["w"] = _attend_weights(q_ref[rows, qc], kk, bias_ref[table, h],
                                     [sink_ref[h * GROUP + g] for g in range(GROUP)])

        def values():
            vv = jnp.concatenate([vm_ref[:, hc], prev_v[:, hc], vc_ref[rows, hc]], axis=0)
            o_ref[rows, qc] = _attend_values(*v["w"], vv).astype(BF16)

        return weights, values

    return [pair(sub, h) for sub in range(ATTN_BLOCKS_PER_STEP) for h in range(N_KV_HEADS)]


def _attn_meta_kernel(sink_ref, q_ref, km_ref, vm_ref, o_ref):
    key = lax.broadcasted_iota(jnp.int32, (N_META, GROUP * N_META), 0)
    qry = lax.broadcasted_iota(jnp.int32, (N_META, GROUP * N_META), 1) % N_META
    bias = jnp.where(qry >= key, 0.0, NEG_INF)
    for h in range(N_KV_HEADS):
        hc = slice(h * HEAD_DIM, (h + 1) * HEAD_DIM)
        q_h = q_ref[:, h * GROUP * HEAD_DIM:(h + 1) * GROUP * HEAD_DIM]
        o = _attend_kv_head(q_h, km_ref[:, hc], vm_ref[:, hc], bias,
                            [sink_ref[h * GROUP + g] for g in range(GROUP)])
        o_ref[:, h * GROUP * HEAD_DIM:(h + 1) * GROUP * HEAD_DIM] = o


def _attn_meta(sinks, q_meta, k_meta, v_meta):
    const = lambda shape: pl.BlockSpec(shape, lambda i: (0,) * len(shape))
    return pl.pallas_call(
        _attn_meta_kernel,
        grid=(1,),
        in_specs=[pl.BlockSpec(memory_space=pltpu.SMEM), const((N_META, ATTN_WIDTH)),
                  const((N_META, KV_WIDTH)), const((N_META, KV_WIDTH))],
        out_specs=const((N_META, ATTN_WIDTH)),
        out_shape=jax.ShapeDtypeStruct((N_META, ATTN_WIDTH), F32),
        compiler_params=_cparams(("arbitrary",), 4 * _nbytes((N_META, ATTN_WIDTH), F32)),
        name="attn_meta",
    )(sinks, q_meta, k_meta, v_meta)


def _attn_sample_kernel(q_ref, km_ref, kw_ref, kn_ref, vm_ref, vw_ref, vn_ref, bias_ref, sink_ref,
                        o_ref, kwo_ref, vwo_ref):
    for src_ref, new_ref, dst_ref in ((kw_ref, kn_ref, kwo_ref), (vw_ref, vn_ref, vwo_ref)):
        dst_ref[:, 0:WINDOW - 1, :] = src_ref[:, 1:WINDOW, :]
        dst_ref[:, WINDOW - 1:WINDOW, :] = new_ref[...]
    dot_qk = functools.partial(jnp.einsum, "bgd,bjd->bgj", preferred_element_type=F32)
    dot_pv = functools.partial(jnp.einsum, "bgj,bjd->bgd", preferred_element_type=F32)
    for h in range(N_KV_HEADS):
        hc = slice(h * HEAD_DIM, (h + 1) * HEAD_DIM)
        q = (q_ref[:, h] * (HEAD_DIM ** -0.5)).astype(BF16)
        k_new = kn_ref[:, :, hc].astype(BF16).astype(F32)
        v_new = vn_ref[:, :, hc].astype(BF16).astype(F32)
        s_m = dot_qk(q, km_ref[:, :, hc].astype(BF16))
        s_w = dot_qk(q, kw_ref[:, :, hc].astype(BF16)) + bias_ref[h]
        s_n = jnp.sum(q.astype(F32) * k_new, axis=-1, keepdims=True)
        sink = sink_ref[h]
        m = jnp.maximum(jnp.maximum(jnp.max(s_m, axis=-1, keepdims=True), jnp.max(s_w, axis=-1, keepdims=True)),
                        jnp.maximum(s_n, sink))
        e_m, e_w, e_n = jnp.exp(s_m - m), jnp.exp(s_w - m), jnp.exp(s_n - m)
        denom = (jnp.sum(e_m, axis=-1, keepdims=True) + jnp.sum(e_w, axis=-1, keepdims=True) + e_n
                 + jnp.exp(sink - m))
        o = (dot_pv(e_m.astype(BF16), vm_ref[:, :, hc].astype(BF16))
             + dot_pv(e_w.astype(BF16), vw_ref[:, :, hc].astype(BF16))
             + e_n.astype(BF16).astype(F32) * v_new)
        o_ref[:, h] = o / denom


def _attn_sample(q4, k_meta, k_win, k_new, v_meta, v_win, v_new, bias, sinks4, bb):
    b3 = lambda rows: pl.BlockSpec((bb, rows, KV_WIDTH), lambda i: (i, 0, 0))
    q_spec = pl.BlockSpec((bb, N_KV_HEADS, GROUP, HEAD_DIM), lambda i: (i, 0, 0, 0))
    win_shape = jax.ShapeDtypeStruct((DEC_BATCH, WINDOW, KV_WIDTH), F32)
    return pl.pallas_call(
        _attn_sample_kernel,
        grid=(DEC_BATCH // bb,),
        in_specs=[
            q_spec,
            b3(N_META), b3(WINDOW), b3(1),
            b3(N_META), b3(WINDOW), b3(1),
            pl.BlockSpec((N_KV_HEADS, GROUP, WINDOW), lambda i: (0, 0, 0)),
            pl.BlockSpec((N_KV_HEADS, GROUP, 1), lambda i: (0, 0, 0)),
        ],
        out_specs=[q_spec, b3(WINDOW), b3(WINDOW)],
        out_shape=[jax.ShapeDtypeStruct((DEC_BATCH, N_KV_HEADS, GROUP, HEAD_DIM), F32), win_shape, win_shape],
        compiler_params=_cparams(("arbitrary",), 5 * _nbytes((bb, WINDOW, KV_WIDTH), F32),
                                 4 * _nbytes((bb, WINDOW, KV_WIDTH), F32)),
        name="attn_sample",
    )(q4, k_meta, k_win, k_new, v_meta, v_win, v_new, bias, sinks4)


N_GATE_CHUNKS = D_MODEL // GATE_CHUNK


def _outproj_stages(lru_ref, attn_ref, gl_refs, ga_refs, x_ref, g2_ref, wl_ref, wa_ref, wo_ref, h_ref, hn_ref):
    v = {"a": [], "l": [], "ss": 0.0}
    half = D_MODEL // 2

    def branch_proj(key, src_ref, w_ref, part):
        if part == 0:
            v[key + "_in"] = src_ref[...]
        v[key].append(jnp.dot(v[key + "_in"], w_ref[:, part * half:(part + 1) * half], preferred_element_type=F32))

    def merge():
        a, l = (jnp.concatenate(v[key], axis=1) for key in ("a", "l"))
        v["m"] = jnp.concatenate(
            [jax.nn.sigmoid(gl_refs[c][...]) * l[:, c * GATE_CHUNK:(c + 1) * GATE_CHUNK]
             + jax.nn.sigmoid(ga_refs[c][...]) * a[:, c * GATE_CHUNK:(c + 1) * GATE_CHUNK]
             for c in range(N_GATE_CHUNKS)], axis=1).astype(BF16)

    def out_chunk(c):
        cols = slice(c * GATE_CHUNK, (c + 1) * GATE_CHUNK)
        h = x_ref[:, cols] + jnp.dot(v["m"], wo_ref[:, cols], preferred_element_type=F32)
        h_ref[:, cols] = h
        v["ss"] = v["ss"] + jnp.sum(h * h, axis=-1, keepdims=True)

    def norm():
        scale = lax.rsqrt(v["ss"] * (1.0 / D_MODEL) + EPS)
        hn_ref[...] = (h_ref[...] * scale * g2_ref[...]).astype(BF16)

    def seq(*stages):
        return lambda: [stage() for stage in stages]

    return [functools.partial(branch_proj, "a", attn_ref, wa_ref, 0),
            functools.partial(branch_proj, "a", attn_ref, wa_ref, 1),
            functools.partial(branch_proj, "l", lru_ref, wl_ref, 0),
            functools.partial(branch_proj, "l", lru_ref, wl_ref, 1),
            seq(merge, functools.partial(out_chunk, 0)),
            functools.partial(out_chunk, 1),
            functools.partial(out_chunk, 2),
            seq(functools.partial(out_chunk, 3), norm)]


def _run_interleaved(project, attend, before=()):
    for i in range(max(len(project), len(attend), len(before))):
        if i < len(attend):
            attend[i][0]()
        if i < len(before):
            before[i]()
        if i < len(project):
            project[i]()
        if i < len(attend):
            attend[i][1]()


def _split_outproj_refs(refs, n_cast):
    n = 2 * N_GATE_CHUNKS
    tile_refs = (refs[:N_GATE_CHUNKS], refs[N_GATE_CHUNKS:n], *refs[n:n + 5])
    rest = refs[n + 5:]
    cast_src, outs, cast_dst = rest[:n_cast], rest[n_cast:n_cast + 2], rest[n_cast + 2:2 * n_cast + 2]
    return tile_refs, cast_src, outs, cast_dst, rest[2 * n_cast + 2:]


def _outproj_kernel(lru_ref, attn_ref, *refs, n_cast):
    tile_refs, cast_src, outs, cast_dst, _ = _split_outproj_refs(refs, n_cast)
    _run_interleaved(_outproj_stages(lru_ref, attn_ref, *tile_refs, *outs), [])
    _cast_chunks(cast_src, cast_dst)


MIXER_TILE = ATTN_BLOCKS_PER_STEP * BLOCK
N_MIXER_INPUTS = 19


def _mixers_outproj_kernel(*refs, n_cast):
    (sink_ref, q_ref, kc_ref, kp_ref, vc_ref, vp_ref, km_ref, vm_ref,
     xb_ref, yb_ref, tail_ref, h0_ref, *lru_params) = refs[:N_MIXER_INPUTS]
    n = 2 * N_GATE_CHUNKS
    rest = refs[N_MIXER_INPUTS:]
    tile_refs = (rest[:N_GATE_CHUNKS], rest[N_GATE_CHUNKS:n], *rest[n:n + 5])
    rest = rest[n + 5:]
    cast_src, (h_ref, hn_ref, hlast_ref), cast_dst = rest[:n_cast], rest[n_cast:n_cast + 3], rest[n_cast + 3:2 * n_cast + 3]
    bias_ref, attn_ref, lru_ref, ext_ref, a3_ref, b3_ref, h3_ref, hstate_ref = rest[2 * n_cast + 3:]
    rows = MIXER_TILE
    t = pl.program_id(0)
    last = pl.num_programs(0) - 1

    attend = lambda: _attend_stages(t, sink_ref, q_ref, kc_ref, kp_ref, vc_ref, vp_ref, km_ref, vm_ref,
                                    bias_ref, attn_ref)
    project = lambda: _outproj_stages(lru_ref, attn_ref, *tile_refs, h_ref, hn_ref)

    def coeffs(n_block):
        def stage():
            _lru_coeffs_block(n_block, rows, ext_ref, a3_ref, b3_ref, xb_ref, *lru_params)
            cols = slice(n_block * LRU_BLOCK, (n_block + 1) * LRU_BLOCK)
            ext_ref[0:SUBLANES, cols] = ext_ref[rows:rows + SUBLANES, cols]
        return stage

    def gate_out(blocks):
        return lambda: _lru_gate_out(rows, h3_ref, yb_ref, lru_ref, blocks)

    def seq(*stages):
        return lambda: [stage() for stage in stages]

    def scan():
        h = _lru_scan(a3_ref, b3_ref, h3_ref, hstate_ref[...], 0, rows // SUBLANES)
        hstate_ref[...] = h
        hlast_ref[...] = h

    half = N_LRU_BLOCKS // 2
    all_coeffs = [coeffs(n_block) for n_block in range(N_LRU_BLOCKS)]

    @pl.when(t == 0)
    def _():
        _build_attn_bias(bias_ref)
        ext_ref[0:SUBLANES, :] = tail_ref[...]
        hstate_ref[...] = h0_ref[...]
        _run_interleaved([], attend(), all_coeffs)
        scan()

    @pl.when((t > 0) & (t < last))
    def _():
        before = [seq(gate_out(range(half)), all_coeffs[0]), seq(gate_out(range(half, N_LRU_BLOCKS)), all_coeffs[1]),
                  *all_coeffs[2:]]
        _run_interleaved(project(), attend(), before)
        _cast_chunks(cast_src, cast_dst)
        scan()

    @pl.when(t == last)
    def _():
        gate_out(range(N_LRU_BLOCKS))()
        _run_interleaved(project(), [])
        _cast_chunks(cast_src, cast_dst)


def _mixers_outproj(sinks, proj, k_meta, v_meta, tail, h0, lru_params, x, g2, wl, wa, wo, cast_weights):
    m = x.shape[0]
    tm = MIXER_TILE
    tiles = m // tm
    assert tiles >= CAST_STEPS
    cur = lambda t: jnp.minimum(t, tiles - 1)
    prv = lambda t: jnp.maximum(t - 1, 0)
    const = lambda shape: pl.BlockSpec(shape, lambda t: (0,) * len(shape))
    resident = lambda shape: pl.BlockSpec(shape, lambda t: (0,) * len(shape), pipeline_mode=pl.Buffered(1))
    kv = lambda col: (pl.BlockSpec((tm, KV_WIDTH), lambda t: (cur(t), col // KV_WIDTH)),
                      pl.BlockSpec((BLOCK, KV_WIDTH),
                                   lambda t: (jnp.maximum(ATTN_BLOCKS_PER_STEP * cur(t) - 1, 0), col // KV_WIDTH)))
    kc, kp = kv(COL_K)
    vc, vp = kv(COL_V)
    prv_spec = lambda width, cb=0: pl.BlockSpec((tm, width), lambda t: (prv(t), cb))
    cast_specs, cast_shapes, cast_bytes = _cast_specs(cast_weights, prv)
    tile = _nbytes((tm, D_MODEL), F32)
    scan_shape = (tm // SUBLANES, N_LRU_BLOCKS * SUBLANES, LANES)
    windows = 9 * tile + cast_bytes
    weights = _nbytes(wl.shape, BF16) + _nbytes(wa.shape, BF16) + _nbytes(wo.shape, BF16)
    scratch = weights + 6 * tile + 2 * _nbytes(ATTN_BIAS_SHAPE, F32)
    return pl.pallas_call(
        functools.partial(_mixers_outproj_kernel, n_cast=len(cast_weights)),
        grid=(tiles + 1,),
        in_specs=[
            pl.BlockSpec(memory_space=pltpu.SMEM),
            pl.BlockSpec((tm, ATTN_WIDTH), lambda t: (cur(t), COL_Q // ATTN_WIDTH)),
            kc, kp, vc, vp,
            const((N_META, KV_WIDTH)),
            const((N_META, KV_WIDTH)),
            pl.BlockSpec((tm, LRU_WIDTH), lambda t: (cur(t), COL_XB // LRU_WIDTH)),
            prv_spec(LRU_WIDTH, COL_YB // LRU_WIDTH),
            const((SUBLANES, LRU_WIDTH)),
            const((SUBLANES, LANES)),
            *[const(p.shape) for p in lru_params],
            *[prv_spec(GATE_CHUNK, col // GATE_CHUNK + c) for col in (COL_GL, COL_GA) for c in range(N_GATE_CHUNKS)],
            prv_spec(D_MODEL),
            const((1, D_MODEL)),
            resident((LRU_WIDTH, D_MODEL)),
            resident((ATTN_WIDTH, D_MODEL)),
            resident((D_MODEL, D_MODEL)),
            *cast_specs,
        ],
        out_specs=[prv_spec(D_MODEL), prv_spec(D_MODEL), const((SUBLANES, LANES)), *cast_specs],
        out_shape=[jax.ShapeDtypeStruct((m, D_MODEL), F32), jax.ShapeDtypeStruct((m, D_MODEL), BF16),
                   jax.ShapeDtypeStruct((SUBLANES, LANES), F32), *cast_shapes],
        scratch_shapes=[
            pltpu.VMEM(ATTN_BIAS_SHAPE, F32),
            pltpu.VMEM((tm, ATTN_WIDTH), BF16),
            pltpu.VMEM((tm, LRU_WIDTH), BF16),
            pltpu.VMEM((tm + SUBLANES, LRU_WIDTH), F32),
            pltpu.VMEM(scan_shape, F32),
            pltpu.VMEM(scan_shape, F32),
            pltpu.VMEM(scan_shape, F32),
            pltpu.VMEM((SUBLANES, LANES), F32),
        ],
        compiler_params=_cparams(("arbitrary",), windows, scratch),
        name="mixers_outproj",
    )(sinks, proj, proj, proj, proj, proj, k_meta, v_meta, proj, proj, tail, h0, *lru_params,
      *([proj] * (2 * N_GATE_CHUNKS)), x, g2, wl, wa, wo, *cast_weights)


def _outproj(lru, attn, proj, x, g2, wl, wa, wo, cast_weights=()):
    m = x.shape[0]
    tm = _tile_plan(m)["outproj"]
    const = lambda shape: pl.BlockSpec(shape, lambda i: (0,) * len(shape))
    cast_specs, cast_shapes, cast_bytes = _cast_specs(cast_weights, lambda i: i)
    assert not cast_weights or m // tm >= CAST_STEPS
    tile = _nbytes((tm, D_MODEL), F32)
    windows = 6 * tile + _nbytes(wl.shape, BF16) + _nbytes(wa.shape, BF16) + _nbytes(wo.shape, BF16) + cast_bytes
    row_spec = lambda width: pl.BlockSpec((tm, width), lambda i: (i, 0))
    return pl.pallas_call(
        functools.partial(_outproj_kernel, n_cast=len(cast_weights)),
        grid=(m // tm,),
        in_specs=[
            row_spec(LRU_WIDTH),
            row_spec(ATTN_WIDTH),
            *[pl.BlockSpec((tm, GATE_CHUNK), functools.partial(lambda i, cb: (i, cb), cb=col // GATE_CHUNK + c))
              for col in (COL_GL, COL_GA) for c in range(N_GATE_CHUNKS)],
            row_spec(D_MODEL),
            const((1, D_MODEL)),
            const((LRU_WIDTH, D_MODEL)),
            const((ATTN_WIDTH, D_MODEL)),
            const((D_MODEL, D_MODEL)),
            *cast_specs,
        ],
        out_specs=[row_spec(D_MODEL), row_spec(D_MODEL), *cast_specs],
        out_shape=[jax.ShapeDtypeStruct((m, D_MODEL), F32), jax.ShapeDtypeStruct((m, D_MODEL), BF16), *cast_shapes],
        compiler_params=_cparams(("arbitrary",), windows, 3 * tile),
        name=f"outproj_{m}",
    )(lru, attn, *([proj] * (2 * N_GATE_CHUNKS)), x, g2, wl, wa, wo, *cast_weights)


MLP_EPILOGUE_ROWS = 256


def _mlp_kernel(h_ref, hn_ref, gf_ref, wu_ref, wd_ref, o_ref):
    f = pl.program_id(1)
    last = pl.num_programs(1) - 1

    def chunk(rows=slice(None)):
        u = jnp.dot(hn_ref[rows, :], wu_ref[...], preferred_element_type=F32)
        u = jnp.square(jnp.maximum(u, 0.0))
        return jnp.dot(u.astype(BF16), wd_ref[...], preferred_element_type=F32)

    @pl.when(f == 0)
    def _():
        o_ref[...] = chunk()

    @pl.when((f > 0) & (f < last))
    def _():
        o_ref[...] += chunk()

    @pl.when(f == last)
    def _():
        step = min(MLP_EPILOGUE_ROWS, o_ref.shape[0])
        for r0 in range(0, o_ref.shape[0], step):
            rows = slice(r0, r0 + step)
            out = h_ref[rows, :] + (o_ref[rows, :] + chunk(rows))
            ms = jnp.mean(out * out, axis=-1, keepdims=True)
            o_ref[rows, :] = out * lax.rsqrt(ms + EPS) * gf_ref[...]


def _mlp(h, hn, gf, wu, wd):
    m = h.shape[0]
    tm, tf = _tile_plan(m)["mlp"]
    windows = (2 * _nbytes((tm, D_MODEL), F32) + _nbytes((tm, D_MODEL), BF16) + 2 * _nbytes((D_MODEL, tf), BF16))
    n_f = D_FF // tf
    return pl.pallas_call(
        _mlp_kernel,
        grid=(m // tm, n_f),
        in_specs=[
            pl.BlockSpec((tm, D_MODEL), lambda i, f: (jnp.where(f == n_f - 1, i, jnp.maximum(i - 1, 0)), 0)),
            pl.BlockSpec((tm, D_MODEL), lambda i, f: (i, 0)),
            pl.BlockSpec((1, D_MODEL), lambda i, f: (0, 0)),
            pl.BlockSpec((D_MODEL, tf), lambda i, f: (0, f)),
            pl.BlockSpec((tf, D_MODEL), lambda i, f: (f, 0)),
        ],
        out_specs=pl.BlockSpec((tm, D_MODEL), lambda i, f: (i, 0)),
        out_shape=jax.ShapeDtypeStruct((m, D_MODEL), F32),
        compiler_params=_cparams(("arbitrary", "arbitrary"), windows, 2 * _nbytes((tm, tf), F32)),
        name=f"mlp_{m}",
    )(h, hn, gf, wu, wd)


def _sample_bias():
    dist = (WINDOW - np.arange(WINDOW)).astype(np.float32)
    slopes = np.asarray(_slopes(), np.float32).reshape(N_KV_HEADS, GROUP, 1)
    return jnp.asarray(-(slopes * dist[None, None, :]))


def kernel(x_prompt, x_sample, cache_meta_k, cache_meta_v, cache_win_k, cache_win_v, state_conv, state_h,
           meta_tokens, norm1_g, w_in, conv_w, conv_b, w_gate_x, b_gate_x, w_gate_a, b_gate_a, lru_a_param,
           attn_sinks, w_lru_out, w_attn_out, w_o, norm2_g, w_mlp_up, w_mlp_down, final_norm_g):
    row = lambda v: v.reshape(1, -1)
    g1, g2, gf = row(norm1_g[0]), row(norm2_g[0]), row(final_norm_g)
    wgx, wga = w_gate_x[0].astype(BF16), w_gate_a[0].astype(BF16)
    cw, cb = conv_w[0], row(conv_b[0])
    bgx, bga, ap = row(b_gate_x[0]), row(b_gate_a[0]), row(lru_a_param[0])
    sinks = attn_sinks[0]

    x_main = x_prompt.reshape(SEQ, D_MODEL)
    x_extra = jnp.concatenate([
        x_sample.reshape(DEC_BATCH, D_MODEL),
        jnp.zeros((EXTRA_ROWS - DEC_BATCH - N_META, D_MODEL), F32),
        meta_tokens.astype(F32)], axis=0)

    proj_e, w_in_b = _inproj(x_extra, g1, w_in[0])
    proj_m, wl, wa, wo, wu, wd = _inproj(
        x_main, g1, w_in_b, (w_lru_out[0], w_attn_out[0], w_o[0], w_mlp_up[0], w_mlp_down[0]))

    lru_e, h_sample, h_meta = _lru_extra(proj_e, state_conv[0].reshape(DEC_BATCH, (CONV_W - 1) * LRU_WIDTH),
                                         state_h[0], cw, cb, wgx, bgx, wga, bga, ap)
    tail = proj_e[EXTRA_ROWS - SUBLANES:, COL_XB:COL_XB + LRU_WIDTH]

    k_meta = proj_e[META_ROW0:, COL_K:COL_K + KV_WIDTH]
    v_meta = proj_e[META_ROW0:, COL_V:COL_V + KV_WIDTH]
    attn_meta = _attn_meta(sinks, proj_e[META_ROW0:, COL_Q:COL_Q + ATTN_WIDTH], k_meta, v_meta)
    k_new = proj_e[:DEC_BATCH, None, COL_K:COL_K + KV_WIDTH]
    v_new = proj_e[:DEC_BATCH, None, COL_V:COL_V + KV_WIDTH]
    q4 = proj_e[:DEC_BATCH, COL_Q:COL_Q + ATTN_WIDTH].reshape(DEC_BATCH, N_KV_HEADS, GROUP, HEAD_DIM)
    flat = lambda c, n: c[0].reshape(DEC_BATCH, n, KV_WIDTH)
    attn_s, kw_out, vw_out = _attn_sample(
        q4, flat(cache_meta_k, N_META), flat(cache_win_k, WINDOW), k_new,
        flat(cache_meta_v, N_META), flat(cache_win_v, WINDOW), v_new,
        _sample_bias(), sinks.reshape(N_KV_HEADS, GROUP, 1), bb=8)
    attn_e = jnp.concatenate([
        attn_s.reshape(DEC_BATCH, ATTN_WIDTH),
        jnp.zeros((EXTRA_ROWS - DEC_BATCH - N_META, ATTN_WIDTH), F32),
        attn_meta], axis=0).astype(BF16)

    res_m, resn_m, h_last = _mixers_outproj(
        sinks, proj_m, k_meta, v_meta, tail, h_meta, (cw, cb, wgx, bgx, wga, bga, ap), x_main, g2, wl, wa, wo, ())
    res_e, resn_e = _outproj(lru_e, attn_e, proj_e, x_extra, g2, wl, wa, wo)
    y_m = _mlp(res_m, resn_m, gf, wu, wd)
    y_e = _mlp(res_e, resn_e, gf, wu, wd)

    kv5 = lambda a, n: a.reshape(1, -1, n, N_KV_HEADS, HEAD_DIM)
    return (
        y_m.reshape(1, SEQ, D_MODEL),
        y_e[:DEC_BATCH].reshape(DEC_BATCH, 1, D_MODEL),
        kv5(k_meta, N_META), kv5(v_meta, N_META),
        kv5(proj_m[SEQ - WINDOW:, COL_K:COL_K + KV_WIDTH], WINDOW),
        kv5(proj_m[SEQ - WINDOW:, COL_V:COL_V + KV_WIDTH], WINDOW),
        proj_m[SEQ - (CONV_W - 1):, COL_XB:COL_XB + LRU_WIDTH].reshape(1, 1, CONV_W - 1, LRU_WIDTH),
        h_last.reshape(1, 1, LRU_WIDTH),
        kv5(kw_out, WINDOW), kv5(vw_out, WINDOW),
        jnp.concatenate([state_conv[0][:, 1:], proj_e[:DEC_BATCH, None, COL_XB:COL_XB + LRU_WIDTH]], axis=1)[None],
        h_sample[None],
    )
```

```python
import functools

import numpy as np
import jax
import jax.numpy as jnp
from jax import lax
from jax.experimental import pallas as pl
from jax.experimental.pallas import tpu as pltpu

D_MODEL = 2048
SEQ = 16384
DEC_BATCH = 128
N_META = 16
LRU_WIDTH = 1024
N_LRU_BLOCKS = 8
LRU_BLOCK = LRU_WIDTH // N_LRU_BLOCKS
CONV_W = 4
LRU_C = 8.0
N_HEADS = 16
N_KV_HEADS = 4
HEAD_DIM = 64
GROUP = N_HEADS // N_KV_HEADS
ATTN_WIDTH = N_HEADS * HEAD_DIM
KV_WIDTH = N_KV_HEADS * HEAD_DIM
WINDOW = 128
BLOCK = 128
D_FF = 4 * D_MODEL
EPS = 1e-6
NEG_INF = -1e30
IN_WIDTH = 2 * LRU_WIDTH + ATTN_WIDTH + 2 * KV_WIDTH + 2 * D_MODEL
COL_XB, COL_YB, COL_Q = 0, LRU_WIDTH, 2 * LRU_WIDTH
COL_K = COL_Q + ATTN_WIDTH
COL_V = COL_K + KV_WIDTH
COL_GL = COL_V + KV_WIDTH
COL_GA = COL_GL + D_MODEL
GATE_CHUNK = 512

EXTRA_ROWS = 256
META_ROW0 = EXTRA_ROWS - N_META
SUBLANES = 8
LANES = 128
MIB = 1024 * 1024

F32 = jnp.float32
BF16 = jnp.bfloat16


def _slopes():
    return [2.0 ** (-8.0 * (h + 1) / N_HEADS) for h in range(N_HEADS)]


V7X_VMEM_REQUEST_CAP = 60 * MIB


def _nbytes(shape, dtype):
    return int(np.prod(shape)) * jnp.dtype(dtype).itemsize


def _cparams(sem, pipelined, resident=0):
    estimate = 2 * pipelined + resident
    limit = min(V7X_VMEM_REQUEST_CAP, estimate + estimate // 4 + 2 * MIB)
    return pltpu.CompilerParams(dimension_semantics=sem, vmem_limit_bytes=limit)


def _tile_plan(rows):
    if rows == EXTRA_ROWS:
        return dict(inproj=(EXTRA_ROWS, 1536), outproj=EXTRA_ROWS, mlp=(EXTRA_ROWS, 2048))
    return dict(inproj=(1024, 1536), lru=512, outproj=256, mlp=(1024, 512))


CAST_STEPS = 64


def _cast_specs(weights, step_of):
    specs, shapes, nbytes = [], [], 0
    for w in weights:
        block = (w.shape[0] // CAST_STEPS, w.shape[1])
        specs.append(pl.BlockSpec(block, lambda *idx: (jnp.minimum(step_of(*idx), CAST_STEPS - 1), 0)))
        shapes.append(jax.ShapeDtypeStruct(w.shape, BF16))
        nbytes += _nbytes(block, F32) + _nbytes(block, BF16)
    return specs, shapes, nbytes


def _cast_chunks(src_refs, dst_refs):
    for src_ref, dst_ref in zip(src_refs, dst_refs):
        dst_ref[...] = src_ref[...].astype(BF16)


def _inproj_kernel(x_ref, g_ref, w_ref, *refs, n_cast, emit_w):
    cast_src, o_ref, cast_dst, xs_ref = refs[:n_cast], refs[n_cast], refs[n_cast + 1:2 * n_cast + 1], refs[-1]

    @pl.when(pl.program_id(1) == 0)
    def _():
        x = x_ref[...]
        ms = jnp.mean(x * x, axis=-1, keepdims=True)
        xs_ref[...] = (x * lax.rsqrt(ms + EPS) * g_ref[...]).astype(BF16)

    w = w_ref[...]
    if emit_w:
        w = w.astype(BF16)
        refs[-2][...] = w
    o_ref[...] = jnp.dot(xs_ref[...], w, preferred_element_type=F32)
    _cast_chunks(cast_src, cast_dst)


def _inproj(x, g, w, cast_weights=()):
    m = x.shape[0]
    tm, tn = _tile_plan(m)["inproj"]
    n_col = IN_WIDTH // tn
    emit_w = w.dtype == F32
    assert not emit_w or m == tm, "the bf16 copy of w is written once per column block"
    cast_specs, cast_shapes, cast_bytes = _cast_specs(cast_weights, lambda i, j: i * n_col + j)
    assert not cast_weights or (m // tm) * n_col >= CAST_STEPS
    w_spec = pl.BlockSpec((D_MODEL, tn), lambda i, j: (0, j))
    windows = (_nbytes((tm, D_MODEL), F32) + _nbytes((D_MODEL, tn), w.dtype) + _nbytes((tm, tn), F32) + cast_bytes
               + emit_w * _nbytes((D_MODEL, tn), BF16))
    return pl.pallas_call(
        functools.partial(_inproj_kernel, n_cast=len(cast_weights), emit_w=emit_w),
        grid=(m // tm, n_col),
        in_specs=[
            pl.BlockSpec((tm, D_MODEL), lambda i, j: (i, 0)),
            pl.BlockSpec((1, D_MODEL), lambda i, j: (0, 0)),
            w_spec,
            *cast_specs,
        ],
        out_specs=[pl.BlockSpec((tm, tn), lambda i, j: (i, j)), *cast_specs, *([w_spec] if emit_w else [])],
        out_shape=[jax.ShapeDtypeStruct((m, IN_WIDTH), F32), *cast_shapes,
                   *([jax.ShapeDtypeStruct(w.shape, BF16)] if emit_w else [])],
        scratch_shapes=[pltpu.VMEM((tm, D_MODEL), BF16)],
        compiler_params=_cparams(("arbitrary", "arbitrary"), windows, _nbytes((tm, D_MODEL), BF16)),
        name=f"inproj_{m}",
    )(x, g, w, *cast_weights)


def _gate_ab(xc_n, n, wgx_ref, bgx_ref, wga_ref, bga_ref, ap_ref):
    cols = slice(n * LRU_BLOCK, (n + 1) * LRU_BLOCK)
    xcb = xc_n.astype(BF16)
    gx = jnp.dot(xcb, wgx_ref[n], preferred_element_type=F32) + bgx_ref[:, cols]
    ga = jnp.dot(xcb, wga_ref[n], preferred_element_type=F32) + bga_ref[:, cols]
    gate_x = jax.nn.sigmoid(gx)
    gate_a = jax.nn.sigmoid(ga)
    log_a = -LRU_C * gate_a * jax.nn.softplus(-ap_ref[:, cols])
    a = jnp.exp(log_a)
    z = -jnp.tanh(log_a) * (a * a + 1.0)
    root = jnp.where(z > 0.0, z * lax.rsqrt(z), 0.0)
    b = root * gate_x * xc_n
    return a, b


def _lru_coeffs(rows, ext_ref, a3_ref, b3_ref, xb, cw_ref, cb_ref, wgx_ref, bgx_ref, wga_ref, bga_ref, ap_ref):
    for n in range(N_LRU_BLOCKS):
        _lru_coeffs_block(n, rows, ext_ref, a3_ref, b3_ref, xb, cw_ref, cb_ref, wgx_ref, bgx_ref, wga_ref, bga_ref,
                          ap_ref)


def _lru_coeffs_block(n, rows, ext_ref, a3_ref, b3_ref, xb, cw_ref, cb_ref, wgx_ref, bgx_ref, wga_ref, bga_ref,
                      ap_ref):
    groups = rows // SUBLANES
    cols = slice(n * LRU_BLOCK, (n + 1) * LRU_BLOCK)
    ext_ref[SUBLANES:SUBLANES + rows, cols] = xb[:, cols]
    x_ext = ext_ref[:, cols]
    xc = x_ext * cw_ref[0:1, cols]
    for t in range(1, CONV_W):
        xc = pltpu.roll(xc, 1, axis=0) + x_ext * cw_ref[t:t + 1, cols]
    xc = xc[SUBLANES:, :] + cb_ref[:, cols]
    a, b = _gate_ab(xc, n, wgx_ref, bgx_ref, wga_ref, bga_ref, ap_ref)
    a3_ref[:, n * SUBLANES:(n + 1) * SUBLANES, :] = a.reshape(groups, SUBLANES, LANES)
    b3_ref[:, n * SUBLANES:(n + 1) * SUBLANES, :] = b.reshape(groups, SUBLANES, LANES)


def _lru_scan(a3_ref, b3_ref, h3_ref, h_init, g_lo, g_hi):
    def group_step(i, h):
        for r in range(0, SUBLANES, 2):
            s0, s1 = (pl.ds(r + k, N_LRU_BLOCKS, stride=SUBLANES) for k in range(2))
            a0, b0, a1, b1 = a3_ref[i, s0, :], b3_ref[i, s0, :], a3_ref[i, s1, :], b3_ref[i, s1, :]
            h3_ref[i, s0, :] = a0 * h + b0
            h = (a1 * a0) * h + (a1 * b0 + b1)
            h3_ref[i, s1, :] = h
        return h

    return lax.fori_loop(g_lo, g_hi, group_step, h_init)


def _lru_gate_out(rows, h3_ref, yb_ref, lru_ref, blocks=range(N_LRU_BLOCKS)):
    for n in blocks:
        cols = slice(n * LRU_BLOCK, (n + 1) * LRU_BLOCK)
        hs = h3_ref[:, n * SUBLANES:(n + 1) * SUBLANES, :].reshape(rows, LANES)
        lru_ref[:, cols] = (jax.nn.gelu(yb_ref[:, cols]) * hs).astype(BF16)


def _lru_rows(rows, ext_ref, a3_ref, b3_ref, h3_ref, xb, yb_ref, cw_ref, cb_ref,
              wgx_ref, bgx_ref, wga_ref, bga_ref, ap_ref, lru_ref, h_init, g_lo, g_hi):
    _lru_coeffs(rows, ext_ref, a3_ref, b3_ref, xb, cw_ref, cb_ref, wgx_ref, bgx_ref, wga_ref, bga_ref, ap_ref)
    h = _lru_scan(a3_ref, b3_ref, h3_ref, h_init, g_lo, g_hi)
    _lru_gate_out(rows, h3_ref, yb_ref, lru_ref)
    return h


def _lru_extra_kernel(xb_ref, yb_ref, sconv_ref, sh_ref, cw_ref, cb_ref, wgx_ref, bgx_ref, wga_ref, bga_ref,
                      ap_ref, lru_ref, hs_ref, hmeta_ref, ext_ref, a3_ref, b3_ref, h3_ref):
    nb = DEC_BATCH
    xb = xb_ref[0:nb, :]
    xc = sconv_ref[:, 0:LRU_WIDTH] * cw_ref[0:1, :]
    xc = xc + sconv_ref[:, LRU_WIDTH:2 * LRU_WIDTH] * cw_ref[1:2, :]
    xc = xc + sconv_ref[:, 2 * LRU_WIDTH:3 * LRU_WIDTH] * cw_ref[2:3, :]
    xc = xc + xb * cw_ref[3:4, :]
    xc = xc + cb_ref[...]
    for n in range(N_LRU_BLOCKS):
        cols = slice(n * LRU_BLOCK, (n + 1) * LRU_BLOCK)
        a, b = _gate_ab(xc[:, cols], n, wgx_ref, bgx_ref, wga_ref, bga_ref, ap_ref)
        h = a * sh_ref[:, cols] + b
        hs_ref[:, cols] = h
        lru_ref[0:nb, cols] = (jax.nn.gelu(yb_ref[0:nb, cols]) * h).astype(BF16)
    rows = EXTRA_ROWS - nb
    ext_ref[0:SUBLANES, :] = jnp.zeros((SUBLANES, LRU_WIDTH), F32)
    h3_ref[...] = jnp.zeros(h3_ref.shape, F32)
    first_group = (rows - N_META) // SUBLANES
    hmeta_ref[...] = _lru_rows(rows, ext_ref, a3_ref, b3_ref, h3_ref, xb_ref[nb:, :], yb_ref.at[nb:, :], cw_ref,
                               cb_ref, wgx_ref, bgx_ref, wga_ref, bga_ref, ap_ref, lru_ref.at[nb:, :],
                               jnp.zeros((SUBLANES, LANES), F32), first_group, rows // SUBLANES)


def _lru_extra(proj_e, sconv, sh, cw, cb, wgx, bgx, wga, bga, ap):
    const = lambda shape: pl.BlockSpec(shape, lambda i: (0,) * len(shape))
    rows = EXTRA_ROWS - DEC_BATCH
    scan_shape = (rows // SUBLANES, N_LRU_BLOCKS * SUBLANES, LANES)
    return pl.pallas_call(
        _lru_extra_kernel,
        grid=(1,),
        in_specs=[
            pl.BlockSpec((EXTRA_ROWS, LRU_WIDTH), lambda i: (0, COL_XB // LRU_WIDTH)),
            pl.BlockSpec((EXTRA_ROWS, LRU_WIDTH), lambda i: (0, COL_YB // LRU_WIDTH)),
            const((DEC_BATCH, (CONV_W - 1) * LRU_WIDTH)),
            const((DEC_BATCH, LRU_WIDTH)),
            const((CONV_W, LRU_WIDTH)),
            const((1, LRU_WIDTH)),
            const((N_LRU_BLOCKS, LRU_BLOCK, LRU_BLOCK)),
            const((1, LRU_WIDTH)),
            const((N_LRU_BLOCKS, LRU_BLOCK, LRU_BLOCK)),
            const((1, LRU_WIDTH)),
            const((1, LRU_WIDTH)),
        ],
        out_specs=[
            const((EXTRA_ROWS, LRU_WIDTH)),
            const((DEC_BATCH, LRU_WIDTH)),
            const((SUBLANES, LANES)),
        ],
        out_shape=[
            jax.ShapeDtypeStruct((EXTRA_ROWS, LRU_WIDTH), BF16),
            jax.ShapeDtypeStruct((DEC_BATCH, LRU_WIDTH), F32),
            jax.ShapeDtypeStruct((SUBLANES, LANES), F32),
        ],
        scratch_shapes=[
            pltpu.VMEM((rows + SUBLANES, LRU_WIDTH), F32),
            pltpu.VMEM(scan_shape, F32),
            pltpu.VMEM(scan_shape, F32),
            pltpu.VMEM(scan_shape, F32),
        ],
        compiler_params=_cparams(("arbitrary",), 8 * _nbytes((DEC_BATCH, LRU_WIDTH), F32),
                                 8 * _nbytes((rows, LRU_WIDTH), F32)),
        name="lru_extra",
    )(proj_e, proj_e, sconv, sh, cw, cb, wgx, bgx, wga, bga, ap)


def _attend_kv_head(q_h, kk, vv, bias, sinks):
    return _attend_values(*_attend_weights(q_h, kk, bias, sinks), vv)


def _attend_weights(q_h, kk, bias, sinks):
    r = q_h.shape[0]
    qs = jnp.concatenate([q_h[:, g * HEAD_DIM:(g + 1) * HEAD_DIM] for g in range(GROUP)], axis=0)
    qs = (qs * (HEAD_DIM ** -0.5)).astype(BF16)
    s = lax.dot_general(kk.astype(BF16), qs, (((1,), (1,)), ((), ())), preferred_element_type=F32) + bias
    sink = jnp.concatenate([jnp.full((1, r), sinks[g], F32) for g in range(GROUP)], axis=1)
    m = jnp.maximum(jnp.max(s, axis=0, keepdims=True), sink)
    e = jnp.exp(s - m)
    denom = jnp.sum(e, axis=0, keepdims=True) + jnp.exp(sink - m)
    return e.astype(BF16), denom


def _attend_values(e, denom, vv):
    r = e.shape[1] // GROUP
    o_t = jnp.dot(vv.T.astype(BF16), e, preferred_element_type=F32) / denom
    pairs = [jnp.concatenate([o_t[:, g * r:(g + 1) * r], o_t[:, (g + 1) * r:(g + 2) * r]], axis=0).T
             for g in range(0, GROUP, 2)]
    return jnp.concatenate(pairs, axis=-1)


ATTN_KEYS = N_META + 2 * BLOCK


ATTN_BLOCKS_PER_STEP = 2


ATTN_BIAS_SHAPE = (2, N_KV_HEADS, ATTN_KEYS, GROUP * BLOCK)


def _build_attn_bias(bias_ref):
    slopes = _slopes()
    key = lax.broadcasted_iota(jnp.int32, (ATTN_KEYS, BLOCK), 0)
    qry = lax.broadcasted_iota(jnp.int32, (ATTN_KEYS, BLOCK), 1)
    d = qry + BLOCK - (key - N_META)
    dist = jnp.where(key < N_META, 0, d).astype(F32)
    for has_prev in range(2):
        band_ok = (d >= 0) & (d <= WINDOW) & ((key >= N_META + BLOCK) | bool(has_prev))
        valid = (key < N_META) | band_ok
        for hd in range(N_HEADS):
            lanes = slice((hd % GROUP) * BLOCK, (hd % GROUP + 1) * BLOCK)
            bias_ref[has_prev, hd // GROUP, :, lanes] = jnp.where(valid, -(slopes[hd] * dist), NEG_INF)


def _attend_stages(tile, sink_ref, q_ref, kc_ref, kp_ref, vc_ref, vp_ref, km_ref, vm_ref, bias_ref, o_ref):
    def pair(sub, h):
        rows = slice(sub * BLOCK, (sub + 1) * BLOCK)
        prev_k, prev_v = (kp_ref, vp_ref) if sub == 0 else (kc_ref.at[(sub - 1) * BLOCK:sub * BLOCK],
                                                             vc_ref.at[(sub - 1) * BLOCK:sub * BLOCK])
        table = jnp.where(tile == 0, 0, 1) if sub == 0 else 1
        hc = slice(h * HEAD_DIM, (h + 1) * HEAD_DIM)
        qc = slice(h * GROUP * HEAD_DIM, (h + 1) * GROUP * HEAD_DIM)
        v = {}

        def weights():
            kk = jnp.concatenate([km_ref[:, hc], prev_k[:, hc], kc_ref[rows, hc]], axis=0)
            v["w"] = _attend_weights(q_ref[rows, qc], kk, bias_ref[table, h],
                                     [sink_ref[h * GROUP + g] for g in range(GROUP)])

        def values():
            vv = jnp.concatenate([vm_ref[:, hc], prev_v[:, hc], vc_ref[rows, hc]], axis=0)
            o_ref[rows, qc] = _attend_values(*v["w"], vv).astype(BF16)

        return weights, values

    return [pair(sub, h) for sub in range(ATTN_BLOCKS_PER_STEP) for h in range(N_KV_HEADS)]


def _attn_meta_kernel(sink_ref, q_ref, km_ref, vm_ref, o_ref):
    key = lax.broadcasted_iota(jnp.int32, (N_META, GROUP * N_META), 0)
    qry = lax.broadcasted_iota(jnp.int32, (N_META, GROUP * N_META), 1) % N_META
    bias = jnp.where(qry >= key, 0.0, NEG_INF)
    for h in range(N_KV_HEADS):
        hc = slice(h * HEAD_DIM, (h + 1) * HEAD_DIM)
        q_h = q_ref[:, h * GROUP * HEAD_DIM:(h + 1) * GROUP * HEAD_DIM]
        o = _attend_kv_head(q_h, km_ref[:, hc], vm_ref[:, hc], bias,
                            [sink_ref[h * GROUP + g] for g in range(GROUP)])
        o_ref[:, h * GROUP * HEAD_DIM:(h + 1) * GROUP * HEAD_DIM] = o


def _attn_meta(sinks, q_meta, k_meta, v_meta):
    const = lambda shape: pl.BlockSpec(shape, lambda i: (0,) * len(shape))
    return pl.pallas_call(
        _attn_meta_kernel,
        grid=(1,),
        in_specs=[pl.BlockSpec(memory_space=pltpu.SMEM), const((N_META, ATTN_WIDTH)),
                  const((N_META, KV_WIDTH)), const((N_META, KV_WIDTH))],
        out_specs=const((N_META, ATTN_WIDTH)),
        out_shape=jax.ShapeDtypeStruct((N_META, ATTN_WIDTH), F32),
        compiler_params=_cparams(("arbitrary",), 4 * _nbytes((N_META, ATTN_WIDTH), F32)),
        name="attn_meta",
    )(sinks, q_meta, k_meta, v_meta)


def _attn_sample_kernel(q_ref, km_ref, kw_ref, kn_ref, vm_ref, vw_ref, vn_ref, bias_ref, sink_ref,
                        o_ref, kwo_ref, vwo_ref):
    for src_ref, new_ref, dst_ref in ((kw_ref, kn_ref, kwo_ref), (vw_ref, vn_ref, vwo_ref)):
        dst_ref[:, 0:WINDOW - 1, :] = src_ref[:, 1:WINDOW, :]
        dst_ref[:, WINDOW - 1:WINDOW, :] = new_ref[...]
    dot_qk = functools.partial(jnp.einsum, "bgd,bjd->bgj", preferred_element_type=F32)
    dot_pv = functools.partial(jnp.einsum, "bgj,bjd->bgd", preferred_element_type=F32)
    for h in range(N_KV_HEADS):
        hc = slice(h * HEAD_DIM, (h + 1) * HEAD_DIM)
        q = (q_ref[:, h] * (HEAD_DIM ** -0.5)).astype(BF16)
        k_new = kn_ref[:, :, hc].astype(BF16).astype(F32)
        v_new = vn_ref[:, :, hc].astype(BF16).astype(F32)
        s_m = dot_qk(q, km_ref[:, :, hc].astype(BF16))
        s_w = dot_qk(q, kw_ref[:, :, hc].astype(BF16)) + bias_ref[h]
        s_n = jnp.sum(q.astype(F32) * k_new, axis=-1, keepdims=True)
        sink = sink_ref[h]
        m = jnp.maximum(jnp.maximum(jnp.max(s_m, axis=-1, keepdims=True), jnp.max(s_w, axis=-1, keepdims=True)),
                        jnp.maximum(s_n, sink))
        e_m, e_w, e_n = jnp.exp(s_m - m), jnp.exp(s_w - m), jnp.exp(s_n - m)
        denom = (jnp.sum(e_m, axis=-1, keepdims=True) + jnp.sum(e_w, axis=-1, keepdims=True) + e_n
                 + jnp.exp(sink - m))
        o = (dot_pv(e_m.astype(BF16), vm_ref[:, :, hc].astype(BF16))
             + dot_pv(e_w.astype(BF16), vw_ref[:, :, hc].astype(BF16))
             + e_n.astype(BF16).astype(F32) * v_new)
        o_ref[:, h] = o / denom


def _attn_sample(q4, k_meta, k_win, k_new, v_meta, v_win, v_new, bias, sinks4, bb):
    b3 = lambda rows: pl.BlockSpec((bb, rows, KV_WIDTH), lambda i: (i, 0, 0))
    q_spec = pl.BlockSpec((bb, N_KV_HEADS, GROUP, HEAD_DIM), lambda i: (i, 0, 0, 0))
    win_shape = jax.ShapeDtypeStruct((DEC_BATCH, WINDOW, KV_WIDTH), F32)
    return pl.pallas_call(
        _attn_sample_kernel,
        grid=(DEC_BATCH // bb,),
        in_specs=[
            q_spec,
            b3(N_META), b3(WINDOW), b3(1),
            b3(N_META), b3(WINDOW), b3(1),
            pl.BlockSpec((N_KV_HEADS, GROUP, WINDOW), lambda i: (0, 0, 0)),
            pl.BlockSpec((N_KV_HEADS, GROUP, 1), lambda i: (0, 0, 0)),
        ],
        out_specs=[q_spec, b3(WINDOW), b3(WINDOW)],
        out_shape=[jax.ShapeDtypeStruct((DEC_BATCH, N_KV_HEADS, GROUP, HEAD_DIM), F32), win_shape, win_shape],
        compiler_params=_cparams(("arbitrary",), 5 * _nbytes((bb, WINDOW, KV_WIDTH), F32),
                                 4 * _nbytes((bb, WINDOW, KV_WIDTH), F32)),
        name="attn_sample",
    )(q4, k_meta, k_win, k_new, v_meta, v_win, v_new, bias, sinks4)


N_GATE_CHUNKS = D_MODEL // GATE_CHUNK


def _outproj_stages(lru_ref, attn_ref, gl_refs, ga_refs, x_ref, g2_ref, wl_ref, wa_ref, wo_ref, h_ref, hn_ref):
    v = {"a": [], "l": [], "ss": 0.0}
    half = D_MODEL // 2

    def branch_proj(key, src_ref, w_ref, part):
        if part == 0:
            v[key + "_in"] = src_ref[...]
        v[key].append(jnp.dot(v[key + "_in"], w_ref[:, part * half:(part + 1) * half], preferred_element_type=F32))

    def merge():
        a, l = (jnp.concatenate(v[key], axis=1) for key in ("a", "l"))
        v["m"] = jnp.concatenate(
            [jax.nn.sigmoid(gl_refs[c][...]) * l[:, c * GATE_CHUNK:(c + 1) * GATE_CHUNK]
             + jax.nn.sigmoid(ga_refs[c][...]) * a[:, c * GATE_CHUNK:(c + 1) * GATE_CHUNK]
             for c in range(N_GATE_CHUNKS)], axis=1).astype(BF16)

    def out_chunk(c):
        cols = slice(c * GATE_CHUNK, (c + 1) * GATE_CHUNK)
        h = x_ref[:, cols] + jnp.dot(v["m"], wo_ref[:, cols], preferred_element_type=F32)
        h_ref[:, cols] = h
        v["ss"] = v["ss"] + jnp.sum(h * h, axis=-1, keepdims=True)

    def norm():
        scale = lax.rsqrt(v["ss"] * (1.0 / D_MODEL) + EPS)
        hn_ref[...] = (h_ref[...] * scale * g2_ref[...]).astype(BF16)

    def seq(*stages):
        return lambda: [stage() for stage in stages]

    return [functools.partial(branch_proj, "a", attn_ref, wa_ref, 0),
            functools.partial(branch_proj, "a", attn_ref, wa_ref, 1),
            functools.partial(branch_proj, "l", lru_ref, wl_ref, 0),
            functools.partial(branch_proj, "l", lru_ref, wl_ref, 1),
            seq(merge, functools.partial(out_chunk, 0)),
            functools.partial(out_chunk, 1),
            functools.partial(out_chunk, 2),
            seq(functools.partial(out_chunk, 3), norm)]


def _run_interleaved(project, attend, before=()):
    for i in range(max(len(project), len(attend), len(before))):
        if i < len(attend):
            attend[i][0]()
        if i < len(before):
            before[i]()
        if i < len(project):
            project[i]()
        if i < len(attend):
            attend[i][1]()


def _outproj_kernel(lru_ref, attn_ref, *refs):
    n = 2 * N_GATE_CHUNKS
    tile_refs = (refs[:N_GATE_CHUNKS], refs[N_GATE_CHUNKS:n], *refs[n:n + 5])
    _run_interleaved(_outproj_stages(lru_ref, attn_ref, *tile_refs, *refs[n + 5:]), [])


MIXER_TILE = ATTN_BLOCKS_PER_STEP * BLOCK
N_MIXER_INPUTS = 19


def _mixers_outproj_kernel(*refs):
    (sink_ref, q_ref, kc_ref, kp_ref, vc_ref, vp_ref, km_ref, vm_ref,
     xb_ref, yb_ref, tail_ref, h0_ref, *lru_params) = refs[:N_MIXER_INPUTS]
    n = 2 * N_GATE_CHUNKS
    rest = refs[N_MIXER_INPUTS:]
    tile_refs = (rest[:N_GATE_CHUNKS], rest[N_GATE_CHUNKS:n], *rest[n:n + 5])
    h_ref, hn_ref, hlast_ref, bias_ref, attn_ref, lru_ref, ext_ref, a3_ref, b3_ref, h3_ref, hstate_ref = rest[n + 5:]
    rows = MIXER_TILE
    t = pl.program_id(0)
    last = pl.num_programs(0) - 1

    attend = lambda: _attend_stages(t, sink_ref, q_ref, kc_ref, kp_ref, vc_ref, vp_ref, km_ref, vm_ref,
                                    bias_ref, attn_ref)
    project = lambda: _outproj_stages(lru_ref, attn_ref, *tile_refs, h_ref, hn_ref)

    def coeffs(n_block):
        def stage():
            _lru_coeffs_block(n_block, rows, ext_ref, a3_ref, b3_ref, xb_ref, *lru_params)
            cols = slice(n_block * LRU_BLOCK, (n_block + 1) * LRU_BLOCK)
            ext_ref[0:SUBLANES, cols] = ext_ref[rows:rows + SUBLANES, cols]
        return stage

    def gate_out(blocks):
        return lambda: _lru_gate_out(rows, h3_ref, yb_ref, lru_ref, blocks)

    def seq(*stages):
        return lambda: [stage() for stage in stages]

    def scan():
        h = _lru_scan(a3_ref, b3_ref, h3_ref, hstate_ref[...], 0, rows // SUBLANES)
        hstate_ref[...] = h
        hlast_ref[...] = h

    half = N_LRU_BLOCKS // 2
    all_coeffs = [coeffs(n_block) for n_block in range(N_LRU_BLOCKS)]

    @pl.when(t == 0)
    def _():
        _build_attn_bias(bias_ref)
        ext_ref[0:SUBLANES, :] = tail_ref[...]
        hstate_ref[...] = h0_ref[...]
        _run_interleaved([], attend(), all_coeffs)
        scan()

    @pl.when((t > 0) & (t < last))
    def _():
        before = [seq(gate_out(range(half)), all_coeffs[0]), seq(gate_out(range(half, N_LRU_BLOCKS)), all_coeffs[1]),
                  *all_coeffs[2:]]
        _run_interleaved(project(), attend(), before)
        scan()

    @pl.when(t == last)
    def _():
        gate_out(range(N_LRU_BLOCKS))()
        _run_interleaved(project(), [])


def _mixers_outproj(sinks, proj, k_meta, v_meta, tail, h0, lru_params, x, g2, wl, wa, wo):
    m = x.shape[0]
    tm = MIXER_TILE
    tiles = m // tm
    cur = lambda t: jnp.minimum(t, tiles - 1)
    prv = lambda t: jnp.maximum(t - 1, 0)
    const = lambda shape: pl.BlockSpec(shape, lambda t: (0,) * len(shape))
    resident = lambda shape: pl.BlockSpec(shape, lambda t: (0,) * len(shape), pipeline_mode=pl.Buffered(1))
    kv = lambda col: (pl.BlockSpec((tm, KV_WIDTH), lambda t: (cur(t), col // KV_WIDTH)),
                      pl.BlockSpec((BLOCK, KV_WIDTH),
                                   lambda t: (jnp.maximum(ATTN_BLOCKS_PER_STEP * cur(t) - 1, 0), col // KV_WIDTH)))
    kc, kp = kv(COL_K)
    vc, vp = kv(COL_V)
    prv_spec = lambda width, cb=0: pl.BlockSpec((tm, width), lambda t: (prv(t), cb))
    tile = _nbytes((tm, D_MODEL), F32)
    scan_shape = (tm // SUBLANES, N_LRU_BLOCKS * SUBLANES, LANES)
    windows = 9 * tile
    weights = _nbytes(wl.shape, BF16) + _nbytes(wa.shape, BF16) + _nbytes(wo.shape, BF16)
    scratch = weights + 6 * tile + 2 * _nbytes(ATTN_BIAS_SHAPE, F32)
    return pl.pallas_call(
        _mixers_outproj_kernel,
        grid=(tiles + 1,),
        in_specs=[
            pl.BlockSpec(memory_space=pltpu.SMEM),
            pl.BlockSpec((tm, ATTN_WIDTH), lambda t: (cur(t), COL_Q // ATTN_WIDTH)),
            kc, kp, vc, vp,
            const((N_META, KV_WIDTH)),
            const((N_META, KV_WIDTH)),
            pl.BlockSpec((tm, LRU_WIDTH), lambda t: (cur(t), COL_XB // LRU_WIDTH)),
            prv_spec(LRU_WIDTH, COL_YB // LRU_WIDTH),
            const((SUBLANES, LRU_WIDTH)),
            const((SUBLANES, LANES)),
            *[const(p.shape) for p in lru_params],
            *[prv_spec(GATE_CHUNK, col // GATE_CHUNK + c) for col in (COL_GL, COL_GA) for c in range(N_GATE_CHUNKS)],
            prv_spec(D_MODEL),
            const((1, D_MODEL)),
            resident((LRU_WIDTH, D_MODEL)),
            resident((ATTN_WIDTH, D_MODEL)),
            resident((D_MODEL, D_MODEL)),
        ],
        out_specs=[prv_spec(D_MODEL), prv_spec(D_MODEL), const((SUBLANES, LANES))],
        out_shape=[jax.ShapeDtypeStruct((m, D_MODEL), F32), jax.ShapeDtypeStruct((m, D_MODEL), BF16),
                   jax.ShapeDtypeStruct((SUBLANES, LANES), F32)],
        scratch_shapes=[
            pltpu.VMEM(ATTN_BIAS_SHAPE, F32),
            pltpu.VMEM((tm, ATTN_WIDTH), BF16),
            pltpu.VMEM((tm, LRU_WIDTH), BF16),
            pltpu.VMEM((tm + SUBLANES, LRU_WIDTH), F32),
            pltpu.VMEM(scan_shape, F32),
            pltpu.VMEM(scan_shape, F32),
            pltpu.VMEM(scan_shape, F32),
            pltpu.VMEM((SUBLANES, LANES), F32),
        ],
        compiler_params=_cparams(("arbitrary",), windows, scratch),
        name="mixers_outproj",
    )(sinks, proj, proj, proj, proj, proj, k_meta, v_meta, proj, proj, tail, h0, *lru_params,
      *([proj] * (2 * N_GATE_CHUNKS)), x, g2, wl, wa, wo)


def _outproj(lru, attn, proj, x, g2, wl, wa, wo):
    m = x.shape[0]
    tm = _tile_plan(m)["outproj"]
    const = lambda shape: pl.BlockSpec(shape, lambda i: (0,) * len(shape))
    tile = _nbytes((tm, D_MODEL), F32)
    windows = 6 * tile + _nbytes(wl.shape, BF16) + _nbytes(wa.shape, BF16) + _nbytes(wo.shape, BF16)
    row_spec = lambda width: pl.BlockSpec((tm, width), lambda i: (i, 0))
    return pl.pallas_call(
        _outproj_kernel,
        grid=(m // tm,),
        in_specs=[
            row_spec(LRU_WIDTH),
            row_spec(ATTN_WIDTH),
            *[pl.BlockSpec((tm, GATE_CHUNK), functools.partial(lambda i, cb: (i, cb), cb=col // GATE_CHUNK + c))
              for col in (COL_GL, COL_GA) for c in range(N_GATE_CHUNKS)],
            row_spec(D_MODEL),
            const((1, D_MODEL)),
            const((LRU_WIDTH, D_MODEL)),
            const((ATTN_WIDTH, D_MODEL)),
            const((D_MODEL, D_MODEL)),
        ],
        out_specs=[row_spec(D_MODEL), row_spec(D_MODEL)],
        out_shape=[jax.ShapeDtypeStruct((m, D_MODEL), F32), jax.ShapeDtypeStruct((m, D_MODEL), BF16)],
        compiler_params=_cparams(("arbitrary",), windows, 3 * tile),
        name=f"outproj_{m}",
    )(lru, attn, *([proj] * (2 * N_GATE_CHUNKS)), x, g2, wl, wa, wo)


MLP_EPILOGUE_ROWS = 256


def _mlp_kernel(h_ref, hn_ref, gf_ref, wu_ref, wd_ref, o_ref):
    f = pl.program_id(1)
    last = pl.num_programs(1) - 1

    def chunk(rows=slice(None)):
        u = jnp.dot(hn_ref[rows, :], wu_ref[...], preferred_element_type=F32)
        u = jnp.square(jnp.maximum(u, 0.0))
        return jnp.dot(u.astype(BF16), wd_ref[...], preferred_element_type=F32)

    @pl.when(f == 0)
    def _():
        o_ref[...] = chunk()

    @pl.when((f > 0) & (f < last))
    def _():
        o_ref[...] += chunk()

    @pl.when(f == last)
    def _():
        step = min(MLP_EPILOGUE_ROWS, o_ref.shape[0])
        for r0 in range(0, o_ref.shape[0], step):
            rows = slice(r0, r0 + step)
            out = h_ref[rows, :] + (o_ref[rows, :] + chunk(rows))
            ms = jnp.mean(out * out, axis=-1, keepdims=True)
            o_ref[rows, :] = out * lax.rsqrt(ms + EPS) * gf_ref[...]


def _mlp(h, hn, gf, wu, wd):
    m = h.shape[0]
    tm, tf = _tile_plan(m)["mlp"]
    windows = (2 * _nbytes((tm, D_MODEL), F32) + _nbytes((tm, D_MODEL), BF16) + 2 * _nbytes((D_MODEL, tf), BF16))
    n_f = D_FF // tf
    return pl.pallas_call(
        _mlp_kernel,
        grid=(m // tm, n_f),
        in_specs=[
            pl.BlockSpec((tm, D_MODEL), lambda i, f: (jnp.where(f == n_f - 1, i, jnp.maximum(i - 1, 0)), 0)),
            pl.BlockSpec((tm, D_MODEL), lambda i, f: (i, 0)),
            pl.BlockSpec((1, D_MODEL), lambda i, f: (0, 0)),
            pl.BlockSpec((D_MODEL, tf), lambda i, f: (0, f)),
            pl.BlockSpec((tf, D_MODEL), lambda i, f: (f, 0)),
        ],
        out_specs=pl.BlockSpec((tm, D_MODEL), lambda i, f: (i, 0)),
        out_shape=jax.ShapeDtypeStruct((m, D_MODEL), F32),
        compiler_params=_cparams(("arbitrary", "arbitrary"), windows, 2 * _nbytes((tm, tf), F32)),
        name=f"mlp_{m}",
    )(h, hn, gf, wu, wd)


def _sample_bias():
    dist = (WINDOW - np.arange(WINDOW)).astype(np.float32)
    slopes = np.asarray(_slopes(), np.float32).reshape(N_KV_HEADS, GROUP, 1)
    return jnp.asarray(-(slopes * dist[None, None, :]))


def kernel(x_prompt, x_sample, cache_meta_k, cache_meta_v, cache_win_k, cache_win_v, state_conv, state_h,
           meta_tokens, norm1_g, w_in, conv_w, conv_b, w_gate_x, b_gate_x, w_gate_a, b_gate_a, lru_a_param,
           attn_sinks, w_lru_out, w_attn_out, w_o, norm2_g, w_mlp_up, w_mlp_down, final_norm_g):
    row = lambda v: v.reshape(1, -1)
    g1, g2, gf = row(norm1_g[0]), row(norm2_g[0]), row(final_norm_g)
    wgx, wga = w_gate_x[0].astype(BF16), w_gate_a[0].astype(BF16)
    cw, cb = conv_w[0], row(conv_b[0])
    bgx, bga, ap = row(b_gate_x[0]), row(b_gate_a[0]), row(lru_a_param[0])
    sinks = attn_sinks[0]

    x_main = x_prompt.reshape(SEQ, D_MODEL)
    x_extra = jnp.concatenate([
        x_sample.reshape(DEC_BATCH, D_MODEL),
        jnp.zeros((EXTRA_ROWS - DEC_BATCH - N_META, D_MODEL), F32),
        meta_tokens.astype(F32)], axis=0)

    proj_e, w_in_b = _inproj(x_extra, g1, w_in[0])
    proj_m, wl, wa, wo, wu, wd = _inproj(
        x_main, g1, w_in_b, (w_lru_out[0], w_attn_out[0], w_o[0], w_mlp_up[0], w_mlp_down[0]))

    lru_e, h_sample, h_meta = _lru_extra(proj_e, state_conv[0].reshape(DEC_BATCH, (CONV_W - 1) * LRU_WIDTH),
                                         state_h[0], cw, cb, wgx, bgx, wga, bga, ap)
    tail = proj_e[EXTRA_ROWS - SUBLANES:, COL_XB:COL_XB + LRU_WIDTH]

    k_meta = proj_e[META_ROW0:, COL_K:COL_K + KV_WIDTH]
    v_meta = proj_e[META_ROW0:, COL_V:COL_V + KV_WIDTH]
    attn_meta = _attn_meta(sinks, proj_e[META_ROW0:, COL_Q:COL_Q + ATTN_WIDTH], k_meta, v_meta)
    k_new = proj_e[:DEC_BATCH, None, COL_K:COL_K + KV_WIDTH]
    v_new = proj_e[:DEC_BATCH, None, COL_V:COL_V + KV_WIDTH]
    q4 = proj_e[:DEC_BATCH, COL_Q:COL_Q + ATTN_WIDTH].reshape(DEC_BATCH, N_KV_HEADS, GROUP, HEAD_DIM)
    flat = lambda c, n: c[0].reshape(DEC_BATCH, n, KV_WIDTH)
    attn_s, kw_out, vw_out = _attn_sample(
        q4, flat(cache_meta_k, N_META), flat(cache_win_k, WINDOW), k_new,
        flat(cache_meta_v, N_META), flat(cache_win_v, WINDOW), v_new,
        _sample_bias(), sinks.reshape(N_KV_HEADS, GROUP, 1), bb=8)
    attn_e = jnp.concatenate([
        attn_s.reshape(DEC_BATCH, ATTN_WIDTH),
        jnp.zeros((EXTRA_ROWS - DEC_BATCH - N_META, ATTN_WIDTH), F32),
        attn_meta], axis=0).astype(BF16)

    res_m, resn_m, h_last = _mixers_outproj(
        sinks, proj_m, k_meta, v_meta, tail, h_meta, (cw, cb, wgx, bgx, wga, bga, ap), x_main, g2, wl, wa, wo)
    res_e, resn_e = _outproj(lru_e, attn_e, proj_e, x_extra, g2, wl, wa, wo)
    y_m = _mlp(res_m, resn_m, gf, wu, wd)
    y_e = _mlp(res_e, resn_e, gf, wu, wd)

    kv5 = lambda a, n: a.reshape(1, -1, n, N_KV_HEADS, HEAD_DIM)
    return (
        y_m.reshape(1, SEQ, D_MODEL),
        y_e[:DEC_BATCH].reshape(DEC_BATCH, 1, D_MODEL),
        kv5(k_meta, N_META), kv5(v_meta, N_META),
        kv5(proj_m[SEQ - WINDOW:, COL_K:COL_K + KV_WIDTH], WINDOW),
        kv5(proj_m[SEQ - WINDOW:, COL_V:COL_V + KV_WIDTH], WINDOW),
        proj_m[SEQ - (CONV_W - 1):, COL_XB:COL_XB + LRU_WIDTH].reshape(1, 1, CONV_W - 1, LRU_WIDTH),
        h_last.reshape(1, 1, LRU_WIDTH),
        kv5(kw_out, WINDOW), kv5(vw_out, WINDOW),
        jnp.concatenate([state_conv[0][:, 1:], proj_e[:DEC_BATCH, None, COL_XB:COL_XB + LRU_WIDTH]], axis=1)[None],
        h_sample[None],
    )
```

```python
import functools

import numpy as np
import jax
import jax.numpy as jnp
from jax import lax
from jax.experimental import pallas as pl
from jax.experimental.pallas import tpu as pltpu

D_MODEL = 2048
SEQ = 16384
DEC_BATCH = 128
N_META = 16
LRU_WIDTH = 1024
N_LRU_BLOCKS = 8
LRU_BLOCK = LRU_WIDTH // N_LRU_BLOCKS
CONV_W = 4
LRU_C = 8.0
N_HEADS = 16
N_KV_HEADS = 4
HEAD_DIM = 64
GROUP = N_HEADS // N_KV_HEADS
ATTN_WIDTH = N_HEADS * HEAD_DIM
KV_WIDTH = N_KV_HEADS * HEAD_DIM
WINDOW = 128
BLOCK = 128
D_FF = 4 * D_MODEL
EPS = 1e-6
NEG_INF = -1e30
IN_WIDTH = 2 * LRU_WIDTH + ATTN_WIDTH + 2 * KV_WIDTH + 2 * D_MODEL
COL_XB, COL_YB, COL_Q = 0, LRU_WIDTH, 2 * LRU_WIDTH
COL_K = COL_Q + ATTN_WIDTH
COL_V = COL_K + KV_WIDTH
COL_GL = COL_V + KV_WIDTH
COL_GA = COL_GL + D_MODEL
GATE_CHUNK = 512

EXTRA_ROWS = 256
META_ROW0 = EXTRA_ROWS - N_META
SUBLANES = 8
LANES = 128
MIB = 1024 * 1024

F32 = jnp.float32
BF16 = jnp.bfloat16


def _slopes():
    return [2.0 ** (-8.0 * (h + 1) / N_HEADS) for h in range(N_HEADS)]


V7X_VMEM_REQUEST_CAP = 60 * MIB


def _nbytes(shape, dtype):
    return int(np.prod(shape)) * jnp.dtype(dtype).itemsize


def _cparams(sem, pipelined, resident=0):
    estimate = 2 * pipelined + resident
    limit = min(V7X_VMEM_REQUEST_CAP, estimate + estimate // 4 + 2 * MIB)
    return pltpu.CompilerParams(dimension_semantics=sem, vmem_limit_bytes=limit)


def _tile_plan(rows):
    if rows == EXTRA_ROWS:
        return dict(inproj=(EXTRA_ROWS, 1536), outproj=EXTRA_ROWS, mlp=(EXTRA_ROWS, 2048))
    return dict(inproj=(1024, 1536), lru=512, outproj=256, mlp=(1024, 512))


CAST_STEPS = 64


def _cast_specs(weights, step_of):
    specs, shapes, nbytes = [], [], 0
    for w in weights:
        block = (w.shape[0] // CAST_STEPS, w.shape[1])
        specs.append(pl.BlockSpec(block, lambda *idx: (jnp.minimum(step_of(*idx), CAST_STEPS - 1), 0)))
        shapes.append(jax.ShapeDtypeStruct(w.shape, BF16))
        nbytes += _nbytes(block, F32) + _nbytes(block, BF16)
    return specs, shapes, nbytes


def _cast_chunks(src_refs, dst_refs):
    for src_ref, dst_ref in zip(src_refs, dst_refs):
        dst_ref[...] = src_ref[...].astype(BF16)


def _inproj_kernel(x_ref, g_ref, w_ref, *refs, n_cast, emit_w):
    cast_src, o_ref, cast_dst, xs_ref = refs[:n_cast], refs[n_cast], refs[n_cast + 1:2 * n_cast + 1], refs[-1]

    @pl.when(pl.program_id(1) == 0)
    def _():
        x = x_ref[...]
        ms = jnp.mean(x * x, axis=-1, keepdims=True)
        xs_ref[...] = (x * lax.rsqrt(ms + EPS) * g_ref[...]).astype(BF16)

    w = w_ref[...]
    if emit_w:
        w = w.astype(BF16)
        refs[-2][...] = w
    o_ref[...] = jnp.dot(xs_ref[...], w, preferred_element_type=F32)
    _cast_chunks(cast_src, cast_dst)


def _inproj(x, g, w, cast_weights=()):
    m = x.shape[0]
    tm, tn = _tile_plan(m)["inproj"]
    n_col = IN_WIDTH // tn
    emit_w = w.dtype == F32
    assert not emit_w or m == tm, "the bf16 copy of w is written once per column block"
    cast_specs, cast_shapes, cast_bytes = _cast_specs(cast_weights, lambda i, j: i * n_col + j)
    assert not cast_weights or (m // tm) * n_col >= CAST_STEPS
    w_spec = pl.BlockSpec((D_MODEL, tn), lambda i, j: (0, j))
    windows = (_nbytes((tm, D_MODEL), F32) + _nbytes((D_MODEL, tn), w.dtype) + _nbytes((tm, tn), F32) + cast_bytes
               + emit_w * _nbytes((D_MODEL, tn), BF16))
    return pl.pallas_call(
        functools.partial(_inproj_kernel, n_cast=len(cast_weights), emit_w=emit_w),
        grid=(m // tm, n_col),
        in_specs=[
            pl.BlockSpec((tm, D_MODEL), lambda i, j: (i, 0)),
            pl.BlockSpec((1, D_MODEL), lambda i, j: (0, 0)),
            w_spec,
            *cast_specs,
        ],
        out_specs=[pl.BlockSpec((tm, tn), lambda i, j: (i, j)), *cast_specs, *([w_spec] if emit_w else [])],
        out_shape=[jax.ShapeDtypeStruct((m, IN_WIDTH), F32), *cast_shapes,
                   *([jax.ShapeDtypeStruct(w.shape, BF16)] if emit_w else [])],
        scratch_shapes=[pltpu.VMEM((tm, D_MODEL), BF16)],
        compiler_params=_cparams(("arbitrary", "arbitrary"), windows, _nbytes((tm, D_MODEL), BF16)),
        name=f"inproj_{m}",
    )(x, g, w, *cast_weights)


def _gate_ab(xc_n, n, wgx_ref, bgx_ref, wga_ref, bga_ref, ap_ref):
    cols = slice(n * LRU_BLOCK, (n + 1) * LRU_BLOCK)
    xcb = xc_n.astype(BF16)
    gx = jnp.dot(xcb, wgx_ref[n], preferred_element_type=F32) + bgx_ref[:, cols]
    ga = jnp.dot(xcb, wga_ref[n], preferred_element_type=F32) + bga_ref[:, cols]
    gate_x = jax.nn.sigmoid(gx)
    gate_a = jax.nn.sigmoid(ga)
    log_a = -LRU_C * gate_a * jax.nn.softplus(-ap_ref[:, cols])
    a = jnp.exp(log_a)
    z = -jnp.tanh(log_a) * (a * a + 1.0)
    root = jnp.where(z > 0.0, z * lax.rsqrt(z), 0.0)
    b = root * gate_x * xc_n
    return a, b


def _lru_coeffs(rows, ext_ref, a3_ref, b3_ref, xb, cw_ref, cb_ref, wgx_ref, bgx_ref, wga_ref, bga_ref, ap_ref):
    for n in range(N_LRU_BLOCKS):
        _lru_coeffs_block(n, rows, ext_ref, a3_ref, b3_ref, xb, cw_ref, cb_ref, wgx_ref, bgx_ref, wga_ref, bga_ref,
                          ap_ref)


def _lru_coeffs_block(n, rows, ext_ref, a3_ref, b3_ref, xb, cw_ref, cb_ref, wgx_ref, bgx_ref, wga_ref, bga_ref,
                      ap_ref):
    groups = rows // SUBLANES
    cols = slice(n * LRU_BLOCK, (n + 1) * LRU_BLOCK)
    ext_ref[SUBLANES:SUBLANES + rows, cols] = xb[:, cols]
    x_ext = ext_ref[:, cols]
    xc = x_ext * cw_ref[0:1, cols]
    for t in range(1, CONV_W):
        xc = pltpu.roll(xc, 1, axis=0) + x_ext * cw_ref[t:t + 1, cols]
    xc = xc[SUBLANES:, :] + cb_ref[:, cols]
    a, b = _gate_ab(xc, n, wgx_ref, bgx_ref, wga_ref, bga_ref, ap_ref)
    a3_ref[:, n * SUBLANES:(n + 1) * SUBLANES, :] = a.reshape(groups, SUBLANES, LANES)
    b3_ref[:, n * SUBLANES:(n + 1) * SUBLANES, :] = b.reshape(groups, SUBLANES, LANES)


def _lru_scan(a3_ref, b3_ref, h3_ref, h_init, g_lo, g_hi):
    def group_step(i, h):
        for r in range(0, SUBLANES, 2):
            s0, s1 = (pl.ds(r + k, N_LRU_BLOCKS, stride=SUBLANES) for k in range(2))
            a0, b0, a1, b1 = a3_ref[i, s0, :], b3_ref[i, s0, :], a3_ref[i, s1, :], b3_ref[i, s1, :]
            h3_ref[i, s0, :] = a0 * h + b0
            h = (a1 * a0) * h + (a1 * b0 + b1)
            h3_ref[i, s1, :] = h
        return h

    return lax.fori_loop(g_lo, g_hi, group_step, h_init)


def _lru_gate_out(rows, h3_ref, yb_ref, lru_ref, blocks=range(N_LRU_BLOCKS)):
    for n in blocks:
        cols = slice(n * LRU_BLOCK, (n + 1) * LRU_BLOCK)
        hs = h3_ref[:, n * SUBLANES:(n + 1) * SUBLANES, :].reshape(rows, LANES)
        lru_ref[:, cols] = (jax.nn.gelu(yb_ref[:, cols]) * hs).astype(BF16)


def _lru_rows(rows, ext_ref, a3_ref, b3_ref, h3_ref, xb, yb_ref, cw_ref, cb_ref,
              wgx_ref, bgx_ref, wga_ref, bga_ref, ap_ref, lru_ref, h_init, g_lo, g_hi):
    _lru_coeffs(rows, ext_ref, a3_ref, b3_ref, xb, cw_ref, cb_ref, wgx_ref, bgx_ref, wga_ref, bga_ref, ap_ref)
    h = _lru_scan(a3_ref, b3_ref, h3_ref, h_init, g_lo, g_hi)
    _lru_gate_out(rows, h3_ref, yb_ref, lru_ref)
    return h


def _lru_extra_kernel(xb_ref, yb_ref, sconv_ref, sh_ref, cw_ref, cb_ref, wgx_ref, bgx_ref, wga_ref, bga_ref,
                      ap_ref, lru_ref, hs_ref, hmeta_ref, ext_ref, a3_ref, b3_ref, h3_ref):
    nb = DEC_BATCH
    xb = xb_ref[0:nb, :]
    xc = sconv_ref[:, 0:LRU_WIDTH] * cw_ref[0:1, :]
    xc = xc + sconv_ref[:, LRU_WIDTH:2 * LRU_WIDTH] * cw_ref[1:2, :]
    xc = xc + sconv_ref[:, 2 * LRU_WIDTH:3 * LRU_WIDTH] * cw_ref[2:3, :]
    xc = xc + xb * cw_ref[3:4, :]
    xc = xc + cb_ref[...]
    for n in range(N_LRU_BLOCKS):
        cols = slice(n * LRU_BLOCK, (n + 1) * LRU_BLOCK)
        a, b = _gate_ab(xc[:, cols], n, wgx_ref, bgx_ref, wga_ref, bga_ref, ap_ref)
        h = a * sh_ref[:, cols] + b
        hs_ref[:, cols] = h
        lru_ref[0:nb, cols] = (jax.nn.gelu(yb_ref[0:nb, cols]) * h).astype(BF16)
    rows = EXTRA_ROWS - nb
    ext_ref[0:SUBLANES, :] = jnp.zeros((SUBLANES, LRU_WIDTH), F32)
    h3_ref[...] = jnp.zeros(h3_ref.shape, F32)
    first_group = (rows - N_META) // SUBLANES
    hmeta_ref[...] = _lru_rows(rows, ext_ref, a3_ref, b3_ref, h3_ref, xb_ref[nb:, :], yb_ref.at[nb:, :], cw_ref,
                               cb_ref, wgx_ref, bgx_ref, wga_ref, bga_ref, ap_ref, lru_ref.at[nb:, :],
                               jnp.zeros((SUBLANES, LANES), F32), first_group, rows // SUBLANES)


def _lru_extra(proj_e, sconv, sh, cw, cb, wgx, bgx, wga, bga, ap):
    const = lambda shape: pl.BlockSpec(shape, lambda i: (0,) * len(shape))
    rows = EXTRA_ROWS - DEC_BATCH
    scan_shape = (rows // SUBLANES, N_LRU_BLOCKS * SUBLANES, LANES)
    return pl.pallas_call(
        _lru_extra_kernel,
        grid=(1,),
        in_specs=[
            pl.BlockSpec((EXTRA_ROWS, LRU_WIDTH), lambda i: (0, COL_XB // LRU_WIDTH)),
            pl.BlockSpec((EXTRA_ROWS, LRU_WIDTH), lambda i: (0, COL_YB // LRU_WIDTH)),
            const((DEC_BATCH, (CONV_W - 1) * LRU_WIDTH)),
            const((DEC_BATCH, LRU_WIDTH)),
            const((CONV_W, LRU_WIDTH)),
            const((1, LRU_WIDTH)),
            const((N_LRU_BLOCKS, LRU_BLOCK, LRU_BLOCK)),
            const((1, LRU_WIDTH)),
            const((N_LRU_BLOCKS, LRU_BLOCK, LRU_BLOCK)),
            const((1, LRU_WIDTH)),
            const((1, LRU_WIDTH)),
        ],
        out_specs=[
            const((EXTRA_ROWS, LRU_WIDTH)),
            const((DEC_BATCH, LRU_WIDTH)),
            const((SUBLANES, LANES)),
        ],
        out_shape=[
            jax.ShapeDtypeStruct((EXTRA_ROWS, LRU_WIDTH), BF16),
            jax.ShapeDtypeStruct((DEC_BATCH, LRU_WIDTH), F32),
            jax.ShapeDtypeStruct((SUBLANES, LANES), F32),
        ],
        scratch_shapes=[
            pltpu.VMEM((rows + SUBLANES, LRU_WIDTH), F32),
            pltpu.VMEM(scan_shape, F32),
            pltpu.VMEM(scan_shape, F32),
            pltpu.VMEM(scan_shape, F32),
        ],
        compiler_params=_cparams(("arbitrary",), 8 * _nbytes((DEC_BATCH, LRU_WIDTH), F32),
                                 8 * _nbytes((rows, LRU_WIDTH), F32)),
        name="lru_extra",
    )(proj_e, proj_e, sconv, sh, cw, cb, wgx, bgx, wga, bga, ap)


def _attend_kv_head(q_h, kk, vv, bias, sinks):
    return _attend_values(*_attend_weights(q_h, kk, bias, sinks), vv)


def _attend_weights(q_h, kk, bias, sinks):
    r = q_h.shape[0]
    qs = jnp.concatenate([q_h[:, g * HEAD_DIM:(g + 1) * HEAD_DIM] for g in range(GROUP)], axis=0)
    qs = (qs * (HEAD_DIM ** -0.5)).astype(BF16)
    s = lax.dot_general(kk.astype(BF16), qs, (((1,), (1,)), ((), ())), preferred_element_type=F32) + bias
    sink = jnp.concatenate([jnp.full((1, r), sinks[g], F32) for g in range(GROUP)], axis=1)
    m = jnp.maximum(jnp.max(s, axis=0, keepdims=True), sink)
    e = jnp.exp(s - m)
    denom = jnp.sum(e, axis=0, keepdims=True) + jnp.exp(sink - m)
    return e.astype(BF16), denom


def _attend_values(e, denom, vv):
    r = e.shape[1] // GROUP
    o_t = jnp.dot(vv.T.astype(BF16), e, preferred_element_type=F32) / denom
    pairs = [jnp.concatenate([o_t[:, g * r:(g + 1) * r], o_t[:, (g + 1) * r:(g + 2) * r]], axis=0).T
             for g in range(0, GROUP, 2)]
    return jnp.concatenate(pairs, axis=-1)


ATTN_BLOCKS_PER_STEP = 2
MIXER_TILE = ATTN_BLOCKS_PER_STEP * BLOCK
ATTN_QUERIES = BLOCK // 2
ATTN_BAND = WINDOW + ATTN_QUERIES
ATTN_KEYS = N_META + ATTN_BAND
ATTN_TABLES = WINDOW // ATTN_QUERIES + 1


ATTN_BIAS_SHAPE = (ATTN_TABLES, N_KV_HEADS, ATTN_KEYS, GROUP * ATTN_QUERIES)


def _build_attn_bias(bias_ref):
    slopes = _slopes()
    key = lax.broadcasted_iota(jnp.int32, (ATTN_KEYS, ATTN_QUERIES), 0)
    qry = lax.broadcasted_iota(jnp.int32, (ATTN_KEYS, ATTN_QUERIES), 1)
    d = qry + WINDOW - (key - N_META)
    dist = jnp.where(key < N_META, 0, d).astype(F32)
    for j in range(ATTN_TABLES):
        band_ok = (d >= 0) & (d <= WINDOW) & (key >= N_META + WINDOW - j * ATTN_QUERIES)
        valid = (key < N_META) | band_ok
        for hd in range(N_HEADS):
            lanes = slice((hd % GROUP) * ATTN_QUERIES, (hd % GROUP + 1) * ATTN_QUERIES)
            bias_ref[j, hd // GROUP, :, lanes] = jnp.where(valid, -(slopes[hd] * dist), NEG_INF)


def _attend_stages(tile, sink_ref, q_ref, kc_ref, kp_ref, vc_ref, vp_ref, km_ref, vm_ref, bias_ref, o_ref):
    def pair(sub, h):
        rows = slice(sub * ATTN_QUERIES, (sub + 1) * ATTN_QUERIES)
        lo = sub * ATTN_QUERIES - WINDOW
        n_before = max(-lo, 0) // ATTN_QUERIES
        table = jnp.where(tile == 0, ATTN_TABLES - 1 - n_before, ATTN_TABLES - 1) if n_before else ATTN_TABLES - 1
        hc = slice(h * HEAD_DIM, (h + 1) * HEAD_DIM)
        qc = slice(h * GROUP * HEAD_DIM, (h + 1) * GROUP * HEAD_DIM)
        v = {}

        def keys(meta_ref, cur_ref, prev_ref):
            if lo < 0:
                band = [prev_ref[WINDOW + lo:WINDOW, hc], cur_ref[0:lo + ATTN_BAND, hc]]
            else:
                band = [cur_ref[lo:lo + ATTN_BAND, hc]]
            return jnp.concatenate([meta_ref[:, hc], *band], axis=0)

        def weights():
            v["w"] = _attend_weights(q_ref[rows, qc], keys(km_ref, kc_ref, kp_ref), bias_ref[table, h],
                                     [sink_ref[h * GROUP + g] for g in range(GROUP)])

        def values():
            o_ref[rows, qc] = _attend_values(*v["w"], keys(vm_ref, vc_ref, vp_ref)).astype(BF16)

        return weights, values

    return [pair(sub, h) for sub in range(MIXER_TILE // ATTN_QUERIES) for h in range(N_KV_HEADS)]


def _attn_meta_kernel(sink_ref, q_ref, km_ref, vm_ref, o_ref):
    key = lax.broadcasted_iota(jnp.int32, (N_META, GROUP * N_META), 0)
    qry = lax.broadcasted_iota(jnp.int32, (N_META, GROUP * N_META), 1) % N_META
    bias = jnp.where(qry >= key, 0.0, NEG_INF)
    for h in range(N_KV_HEADS):
        hc = slice(h * HEAD_DIM, (h + 1) * HEAD_DIM)
        q_h = q_ref[:, h * GROUP * HEAD_DIM:(h + 1) * GROUP * HEAD_DIM]
        o = _attend_kv_head(q_h, km_ref[:, hc], vm_ref[:, hc], bias,
                            [sink_ref[h * GROUP + g] for g in range(GROUP)])
        o_ref[:, h * GROUP * HEAD_DIM:(h + 1) * GROUP * HEAD_DIM] = o


def _attn_meta(sinks, q_meta, k_meta, v_meta):
    const = lambda shape: pl.BlockSpec(shape, lambda i: (0,) * len(shape))
    return pl.pallas_call(
        _attn_meta_kernel,
        grid=(1,),
        in_specs=[pl.BlockSpec(memory_space=pltpu.SMEM), const((N_META, ATTN_WIDTH)),
                  const((N_META, KV_WIDTH)), const((N_META, KV_WIDTH))],
        out_specs=const((N_META, ATTN_WIDTH)),
        out_shape=jax.ShapeDtypeStruct((N_META, ATTN_WIDTH), F32),
        compiler_params=_cparams(("arbitrary",), 4 * _nbytes((N_META, ATTN_WIDTH), F32)),
        name="attn_meta",
    )(sinks, q_meta, k_meta, v_meta)


def _attn_sample_kernel(q_ref, km_ref, kw_ref, kn_ref, vm_ref, vw_ref, vn_ref, bias_ref, sink_ref,
                        o_ref, kwo_ref, vwo_ref):
    for src_ref, new_ref, dst_ref in ((kw_ref, kn_ref, kwo_ref), (vw_ref, vn_ref, vwo_ref)):
        dst_ref[:, 0:WINDOW - 1, :] = src_ref[:, 1:WINDOW, :]
        dst_ref[:, WINDOW - 1:WINDOW, :] = new_ref[...]
    dot_qk = functools.partial(jnp.einsum, "bgd,bjd->bgj", preferred_element_type=F32)
    dot_pv = functools.partial(jnp.einsum, "bgj,bjd->bgd", preferred_element_type=F32)
    for h in range(N_KV_HEADS):
        hc = slice(h * HEAD_DIM, (h + 1) * HEAD_DIM)
        q = (q_ref[:, h] * (HEAD_DIM ** -0.5)).astype(BF16)
        k_new = kn_ref[:, :, hc].astype(BF16).astype(F32)
        v_new = vn_ref[:, :, hc].astype(BF16).astype(F32)
        s_m = dot_qk(q, km_ref[:, :, hc].astype(BF16))
        s_w = dot_qk(q, kw_ref[:, :, hc].astype(BF16)) + bias_ref[h]
        s_n = jnp.sum(q.astype(F32) * k_new, axis=-1, keepdims=True)
        sink = sink_ref[h]
        m = jnp.maximum(jnp.maximum(jnp.max(s_m, axis=-1, keepdims=True), jnp.max(s_w, axis=-1, keepdims=True)),
                        jnp.maximum(s_n, sink))
        e_m, e_w, e_n = jnp.exp(s_m - m), jnp.exp(s_w - m), jnp.exp(s_n - m)
        denom = (jnp.sum(e_m, axis=-1, keepdims=True) + jnp.sum(e_w, axis=-1, keepdims=True) + e_n
                 + jnp.exp(sink - m))
        o = (dot_pv(e_m.astype(BF16), vm_ref[:, :, hc].astype(BF16))
             + dot_pv(e_w.astype(BF16), vw_ref[:, :, hc].astype(BF16))
             + e_n.astype(BF16).astype(F32) * v_new)
        o_ref[:, h] = o / denom


def _attn_sample(q4, k_meta, k_win, k_new, v_meta, v_win, v_new, bias, sinks4, bb):
    b3 = lambda rows: pl.BlockSpec((bb, rows, KV_WIDTH), lambda i: (i, 0, 0))
    q_spec = pl.BlockSpec((bb, N_KV_HEADS, GROUP, HEAD_DIM), lambda i: (i, 0, 0, 0))
    win_shape = jax.ShapeDtypeStruct((DEC_BATCH, WINDOW, KV_WIDTH), F32)
    return pl.pallas_call(
        _attn_sample_kernel,
        grid=(DEC_BATCH // bb,),
        in_specs=[
            q_spec,
            b3(N_META), b3(WINDOW), b3(1),
            b3(N_META), b3(WINDOW), b3(1),
            pl.BlockSpec((N_KV_HEADS, GROUP, WINDOW), lambda i: (0, 0, 0)),
            pl.BlockSpec((N_KV_HEADS, GROUP, 1), lambda i: (0, 0, 0)),
        ],
        out_specs=[q_spec, b3(WINDOW), b3(WINDOW)],
        out_shape=[jax.ShapeDtypeStruct((DEC_BATCH, N_KV_HEADS, GROUP, HEAD_DIM), F32), win_shape, win_shape],
        compiler_params=_cparams(("arbitrary",), 5 * _nbytes((bb, WINDOW, KV_WIDTH), F32),
                                 4 * _nbytes((bb, WINDOW, KV_WIDTH), F32)),
        name="attn_sample",
    )(q4, k_meta, k_win, k_new, v_meta, v_win, v_new, bias, sinks4)


N_GATE_CHUNKS = D_MODEL // GATE_CHUNK
OUTPROJ_BRANCH_PARTS = 4
OUTPROJ_CHUNK = D_MODEL // (2 * OUTPROJ_BRANCH_PARTS)


def _outproj_stages(lru_ref, attn_ref, gl_refs, ga_refs, x_ref, g2_ref, wl_ref, wa_ref, wo_ref, h_ref, hn_ref):
    v = {"a": [], "l": [], "ss": 0.0}
    part_cols = D_MODEL // OUTPROJ_BRANCH_PARTS

    def branch_proj(key, src_ref, w_ref, part):
        if part == 0:
            v[key + "_in"] = src_ref[...]
        cols = slice(part * part_cols, (part + 1) * part_cols)
        v[key].append(jnp.dot(v[key + "_in"], w_ref[:, cols], preferred_element_type=F32))

    def merge():
        a, l = (jnp.concatenate(v[key], axis=1) for key in ("a", "l"))
        v["m"] = jnp.concatenate(
            [jax.nn.sigmoid(gl_refs[c][...]) * l[:, c * GATE_CHUNK:(c + 1) * GATE_CHUNK]
             + jax.nn.sigmoid(ga_refs[c][...]) * a[:, c * GATE_CHUNK:(c + 1) * GATE_CHUNK]
             for c in range(N_GATE_CHUNKS)], axis=1).astype(BF16)

    def out_chunk(c):
        cols = slice(c * OUTPROJ_CHUNK, (c + 1) * OUTPROJ_CHUNK)
        h = x_ref[:, cols] + jnp.dot(v["m"], wo_ref[:, cols], preferred_element_type=F32)
        h_ref[:, cols] = h
        v["ss"] = v["ss"] + jnp.sum(h * h, axis=-1, keepdims=True)

    def norm():
        scale = lax.rsqrt(v["ss"] * (1.0 / D_MODEL) + EPS)
        hn_ref[...] = (h_ref[...] * scale * g2_ref[...]).astype(BF16)

    def seq(*stages):
        return lambda: [stage() for stage in stages]

    out_chunks = [functools.partial(out_chunk, c) for c in range(D_MODEL // OUTPROJ_CHUNK)]
    return [*[functools.partial(branch_proj, "a", attn_ref, wa_ref, part) for part in range(OUTPROJ_BRANCH_PARTS)],
            *[functools.partial(branch_proj, "l", lru_ref, wl_ref, part) for part in range(OUTPROJ_BRANCH_PARTS)],
            seq(merge, out_chunks[0]), *out_chunks[1:-1], seq(out_chunks[-1], norm)]


def _run_interleaved(project, attend, before=()):
    for i in range(max(len(project), len(attend), len(before))):
        if i < len(attend):
            attend[i][0]()
        if i < len(before):
            before[i]()
        if i < len(project):
            project[i]()
        if i < len(attend):
            attend[i][1]()


def _outproj_kernel(lru_ref, attn_ref, *refs):
    n = 2 * N_GATE_CHUNKS
    tile_refs = (refs[:N_GATE_CHUNKS], refs[N_GATE_CHUNKS:n], *refs[n:n + 5])
    _run_interleaved(_outproj_stages(lru_ref, attn_ref, *tile_refs, *refs[n + 5:]), [])


N_MIXER_INPUTS = 19


def _mixers_outproj_kernel(*refs):
    (sink_ref, q_ref, kc_ref, kp_ref, vc_ref, vp_ref, km_ref, vm_ref,
     xb_ref, yb_ref, tail_ref, h0_ref, *lru_params) = refs[:N_MIXER_INPUTS]
    n = 2 * N_GATE_CHUNKS
    rest = refs[N_MIXER_INPUTS:]
    tile_refs = (rest[:N_GATE_CHUNKS], rest[N_GATE_CHUNKS:n], *rest[n:n + 5])
    h_ref, hn_ref, hlast_ref, bias_ref, attn_ref, lru_ref, ext_ref, a3_ref, b3_ref, h3_ref, hstate_ref = rest[n + 5:]
    rows = MIXER_TILE
    t = pl.program_id(0)
    last = pl.num_programs(0) - 1

    attend = lambda: _attend_stages(t, sink_ref, q_ref, kc_ref, kp_ref, vc_ref, vp_ref, km_ref, vm_ref,
                                    bias_ref, attn_ref)
    project = lambda: _outproj_stages(lru_ref, attn_ref, *tile_refs, h_ref, hn_ref)

    def coeffs(n_block):
        def stage():
            _lru_coeffs_block(n_block, rows, ext_ref, a3_ref, b3_ref, xb_ref, *lru_params)
            cols = slice(n_block * LRU_BLOCK, (n_block + 1) * LRU_BLOCK)
            ext_ref[0:SUBLANES, cols] = ext_ref[rows:rows + SUBLANES, cols]
        return stage

    def gate_out(blocks):
        return lambda: _lru_gate_out(rows, h3_ref, yb_ref, lru_ref, blocks)

    def seq(*stages):
        return lambda: [stage() for stage in stages]

    def scan():
        h = _lru_scan(a3_ref, b3_ref, h3_ref, hstate_ref[...], 0, rows // SUBLANES)
        hstate_ref[...] = h
        hlast_ref[...] = h

    half = N_LRU_BLOCKS // 2
    all_coeffs = [coeffs(n_block) for n_block in range(N_LRU_BLOCKS)]

    @pl.when(t == 0)
    def _():
        _build_attn_bias(bias_ref)
        ext_ref[0:SUBLANES, :] = tail_ref[...]
        hstate_ref[...] = h0_ref[...]
        _run_interleaved([], attend(), all_coeffs)
        scan()

    @pl.when((t > 0) & (t < last))
    def _():
        before = [seq(gate_out(range(half)), all_coeffs[0]), gate_out(range(half, N_LRU_BLOCKS))]
        for stage in all_coeffs[1:]:
            before += [stage, seq()]
        _run_interleaved(project(), attend(), before)
        scan()

    @pl.when(t == last)
    def _():
        gate_out(range(N_LRU_BLOCKS))()
        _run_interleaved(project(), [])


def _mixers_outproj(sinks, proj, k_meta, v_meta, tail, h0, lru_params, x, g2, wl, wa, wo):
    m = x.shape[0]
    tm = MIXER_TILE
    tiles = m // tm
    cur = lambda t: jnp.minimum(t, tiles - 1)
    prv = lambda t: jnp.maximum(t - 1, 0)
    const = lambda shape: pl.BlockSpec(shape, lambda t: (0,) * len(shape))
    resident = lambda shape: pl.BlockSpec(shape, lambda t: (0,) * len(shape), pipeline_mode=pl.Buffered(1))
    kv = lambda col: (pl.BlockSpec((tm, KV_WIDTH), lambda t: (cur(t), col // KV_WIDTH)),
                      pl.BlockSpec((BLOCK, KV_WIDTH),
                                   lambda t: (jnp.maximum(ATTN_BLOCKS_PER_STEP * cur(t) - 1, 0), col // KV_WIDTH)))
    kc, kp = kv(COL_K)
    vc, vp = kv(COL_V)
    prv_spec = lambda width, cb=0: pl.BlockSpec((tm, width), lambda t: (prv(t), cb))
    tile = _nbytes((tm, D_MODEL), F32)
    scan_shape = (tm // SUBLANES, N_LRU_BLOCKS * SUBLANES, LANES)
    windows = 9 * tile
    weights = _nbytes(wl.shape, BF16) + _nbytes(wa.shape, BF16) + _nbytes(wo.shape, BF16)
    scratch = weights + 6 * tile + 2 * _nbytes(ATTN_BIAS_SHAPE, F32)
    return pl.pallas_call(
        _mixers_outproj_kernel,
        grid=(tiles + 1,),
        in_specs=[
            pl.BlockSpec(memory_space=pltpu.SMEM),
            pl.BlockSpec((tm, ATTN_WIDTH), lambda t: (cur(t), COL_Q // ATTN_WIDTH)),
            kc, kp, vc, vp,
            const((N_META, KV_WIDTH)),
            const((N_META, KV_WIDTH)),
            pl.BlockSpec((tm, LRU_WIDTH), lambda t: (cur(t), COL_XB // LRU_WIDTH)),
            prv_spec(LRU_WIDTH, COL_YB // LRU_WIDTH),
            const((SUBLANES, LRU_WIDTH)),
            const((SUBLANES, LANES)),
            *[const(p.shape) for p in lru_params],
            *[prv_spec(GATE_CHUNK, col // GATE_CHUNK + c) for col in (COL_GL, COL_GA) for c in range(N_GATE_CHUNKS)],
            prv_spec(D_MODEL),
            const((1, D_MODEL)),
            resident((LRU_WIDTH, D_MODEL)),
            resident((ATTN_WIDTH, D_MODEL)),
            resident((D_MODEL, D_MODEL)),
        ],
        out_specs=[prv_spec(D_MODEL), prv_spec(D_MODEL), const((SUBLANES, LANES))],
        out_shape=[jax.ShapeDtypeStruct((m, D_MODEL), F32), jax.ShapeDtypeStruct((m, D_MODEL), BF16),
                   jax.ShapeDtypeStruct((SUBLANES, LANES), F32)],
        scratch_shapes=[
            pltpu.VMEM(ATTN_BIAS_SHAPE, F32),
            pltpu.VMEM((tm, ATTN_WIDTH), BF16),
            pltpu.VMEM((tm, LRU_WIDTH), BF16),
            pltpu.VMEM((tm + SUBLANES, LRU_WIDTH), F32),
            pltpu.VMEM(scan_shape, F32),
            pltpu.VMEM(scan_shape, F32),
            pltpu.VMEM(scan_shape, F32),
            pltpu.VMEM((SUBLANES, LANES), F32),
        ],
        compiler_params=_cparams(("arbitrary",), windows, scratch),
        name="mixers_outproj",
    )(sinks, proj, proj, proj, proj, proj, k_meta, v_meta, proj, proj, tail, h0, *lru_params,
      *([proj] * (2 * N_GATE_CHUNKS)), x, g2, wl, wa, wo)


def _outproj(lru, attn, proj, x, g2, wl, wa, wo):
    m = x.shape[0]
    tm = _tile_plan(m)["outproj"]
    const = lambda shape: pl.BlockSpec(shape, lambda i: (0,) * len(shape))
    tile = _nbytes((tm, D_MODEL), F32)
    windows = 6 * tile + _nbytes(wl.shape, BF16) + _nbytes(wa.shape, BF16) + _nbytes(wo.shape, BF16)
    row_spec = lambda width: pl.BlockSpec((tm, width), lambda i: (i, 0))
    return pl.pallas_call(
        _outproj_kernel,
        grid=(m // tm,),
        in_specs=[
            row_spec(LRU_WIDTH),
            row_spec(ATTN_WIDTH),
            *[pl.BlockSpec((tm, GATE_CHUNK), functools.partial(lambda i, cb: (i, cb), cb=col // GATE_CHUNK + c))
              for col in (COL_GL, COL_GA) for c in range(N_GATE_CHUNKS)],
            row_spec(D_MODEL),
            const((1, D_MODEL)),
            const((LRU_WIDTH, D_MODEL)),
            const((ATTN_WIDTH, D_MODEL)),
            const((D_MODEL, D_MODEL)),
        ],
        out_specs=[row_spec(D_MODEL), row_spec(D_MODEL)],
        out_shape=[jax.ShapeDtypeStruct((m, D_MODEL), F32), jax.ShapeDtypeStruct((m, D_MODEL), BF16)],
        compiler_params=_cparams(("arbitrary",), windows, 3 * tile),
        name=f"outproj_{m}",
    )(lru, attn, *([proj] * (2 * N_GATE_CHUNKS)), x, g2, wl, wa, wo)


MLP_EPILOGUE_ROWS = 256


def _mlp_kernel(h_ref, hn_ref, gf_ref, wu_ref, wd_ref, o_ref):
    f = pl.program_id(1)
    last = pl.num_programs(1) - 1

    def chunk(rows=slice(None)):
        u = jnp.dot(hn_ref[rows, :], wu_ref[...], preferred_element_type=F32)
        u = jnp.square(jnp.maximum(u, 0.0))
        return jnp.dot(u.astype(BF16), wd_ref[...], preferred_element_type=F32)

    @pl.when(f == 0)
    def _():
        o_ref[...] = chunk()

    @pl.when((f > 0) & (f < last))
    def _():
        o_ref[...] += chunk()

    @pl.when(f == last)
    def _():
        step = min(MLP_EPILOGUE_ROWS, o_ref.shape[0])
        for r0 in range(0, o_ref.shape[0], step):
            rows = slice(r0, r0 + step)
            out = h_ref[rows, :] + (o_ref[rows, :] + chunk(rows))
            ms = jnp.mean(out * out, axis=-1, keepdims=True)
            o_ref[rows, :] = out * lax.rsqrt(ms + EPS) * gf_ref[...]


def _mlp(h, hn, gf, wu, wd):
    m = h.shape[0]
    tm, tf = _tile_plan(m)["mlp"]
    windows = (2 * _nbytes((tm, D_MODEL), F32) + _nbytes((tm, D_MODEL), BF16) + 2 * _nbytes((D_MODEL, tf), BF16))
    n_f = D_FF // tf
    return pl.pallas_call(
        _mlp_kernel,
        grid=(m // tm, n_f),
        in_specs=[
            pl.BlockSpec((tm, D_MODEL), lambda i, f: (jnp.where(f == n_f - 1, i, jnp.maximum(i - 1, 0)), 0)),
            pl.BlockSpec((tm, D_MODEL), lambda i, f: (i, 0)),
            pl.BlockSpec((1, D_MODEL), lambda i, f: (0, 0)),
            pl.BlockSpec((D_MODEL, tf), lambda i, f: (0, f)),
            pl.BlockSpec((tf, D_MODEL), lambda i, f: (f, 0)),
        ],
        out_specs=pl.BlockSpec((tm, D_MODEL), lambda i, f: (i, 0)),
        out_shape=jax.ShapeDtypeStruct((m, D_MODEL), F32),
        compiler_params=_cparams(("arbitrary", "arbitrary"), windows, 2 * _nbytes((tm, tf), F32)),
        name=f"mlp_{m}",
    )(h, hn, gf, wu, wd)


def _sample_bias():
    dist = (WINDOW - np.arange(WINDOW)).astype(np.float32)
    slopes = np.asarray(_slopes(), np.float32).reshape(N_KV_HEADS, GROUP, 1)
    return jnp.asarray(-(slopes * dist[None, None, :]))


def kernel(x_prompt, x_sample, cache_meta_k, cache_meta_v, cache_win_k, cache_win_v, state_conv, state_h,
           meta_tokens, norm1_g, w_in, conv_w, conv_b, w_gate_x, b_gate_x, w_gate_a, b_gate_a, lru_a_param,
           attn_sinks, w_lru_out, w_attn_out, w_o, norm2_g, w_mlp_up, w_mlp_down, final_norm_g):
    row = lambda v: v.reshape(1, -1)
    g1, g2, gf = row(norm1_g[0]), row(norm2_g[0]), row(final_norm_g)
    wgx, wga = w_gate_x[0].astype(BF16), w_gate_a[0].astype(BF16)
    cw, cb = conv_w[0], row(conv_b[0])
    bgx, bga, ap = row(b_gate_x[0]), row(b_gate_a[0]), row(lru_a_param[0])
    sinks = attn_sinks[0]

    x_main = x_prompt.reshape(SEQ, D_MODEL)
    x_extra = jnp.concatenate([
        x_sample.reshape(DEC_BATCH, D_MODEL),
        jnp.zeros((EXTRA_ROWS - DEC_BATCH - N_META, D_MODEL), F32),
        meta_tokens.astype(F32)], axis=0)

    proj_e, w_in_b = _inproj(x_extra, g1, w_in[0])
    proj_m, wl, wa, wo, wu, wd = _inproj(
        x_main, g1, w_in_b, (w_lru_out[0], w_attn_out[0], w_o[0], w_mlp_up[0], w_mlp_down[0]))

    lru_e, h_sample, h_meta = _lru_extra(proj_e, state_conv[0].reshape(DEC_BATCH, (CONV_W - 1) * LRU_WIDTH),
                                         state_h[0], cw, cb, wgx, bgx, wga, bga, ap)
    tail = proj_e[EXTRA_ROWS - SUBLANES:, COL_XB:COL_XB + LRU_WIDTH]

    k_meta = proj_e[META_ROW0:, COL_K:COL_K + KV_WIDTH]
    v_meta = proj_e[META_ROW0:, COL_V:COL_V + KV_WIDTH]
    attn_meta = _attn_meta(sinks, proj_e[META_ROW0:, COL_Q:COL_Q + ATTN_WIDTH], k_meta, v_meta)
    k_new = proj_e[:DEC_BATCH, None, COL_K:COL_K + KV_WIDTH]
    v_new = proj_e[:DEC_BATCH, None, COL_V:COL_V + KV_WIDTH]
    q4 = proj_e[:DEC_BATCH, COL_Q:COL_Q + ATTN_WIDTH].reshape(DEC_BATCH, N_KV_HEADS, GROUP, HEAD_DIM)
    flat = lambda c, n: c[0].reshape(DEC_BATCH, n, KV_WIDTH)
    attn_s, kw_out, vw_out = _attn_sample(
        q4, flat(cache_meta_k, N_META), flat(cache_win_k, WINDOW), k_new,
        flat(cache_meta_v, N_META), flat(cache_win_v, WINDOW), v_new,
        _sample_bias(), sinks.reshape(N_KV_HEADS, GROUP, 1), bb=8)
    attn_e = jnp.concatenate([
        attn_s.reshape(DEC_BATCH, ATTN_WIDTH),
        jnp.zeros((EXTRA_ROWS - DEC_BATCH - N_META, ATTN_WIDTH), F32),
        attn_meta], axis=0).astype(BF16)

    res_m, resn_m, h_last = _mixers_outproj(
        sinks, proj_m, k_meta, v_meta, tail, h_meta, (cw, cb, wgx, bgx, wga, bga, ap), x_main, g2, wl, wa, wo)
    res_e, resn_e = _outproj(lru_e, attn_e, proj_e, x_extra, g2, wl, wa, wo)
    y_m = _mlp(res_m, resn_m, gf, wu, wd)
    y_e = _mlp(res_e, resn_e, gf, wu, wd)

    kv5 = lambda a, n: a.reshape(1, -1, n, N_KV_HEADS, HEAD_DIM)
    return (
        y_m.reshape(1, SEQ, D_MODEL),
        y_e[:DEC_BATCH].reshape(DEC_BATCH, 1, D_MODEL),
        kv5(k_meta, N_META), kv5(v_meta, N_META),
        kv5(proj_m[SEQ - WINDOW:, COL_K:COL_K + KV_WIDTH], WINDOW),
        kv5(proj_m[SEQ - WINDOW:, COL_V:COL_V + KV_WIDTH], WINDOW),
        proj_m[SEQ - (CONV_W - 1):, COL_XB:COL_XB + LRU_WIDTH].reshape(1, 1, CONV_W - 1, LRU_WIDTH),
        h_last.reshape(1, 1, LRU_WIDTH),
        kv5(kw_out, WINDOW), kv5(vw_out, WINDOW),
        jnp.concatenate([state_conv[0][:, 1:], proj_e[:DEC_BATCH, None, COL_XB:COL_XB + LRU_WIDTH]], axis=1)[None],
        h_sample[None],
    )
```

```python
import functools

import numpy as np
import jax
import jax.numpy as jnp
from jax import lax
from jax.experimental import pallas as pl
from jax.experimental.pallas import tpu as pltpu

D_MODEL = 2048
SEQ = 16384
DEC_BATCH = 128
N_META = 16
LRU_WIDTH = 1024
N_LRU_BLOCKS = 8
LRU_BLOCK = LRU_WIDTH // N_LRU_BLOCKS
CONV_W = 4
LRU_C = 8.0
N_HEADS = 16
N_KV_HEADS = 4
HEAD_DIM = 64
GROUP = N_HEADS // N_KV_HEADS
ATTN_WIDTH = N_HEADS * HEAD_DIM
KV_WIDTH = N_KV_HEADS * HEAD_DIM
WINDOW = 128
BLOCK = 128
D_FF = 4 * D_MODEL
EPS = 1e-6
NEG_INF = -1e30
IN_WIDTH = 2 * LRU_WIDTH + ATTN_WIDTH + 2 * KV_WIDTH + 2 * D_MODEL
COL_XB, COL_YB, COL_Q = 0, LRU_WIDTH, 2 * LRU_WIDTH
COL_K = COL_Q + ATTN_WIDTH
COL_V = COL_K + KV_WIDTH
COL_GL = COL_V + KV_WIDTH
COL_GA = COL_GL + D_MODEL
GATE_CHUNK = 512

EXTRA_ROWS = 256
META_ROW0 = EXTRA_ROWS - N_META
SUBLANES = 8
LANES = 128
MIB = 1024 * 1024

F32 = jnp.float32
BF16 = jnp.bfloat16


def _slopes():
    return [2.0 ** (-8.0 * (h + 1) / N_HEADS) for h in range(N_HEADS)]


V7X_VMEM_REQUEST_CAP = 60 * MIB


def _nbytes(shape, dtype):
    return int(np.prod(shape)) * jnp.dtype(dtype).itemsize


def _cparams(sem, pipelined, resident=0):
    estimate = 2 * pipelined + resident
    limit = min(V7X_VMEM_REQUEST_CAP, estimate + estimate // 4 + 2 * MIB)
    return pltpu.CompilerParams(dimension_semantics=sem, vmem_limit_bytes=limit)


def _tile_plan(rows):
    if rows == EXTRA_ROWS:
        return dict(inproj=(EXTRA_ROWS, 1536), outproj=EXTRA_ROWS, mlp=(EXTRA_ROWS, 2048))
    return dict(inproj=(1024, 1536), lru=512, outproj=256, mlp=(1024, 512))


CAST_STEPS = 64


def _cast_specs(weights, step_of):
    specs, shapes, nbytes = [], [], 0
    for w in weights:
        block = (w.shape[0] // CAST_STEPS, w.shape[1])
        specs.append(pl.BlockSpec(block, lambda *idx: (jnp.minimum(step_of(*idx), CAST_STEPS - 1), 0)))
        shapes.append(jax.ShapeDtypeStruct(w.shape, BF16))
        nbytes += _nbytes(block, F32) + _nbytes(block, BF16)
    return specs, shapes, nbytes


def _cast_chunks(src_refs, dst_refs):
    for src_ref, dst_ref in zip(src_refs, dst_refs):
        dst_ref[...] = src_ref[...].astype(BF16)


def _inproj_kernel(x_ref, g_ref, w_ref, *refs, n_cast, emit_w):
    cast_src, o_ref, cast_dst, xs_ref = refs[:n_cast], refs[n_cast], refs[n_cast + 1:2 * n_cast + 1], refs[-1]

    @pl.when(pl.program_id(1) == 0)
    def _():
        x = x_ref[...]
        ms = jnp.mean(x * x, axis=-1, keepdims=True)
        xs_ref[...] = (x * lax.rsqrt(ms + EPS) * g_ref[...]).astype(BF16)

    w = w_ref[...]
    if emit_w:
        w = w.astype(BF16)
        refs[-2][...] = w
    o_ref[...] = jnp.dot(xs_ref[...], w, preferred_element_type=F32)
    _cast_chunks(cast_src, cast_dst)


def _inproj(x, g, w, cast_weights=()):
    m = x.shape[0]
    tm, tn = _tile_plan(m)["inproj"]
    n_col = IN_WIDTH // tn
    emit_w = w.dtype == F32
    assert not emit_w or m == tm, "the bf16 copy of w is written once per column block"
    cast_specs, cast_shapes, cast_bytes = _cast_specs(cast_weights, lambda i, j: i * n_col + j)
    assert not cast_weights or (m // tm) * n_col >= CAST_STEPS
    w_spec = pl.BlockSpec((D_MODEL, tn), lambda i, j: (0, j))
    windows = (_nbytes((tm, D_MODEL), F32) + _nbytes((D_MODEL, tn), w.dtype) + _nbytes((tm, tn), F32) + cast_bytes
               + emit_w * _nbytes((D_MODEL, tn), BF16))
    return pl.pallas_call(
        functools.partial(_inproj_kernel, n_cast=len(cast_weights), emit_w=emit_w),
        grid=(m // tm, n_col),
        in_specs=[
            pl.BlockSpec((tm, D_MODEL), lambda i, j: (i, 0)),
            pl.BlockSpec((1, D_MODEL), lambda i, j: (0, 0)),
            w_spec,
            *cast_specs,
        ],
        out_specs=[pl.BlockSpec((tm, tn), lambda i, j: (i, j)), *cast_specs, *([w_spec] if emit_w else [])],
        out_shape=[jax.ShapeDtypeStruct((m, IN_WIDTH), F32), *cast_shapes,
                   *([jax.ShapeDtypeStruct(w.shape, BF16)] if emit_w else [])],
        scratch_shapes=[pltpu.VMEM((tm, D_MODEL), BF16)],
        compiler_params=_cparams(("arbitrary", "arbitrary"), windows, _nbytes((tm, D_MODEL), BF16)),
        name=f"inproj_{m}",
    )(x, g, w, *cast_weights)


def _gate_ab(xc_n, n, wgx_ref, bgx_ref, wga_ref, bga_ref, ap_ref):
    cols = slice(n * LRU_BLOCK, (n + 1) * LRU_BLOCK)
    xcb = xc_n.astype(BF16)
    gx = jnp.dot(xcb, wgx_ref[n], preferred_element_type=F32) + bgx_ref[:, cols]
    ga = jnp.dot(xcb, wga_ref[n], preferred_element_type=F32) + bga_ref[:, cols]
    gate_x = jax.nn.sigmoid(gx)
    gate_a = jax.nn.sigmoid(ga)
    log_a = -LRU_C * gate_a * jax.nn.softplus(-ap_ref[:, cols])
    a = jnp.exp(log_a)
    z = -jnp.tanh(log_a) * (a * a + 1.0)
    root = jnp.where(z > 0.0, z * lax.rsqrt(z), 0.0)
    b = root * gate_x * xc_n
    return a, b


def _lru_coeffs(rows, ext_ref, a3_ref, b3_ref, xb, cw_ref, cb_ref, wgx_ref, bgx_ref, wga_ref, bga_ref, ap_ref):
    for n in range(N_LRU_BLOCKS):
        _lru_coeffs_block(n, rows, ext_ref, a3_ref, b3_ref, xb, cw_ref, cb_ref, wgx_ref, bgx_ref, wga_ref, bga_ref,
                          ap_ref)


def _lru_coeffs_block(n, rows, ext_ref, a3_ref, b3_ref, xb, cw_ref, cb_ref, wgx_ref, bgx_ref, wga_ref, bga_ref,
                      ap_ref):
    groups = rows // SUBLANES
    cols = slice(n * LRU_BLOCK, (n + 1) * LRU_BLOCK)
    ext_ref[SUBLANES:SUBLANES + rows, cols] = xb[:, cols]
    x_ext = ext_ref[:, cols]
    xc = x_ext * cw_ref[0:1, cols]
    for t in range(1, CONV_W):
        xc = pltpu.roll(xc, 1, axis=0) + x_ext * cw_ref[t:t + 1, cols]
    xc = xc[SUBLANES:, :] + cb_ref[:, cols]
    a, b = _gate_ab(xc, n, wgx_ref, bgx_ref, wga_ref, bga_ref, ap_ref)
    a3_ref[:, n * SUBLANES:(n + 1) * SUBLANES, :] = a.reshape(groups, SUBLANES, LANES)
    b3_ref[:, n * SUBLANES:(n + 1) * SUBLANES, :] = b.reshape(groups, SUBLANES, LANES)


SCAN_UNROLL = 4


def _lru_scan(a3_ref, b3_ref, h3_ref, h_init, g_lo, g_hi):
    def group_step(i, h):
        for r in range(0, SUBLANES, 2):
            s0, s1 = (pl.ds(r + k, N_LRU_BLOCKS, stride=SUBLANES) for k in range(2))
            a0, b0, a1, b1 = a3_ref[i, s0, :], b3_ref[i, s0, :], a3_ref[i, s1, :], b3_ref[i, s1, :]
            h3_ref[i, s0, :] = a0 * h + b0
            h = (a1 * a0) * h + (a1 * b0 + b1)
            h3_ref[i, s1, :] = h
        return h

    return lax.fori_loop(g_lo, g_hi, group_step, h_init, unroll=SCAN_UNROLL)


def _lru_gate_out(rows, h3_ref, yb_ref, lru_ref, blocks=range(N_LRU_BLOCKS)):
    for n in blocks:
        cols = slice(n * LRU_BLOCK, (n + 1) * LRU_BLOCK)
        hs = h3_ref[:, n * SUBLANES:(n + 1) * SUBLANES, :].reshape(rows, LANES)
        lru_ref[:, cols] = (jax.nn.gelu(yb_ref[:, cols]) * hs).astype(BF16)


def _lru_rows(rows, ext_ref, a3_ref, b3_ref, h3_ref, xb, yb_ref, cw_ref, cb_ref,
              wgx_ref, bgx_ref, wga_ref, bga_ref, ap_ref, lru_ref, h_init, g_lo, g_hi):
    _lru_coeffs(rows, ext_ref, a3_ref, b3_ref, xb, cw_ref, cb_ref, wgx_ref, bgx_ref, wga_ref, bga_ref, ap_ref)
    h = _lru_scan(a3_ref, b3_ref, h3_ref, h_init, g_lo, g_hi)
    _lru_gate_out(rows, h3_ref, yb_ref, lru_ref)
    return h


def _lru_extra_kernel(xb_ref, yb_ref, sconv_ref, sh_ref, cw_ref, cb_ref, wgx_ref, bgx_ref, wga_ref, bga_ref,
                      ap_ref, lru_ref, hs_ref, hmeta_ref, ext_ref, a3_ref, b3_ref, h3_ref):
    nb = DEC_BATCH
    xb = xb_ref[0:nb, :]
    xc = sconv_ref[:, 0:LRU_WIDTH] * cw_ref[0:1, :]
    xc = xc + sconv_ref[:, LRU_WIDTH:2 * LRU_WIDTH] * cw_ref[1:2, :]
    xc = xc + sconv_ref[:, 2 * LRU_WIDTH:3 * LRU_WIDTH] * cw_ref[2:3, :]
    xc = xc + xb * cw_ref[3:4, :]
    xc = xc + cb_ref[...]
    for n in range(N_LRU_BLOCKS):
        cols = slice(n * LRU_BLOCK, (n + 1) * LRU_BLOCK)
        a, b = _gate_ab(xc[:, cols], n, wgx_ref, bgx_ref, wga_ref, bga_ref, ap_ref)
        h = a * sh_ref[:, cols] + b
        hs_ref[:, cols] = h
        lru_ref[0:nb, cols] = (jax.nn.gelu(yb_ref[0:nb, cols]) * h).astype(BF16)
    rows = EXTRA_ROWS - nb
    ext_ref[0:SUBLANES, :] = jnp.zeros((SUBLANES, LRU_WIDTH), F32)
    h3_ref[...] = jnp.zeros(h3_ref.shape, F32)
    first_group = (rows - N_META) // SUBLANES
    hmeta_ref[...] = _lru_rows(rows, ext_ref, a3_ref, b3_ref, h3_ref, xb_ref[nb:, :], yb_ref.at[nb:, :], cw_ref,
                               cb_ref, wgx_ref, bgx_ref, wga_ref, bga_ref, ap_ref, lru_ref.at[nb:, :],
                               jnp.zeros((SUBLANES, LANES), F32), first_group, rows // SUBLANES)


def _lru_extra(proj_e, sconv, sh, cw, cb, wgx, bgx, wga, bga, ap):
    const = lambda shape: pl.BlockSpec(shape, lambda i: (0,) * len(shape))
    rows = EXTRA_ROWS - DEC_BATCH
    scan_shape = (rows // SUBLANES, N_LRU_BLOCKS * SUBLANES, LANES)
    return pl.pallas_call(
        _lru_extra_kernel,
        grid=(1,),
        in_specs=[
            pl.BlockSpec((EXTRA_ROWS, LRU_WIDTH), lambda i: (0, COL_XB // LRU_WIDTH)),
            pl.BlockSpec((EXTRA_ROWS, LRU_WIDTH), lambda i: (0, COL_YB // LRU_WIDTH)),
            const((DEC_BATCH, (CONV_W - 1) * LRU_WIDTH)),
            const((DEC_BATCH, LRU_WIDTH)),
            const((CONV_W, LRU_WIDTH)),
            const((1, LRU_WIDTH)),
            const((N_LRU_BLOCKS, LRU_BLOCK, LRU_BLOCK)),
            const((1, LRU_WIDTH)),
            const((N_LRU_BLOCKS, LRU_BLOCK, LRU_BLOCK)),
            const((1, LRU_WIDTH)),
            const((1, LRU_WIDTH)),
        ],
        out_specs=[
            const((EXTRA_ROWS, LRU_WIDTH)),
            const((DEC_BATCH, LRU_WIDTH)),
            const((SUBLANES, LANES)),
        ],
        out_shape=[
            jax.ShapeDtypeStruct((EXTRA_ROWS, LRU_WIDTH), BF16),
            jax.ShapeDtypeStruct((DEC_BATCH, LRU_WIDTH), F32),
            jax.ShapeDtypeStruct((SUBLANES, LANES), F32),
        ],
        scratch_shapes=[
            pltpu.VMEM((rows + SUBLANES, LRU_WIDTH), F32),
            pltpu.VMEM(scan_shape, F32),
            pltpu.VMEM(scan_shape, F32),
            pltpu.VMEM(scan_shape, F32),
        ],
        compiler_params=_cparams(("arbitrary",), 8 * _nbytes((DEC_BATCH, LRU_WIDTH), F32),
                                 8 * _nbytes((rows, LRU_WIDTH), F32)),
        name="lru_extra",
    )(proj_e, proj_e, sconv, sh, cw, cb, wgx, bgx, wga, bga, ap)


def _attend_kv_head(q_h, kk, vv, bias, sinks):
    return _attend_values(*_attend_weights(q_h, kk, bias, sinks), vv)


def _attend_weights(q_h, kk, bias, sinks):
    r = q_h.shape[0]
    qs = jnp.concatenate([q_h[:, g * HEAD_DIM:(g + 1) * HEAD_DIM] for g in range(GROUP)], axis=0)
    qs = (qs * (HEAD_DIM ** -0.5)).astype(BF16)
    s = lax.dot_general(kk.astype(BF16), qs, (((1,), (1,)), ((), ())), preferred_element_type=F32) + bias
    sink = jnp.concatenate([jnp.full((1, r), sinks[g], F32) for g in range(GROUP)], axis=1)
    m = jnp.maximum(jnp.max(s, axis=0, keepdims=True), sink)
    e = jnp.exp(s - m)
    denom = jnp.sum(e, axis=0, keepdims=True) + jnp.exp(sink - m)
    return e.astype(BF16), denom


def _attend_values(e, denom, vv):
    r = e.shape[1] // GROUP
    o_t = jnp.dot(vv.T.astype(BF16), e, preferred_element_type=F32) / denom
    pairs = [jnp.concatenate([o_t[:, g * r:(g + 1) * r], o_t[:, (g + 1) * r:(g + 2) * r]], axis=0).T
             for g in range(0, GROUP, 2)]
    return jnp.concatenate(pairs, axis=-1)


ATTN_BLOCKS_PER_STEP = 2
MIXER_TILE = ATTN_BLOCKS_PER_STEP * BLOCK
ATTN_QUERIES = BLOCK // 2
ATTN_BAND = WINDOW + ATTN_QUERIES
ATTN_KEYS = N_META + ATTN_BAND
ATTN_TABLES = WINDOW // ATTN_QUERIES + 1


ATTN_BIAS_SHAPE = (ATTN_TABLES, N_KV_HEADS, ATTN_KEYS, GROUP * ATTN_QUERIES)


def _build_attn_bias(bias_ref):
    slopes = _slopes()
    key = lax.broadcasted_iota(jnp.int32, (ATTN_KEYS, ATTN_QUERIES), 0)
    qry = lax.broadcasted_iota(jnp.int32, (ATTN_KEYS, ATTN_QUERIES), 1)
    d = qry + WINDOW - (key - N_META)
    dist = jnp.where(key < N_META, 0, d).astype(F32)
    for j in range(ATTN_TABLES):
        band_ok = (d >= 0) & (d <= WINDOW) & (key >= N_META + WINDOW - j * ATTN_QUERIES)
        valid = (key < N_META) | band_ok
        for hd in range(N_HEADS):
            lanes = slice((hd % GROUP) * ATTN_QUERIES, (hd % GROUP + 1) * ATTN_QUERIES)
            bias_ref[j, hd // GROUP, :, lanes] = jnp.where(valid, -(slopes[hd] * dist), NEG_INF)


def _attend_stages(tile, sink_ref, q_ref, kc_ref, kp_ref, vc_ref, vp_ref, km_ref, vm_ref, bias_ref, o_ref):
    def pair(sub, h):
        rows = slice(sub * ATTN_QUERIES, (sub + 1) * ATTN_QUERIES)
        lo = sub * ATTN_QUERIES - WINDOW
        n_before = max(-lo, 0) // ATTN_QUERIES
        table = jnp.where(tile == 0, ATTN_TABLES - 1 - n_before, ATTN_TABLES - 1) if n_before else ATTN_TABLES - 1
        hc = slice(h * HEAD_DIM, (h + 1) * HEAD_DIM)
        qc = slice(h * GROUP * HEAD_DIM, (h + 1) * GROUP * HEAD_DIM)
        v = {}

        def keys(meta_ref, cur_ref, prev_ref):
            if lo < 0:
                band = [prev_ref[WINDOW + lo:WINDOW, hc], cur_ref[0:lo + ATTN_BAND, hc]]
            else:
                band = [cur_ref[lo:lo + ATTN_BAND, hc]]
            return jnp.concatenate([meta_ref[:, hc], *band], axis=0)

        def weights():
            v["w"] = _attend_weights(q_ref[rows, qc], keys(km_ref, kc_ref, kp_ref), bias_ref[table, h],
                                     [sink_ref[h * GROUP + g] for g in range(GROUP)])

        def values():
            o_ref[rows, qc] = _attend_values(*v["w"], keys(vm_ref, vc_ref, vp_ref)).astype(BF16)

        return weights, values

    return [pair(sub, h) for sub in range(MIXER_TILE // ATTN_QUERIES) for h in range(N_KV_HEADS)]


def _attn_meta_kernel(sink_ref, q_ref, km_ref, vm_ref, o_ref):
    key = lax.broadcasted_iota(jnp.int32, (N_META, GROUP * N_META), 0)
    qry = lax.broadcasted_iota(jnp.int32, (N_META, GROUP * N_META), 1) % N_META
    bias = jnp.where(qry >= key, 0.0, NEG_INF)
    for h in range(N_KV_HEADS):
        hc = slice(h * HEAD_DIM, (h + 1) * HEAD_DIM)
        q_h = q_ref[:, h * GROUP * HEAD_DIM:(h + 1) * GROUP * HEAD_DIM]
        o = _attend_kv_head(q_h, km_ref[:, hc], vm_ref[:, hc], bias,
                            [sink_ref[h * GROUP + g] for g in range(GROUP)])
        o_ref[:, h * GROUP * HEAD_DIM:(h + 1) * GROUP * HEAD_DIM] = o


def _attn_meta(sinks, q_meta, k_meta, v_meta):
    const = lambda shape: pl.BlockSpec(shape, lambda i: (0,) * len(shape))
    return pl.pallas_call(
        _attn_meta_kernel,
        grid=(1,),
        in_specs=[pl.BlockSpec(memory_space=pltpu.SMEM), const((N_META, ATTN_WIDTH)),
                  const((N_META, KV_WIDTH)), const((N_META, KV_WIDTH))],
        out_specs=const((N_META, ATTN_WIDTH)),
        out_shape=jax.ShapeDtypeStruct((N_META, ATTN_WIDTH), F32),
        compiler_params=_cparams(("arbitrary",), 4 * _nbytes((N_META, ATTN_WIDTH), F32)),
        name="attn_meta",
    )(sinks, q_meta, k_meta, v_meta)


def _attn_sample_kernel(q_ref, km_ref, kw_ref, kn_ref, knt_ref, vm_ref, vw_ref, vn_ref, vnt_ref, bias_ref, sink_ref,
                        o_ref, kwo_ref, vwo_ref):
    bb = q_ref.shape[0]
    position = lax.broadcasted_iota(jnp.int32, (HEAD_DIM, WINDOW), 1)
    for src_ref, newt_ref, dst_ref in ((kw_ref, knt_ref, kwo_ref), (vw_ref, vnt_ref, vwo_ref)):
        for h in range(N_KV_HEADS):
            new_t = newt_ref[0, h * HEAD_DIM:(h + 1) * HEAD_DIM, :]
            for j in range(bb):
                dst_ref[j, h] = jnp.where(position == WINDOW - 1, new_t[:, j:j + 1],
                                          pltpu.roll(src_ref[j, h], WINDOW - 1, axis=1))
    dot_qk = functools.partial(jnp.einsum, "bgd,bjd->bgj", preferred_element_type=F32)
    dot_qkt = functools.partial(jnp.einsum, "bgd,bdj->bgj", preferred_element_type=F32)
    dot_pv = functools.partial(jnp.einsum, "bgj,bjd->bgd", preferred_element_type=F32)
    dot_pvt = functools.partial(jnp.einsum, "bgj,bdj->bgd", preferred_element_type=F32)
    for h in range(N_KV_HEADS):
        hc = slice(h * HEAD_DIM, (h + 1) * HEAD_DIM)
        q = (q_ref[:, h] * (HEAD_DIM ** -0.5)).astype(BF16)
        k_new = kn_ref[:, :, hc].astype(BF16).astype(F32)
        v_new = vn_ref[:, :, hc].astype(BF16).astype(F32)
        s_m = dot_qk(q, km_ref[:, :, hc].astype(BF16))
        s_w = dot_qkt(q, kw_ref[:, h].astype(BF16)) + bias_ref[h]
        s_n = jnp.sum(q.astype(F32) * k_new, axis=-1, keepdims=True)
        sink = sink_ref[h]
        m = jnp.maximum(jnp.maximum(jnp.max(s_m, axis=-1, keepdims=True), jnp.max(s_w, axis=-1, keepdims=True)),
                        jnp.maximum(s_n, sink))
        e_m, e_w, e_n = jnp.exp(s_m - m), jnp.exp(s_w - m), jnp.exp(s_n - m)
        denom = (jnp.sum(e_m, axis=-1, keepdims=True) + jnp.sum(e_w, axis=-1, keepdims=True) + e_n
                 + jnp.exp(sink - m))
        o = (dot_pv(e_m.astype(BF16), vm_ref[:, :, hc].astype(BF16))
             + dot_pvt(e_w.astype(BF16), vw_ref[:, h].astype(BF16))
             + e_n.astype(BF16).astype(F32) * v_new)
        o_ref[:, h] = o / denom


def _attn_sample(q4, k_meta, k_win, k_new, v_meta, v_win, v_new, bias, sinks4, bb):
    b3 = lambda rows: pl.BlockSpec((bb, rows, KV_WIDTH), lambda i: (i, 0, 0))
    q_spec = pl.BlockSpec((bb, N_KV_HEADS, GROUP, HEAD_DIM), lambda i: (i, 0, 0, 0))
    win_spec = pl.BlockSpec((bb, N_KV_HEADS, HEAD_DIM, WINDOW), lambda i: (i, 0, 0, 0))
    new_t_spec = pl.BlockSpec((1, KV_WIDTH, bb), lambda i: (i, 0, 0))
    new_t = lambda new: jnp.transpose(new.reshape(DEC_BATCH // bb, bb, KV_WIDTH), (0, 2, 1))
    win_shape = jax.ShapeDtypeStruct((DEC_BATCH, N_KV_HEADS, HEAD_DIM, WINDOW), F32)
    return pl.pallas_call(
        _attn_sample_kernel,
        grid=(DEC_BATCH // bb,),
        in_specs=[
            q_spec,
            b3(N_META), win_spec, b3(1), new_t_spec,
            b3(N_META), win_spec, b3(1), new_t_spec,
            pl.BlockSpec((N_KV_HEADS, GROUP, WINDOW), lambda i: (0, 0, 0)),
            pl.BlockSpec((N_KV_HEADS, GROUP, 1), lambda i: (0, 0, 0)),
        ],
        out_specs=[q_spec, win_spec, win_spec],
        out_shape=[jax.ShapeDtypeStruct((DEC_BATCH, N_KV_HEADS, GROUP, HEAD_DIM), F32), win_shape, win_shape],
        compiler_params=_cparams(("arbitrary",), 5 * _nbytes((bb, WINDOW, KV_WIDTH), F32),
                                 4 * _nbytes((bb, WINDOW, KV_WIDTH), F32)),
        name="attn_sample",
    )(q4, k_meta, k_win, k_new[:, None, :], new_t(k_new), v_meta, v_win, v_new[:, None, :], new_t(v_new), bias, sinks4)


N_GATE_CHUNKS = D_MODEL // GATE_CHUNK
OUTPROJ_BRANCH_PARTS = 4
OUTPROJ_CHUNK = D_MODEL // (2 * OUTPROJ_BRANCH_PARTS)


def _outproj_stages(lru_ref, attn_ref, gl_refs, ga_refs, x_ref, g2_ref, wl_ref, wa_ref, wo_ref, h_ref, hn_ref):
    v = {"a": [], "l": [], "ss": 0.0}
    part_cols = D_MODEL // OUTPROJ_BRANCH_PARTS

    def branch_proj(key, src_ref, w_ref, part):
        if part == 0:
            v[key + "_in"] = src_ref[...]
        cols = slice(part * part_cols, (part + 1) * part_cols)
        v[key].append(jnp.dot(v[key + "_in"], w_ref[:, cols], preferred_element_type=F32))

    def merge():
        a, l = (jnp.concatenate(v[key], axis=1) for key in ("a", "l"))
        v["m"] = jnp.concatenate(
            [jax.nn.sigmoid(gl_refs[c][...]) * l[:, c * GATE_CHUNK:(c + 1) * GATE_CHUNK]
             + jax.nn.sigmoid(ga_refs[c][...]) * a[:, c * GATE_CHUNK:(c + 1) * GATE_CHUNK]
             for c in range(N_GATE_CHUNKS)], axis=1).astype(BF16)

    def out_chunk(c):
        cols = slice(c * OUTPROJ_CHUNK, (c + 1) * OUTPROJ_CHUNK)
        h = x_ref[:, cols] + jnp.dot(v["m"], wo_ref[:, cols], preferred_element_type=F32)
        h_ref[:, cols] = h
        v["ss"] = v["ss"] + jnp.sum(h * h, axis=-1, keepdims=True)

    def norm():
        scale = lax.rsqrt(v["ss"] * (1.0 / D_MODEL) + EPS)
        hn_ref[...] = (h_ref[...] * scale * g2_ref[...]).astype(BF16)

    def seq(*stages):
        return lambda: [stage() for stage in stages]

    out_chunks = [functools.partial(out_chunk, c) for c in range(D_MODEL // OUTPROJ_CHUNK)]
    return [*[functools.partial(branch_proj, "a", attn_ref, wa_ref, part) for part in range(OUTPROJ_BRANCH_PARTS)],
            *[functools.partial(branch_proj, "l", lru_ref, wl_ref, part) for part in range(OUTPROJ_BRANCH_PARTS)],
            seq(merge, out_chunks[0]), *out_chunks[1:-1], seq(out_chunks[-1], norm)]


def _run_interleaved(project, attend, before=()):
    for i in range(max(len(project), len(attend), len(before))):
        if i < len(attend):
            attend[i][0]()
        if i < len(before):
            before[i]()
        if i < len(project):
            project[i]()
        if i < len(attend):
            attend[i][1]()


def _outproj_kernel(lru_ref, attn_ref, *refs):
    n = 2 * N_GATE_CHUNKS
    tile_refs = (refs[:N_GATE_CHUNKS], refs[N_GATE_CHUNKS:n], *refs[n:n + 5])
    _run_interleaved(_outproj_stages(lru_ref, attn_ref, *tile_refs, *refs[n + 5:]), [])


N_MIXER_INPUTS = 19


def _mixers_outproj_kernel(*refs):
    (sink_ref, q_ref, kc_ref, kp_ref, vc_ref, vp_ref, km_ref, vm_ref,
     xb_ref, yb_ref, tail_ref, h0_ref, *lru_params) = refs[:N_MIXER_INPUTS]
    n = 2 * N_GATE_CHUNKS
    rest = refs[N_MIXER_INPUTS:]
    tile_refs = (rest[:N_GATE_CHUNKS], rest[N_GATE_CHUNKS:n], *rest[n:n + 5])
    h_ref, hn_ref, hlast_ref, bias_ref, attn_ref, lru_ref, ext_ref, a3_ref, b3_ref, h3_ref, hstate_ref = rest[n + 5:]
    rows = MIXER_TILE
    t = pl.program_id(0)
    last = pl.num_programs(0) - 1

    attend = lambda: _attend_stages(t, sink_ref, q_ref, kc_ref, kp_ref, vc_ref, vp_ref, km_ref, vm_ref,
                                    bias_ref, attn_ref)
    project = lambda: _outproj_stages(lru_ref, attn_ref, *tile_refs, h_ref, hn_ref)

    def coeffs(n_block):
        def stage():
            _lru_coeffs_block(n_block, rows, ext_ref, a3_ref, b3_ref, xb_ref, *lru_params)
            cols = slice(n_block * LRU_BLOCK, (n_block + 1) * LRU_BLOCK)
            ext_ref[0:SUBLANES, cols] = ext_ref[rows:rows + SUBLANES, cols]
        return stage

    def gate_out(blocks):
        return lambda: _lru_gate_out(rows, h3_ref, yb_ref, lru_ref, blocks)

    def seq(*stages):
        return lambda: [stage() for stage in stages]

    def scan():
        h = _lru_scan(a3_ref, b3_ref, h3_ref, hstate_ref[...], 0, rows // SUBLANES)
        hstate_ref[...] = h
        hlast_ref[...] = h

    half = N_LRU_BLOCKS // 2
    all_coeffs = [coeffs(n_block) for n_block in range(N_LRU_BLOCKS)]

    @pl.when(t == 0)
    def _():
        _build_attn_bias(bias_ref)
        ext_ref[0:SUBLANES, :] = tail_ref[...]
        hstate_ref[...] = h0_ref[...]
        _run_interleaved([], attend(), all_coeffs)
        scan()

    @pl.when((t > 0) & (t < last))
    def _():
        before = [seq(gate_out(range(half)), all_coeffs[0]), gate_out(range(half, N_LRU_BLOCKS))]
        for stage in all_coeffs[1:]:
            before += [stage, seq()]
        _run_interleaved(project(), attend(), before)
        scan()

    @pl.when(t == last)
    def _():
        gate_out(range(N_LRU_BLOCKS))()
        _run_interleaved(project(), [])


def _mixers_outproj(sinks, proj, k_meta, v_meta, tail, h0, lru_params, x, g2, wl, wa, wo):
    m = x.shape[0]
    tm = MIXER_TILE
    tiles = m // tm
    cur = lambda t: jnp.minimum(t, tiles - 1)
    prv = lambda t: jnp.maximum(t - 1, 0)
    const = lambda shape: pl.BlockSpec(shape, lambda t: (0,) * len(shape))
    resident = lambda shape: pl.BlockSpec(shape, lambda t: (0,) * len(shape), pipeline_mode=pl.Buffered(1))
    kv = lambda col: (pl.BlockSpec((tm, KV_WIDTH), lambda t: (cur(t), col // KV_WIDTH)),
                      pl.BlockSpec((BLOCK, KV_WIDTH),
                                   lambda t: (jnp.maximum(ATTN_BLOCKS_PER_STEP * cur(t) - 1, 0), col // KV_WIDTH)))
    kc, kp = kv(COL_K)
    vc, vp = kv(COL_V)
    prv_spec = lambda width, cb=0: pl.BlockSpec((tm, width), lambda t: (prv(t), cb))
    tile = _nbytes((tm, D_MODEL), F32)
    scan_shape = (tm // SUBLANES, N_LRU_BLOCKS * SUBLANES, LANES)
    windows = 9 * tile
    weights = _nbytes(wl.shape, BF16) + _nbytes(wa.shape, BF16) + _nbytes(wo.shape, BF16)
    scratch = weights + 6 * tile + 2 * _nbytes(ATTN_BIAS_SHAPE, F32)
    return pl.pallas_call(
        _mixers_outproj_kernel,
        grid=(tiles + 1,),
        in_specs=[
            pl.BlockSpec(memory_space=pltpu.SMEM),
            pl.BlockSpec((tm, ATTN_WIDTH), lambda t: (cur(t), COL_Q // ATTN_WIDTH)),
            kc, kp, vc, vp,
            const((N_META, KV_WIDTH)),
            const((N_META, KV_WIDTH)),
            pl.BlockSpec((tm, LRU_WIDTH), lambda t: (cur(t), COL_XB // LRU_WIDTH)),
            prv_spec(LRU_WIDTH, COL_YB // LRU_WIDTH),
            const((SUBLANES, LRU_WIDTH)),
            const((SUBLANES, LANES)),
            *[const(p.shape) for p in lru_params],
            *[prv_spec(GATE_CHUNK, col // GATE_CHUNK + c) for col in (COL_GL, COL_GA) for c in range(N_GATE_CHUNKS)],
            prv_spec(D_MODEL),
            const((1, D_MODEL)),
            resident((LRU_WIDTH, D_MODEL)),
            resident((ATTN_WIDTH, D_MODEL)),
            resident((D_MODEL, D_MODEL)),
        ],
        out_specs=[prv_spec(D_MODEL), prv_spec(D_MODEL), const((SUBLANES, LANES))],
        out_shape=[jax.ShapeDtypeStruct((m, D_MODEL), F32), jax.ShapeDtypeStruct((m, D_MODEL), BF16),
                   jax.ShapeDtypeStruct((SUBLANES, LANES), F32)],
        scratch_shapes=[
            pltpu.VMEM(ATTN_BIAS_SHAPE, F32),
            pltpu.VMEM((tm, ATTN_WIDTH), BF16),
            pltpu.VMEM((tm, LRU_WIDTH), BF16),
            pltpu.VMEM((tm + SUBLANES, LRU_WIDTH), F32),
            pltpu.VMEM(scan_shape, F32),
            pltpu.VMEM(scan_shape, F32),
            pltpu.VMEM(scan_shape, F32),
            pltpu.VMEM((SUBLANES, LANES), F32),
        ],
        compiler_params=_cparams(("arbitrary",), windows, scratch),
        name="mixers_outproj",
    )(sinks, proj, proj, proj, proj, proj, k_meta, v_meta, proj, proj, tail, h0, *lru_params,
      *([proj] * (2 * N_GATE_CHUNKS)), x, g2, wl, wa, wo)


def _outproj(lru, attn, proj, x, g2, wl, wa, wo):
    m = x.shape[0]
    tm = _tile_plan(m)["outproj"]
    const = lambda shape: pl.BlockSpec(shape, lambda i: (0,) * len(shape))
    tile = _nbytes((tm, D_MODEL), F32)
    windows = 6 * tile + _nbytes(wl.shape, BF16) + _nbytes(wa.shape, BF16) + _nbytes(wo.shape, BF16)
    row_spec = lambda width: pl.BlockSpec((tm, width), lambda i: (i, 0))
    return pl.pallas_call(
        _outproj_kernel,
        grid=(m // tm,),
        in_specs=[
            row_spec(LRU_WIDTH),
            row_spec(ATTN_WIDTH),
            *[pl.BlockSpec((tm, GATE_CHUNK), functools.partial(lambda i, cb: (i, cb), cb=col // GATE_CHUNK + c))
              for col in (COL_GL, COL_GA) for c in range(N_GATE_CHUNKS)],
            row_spec(D_MODEL),
            const((1, D_MODEL)),
            const((LRU_WIDTH, D_MODEL)),
            const((ATTN_WIDTH, D_MODEL)),
            const((D_MODEL, D_MODEL)),
        ],
        out_specs=[row_spec(D_MODEL), row_spec(D_MODEL)],
        out_shape=[jax.ShapeDtypeStruct((m, D_MODEL), F32), jax.ShapeDtypeStruct((m, D_MODEL), BF16)],
        compiler_params=_cparams(("arbitrary",), windows, 3 * tile),
        name=f"outproj_{m}",
    )(lru, attn, *([proj] * (2 * N_GATE_CHUNKS)), x, g2, wl, wa, wo)


MLP_EPILOGUE_ROWS = 256


def _mlp_kernel(h_ref, hn_ref, gf_ref, wu_ref, wd_ref, o_ref):
    f = pl.program_id(1)
    last = pl.num_programs(1) - 1

    def chunk(rows=slice(None)):
        u = jnp.dot(hn_ref[rows, :], wu_ref[...], preferred_element_type=F32)
        u = jnp.square(jnp.maximum(u, 0.0))
        return jnp.dot(u.astype(BF16), wd_ref[...], preferred_element_type=F32)

    @pl.when(f == 0)
    def _():
        o_ref[...] = chunk()

    @pl.when((f > 0) & (f < last))
    def _():
        o_ref[...] += chunk()

    @pl.when(f == last)
    def _():
        step = min(MLP_EPILOGUE_ROWS, o_ref.shape[0])
        for r0 in range(0, o_ref.shape[0], step):
            rows = slice(r0, r0 + step)
            out = h_ref[rows, :] + (o_ref[rows, :] + chunk(rows))
            ms = jnp.mean(out * out, axis=-1, keepdims=True)
            o_ref[rows, :] = out * lax.rsqrt(ms + EPS) * gf_ref[...]


def _mlp(h, hn, gf, wu, wd):
    m = h.shape[0]
    tm, tf = _tile_plan(m)["mlp"]
    windows = (2 * _nbytes((tm, D_MODEL), F32) + _nbytes((tm, D_MODEL), BF16) + 2 * _nbytes((D_MODEL, tf), BF16))
    n_f = D_FF // tf
    return pl.pallas_call(
        _mlp_kernel,
        grid=(m // tm, n_f),
        in_specs=[
            pl.BlockSpec((tm, D_MODEL), lambda i, f: (jnp.where(f == n_f - 1, i, jnp.maximum(i - 1, 0)), 0)),
            pl.BlockSpec((tm, D_MODEL), lambda i, f: (i, 0)),
            pl.BlockSpec((1, D_MODEL), lambda i, f: (0, 0)),
            pl.BlockSpec((D_MODEL, tf), lambda i, f: (0, f)),
            pl.BlockSpec((tf, D_MODEL), lambda i, f: (f, 0)),
        ],
        out_specs=pl.BlockSpec((tm, D_MODEL), lambda i, f: (i, 0)),
        out_shape=jax.ShapeDtypeStruct((m, D_MODEL), F32),
        compiler_params=_cparams(("arbitrary", "arbitrary"), windows, 2 * _nbytes((tm, tf), F32)),
        name=f"mlp_{m}",
    )(h, hn, gf, wu, wd)


def _sample_bias():
    dist = (WINDOW - np.arange(WINDOW)).astype(np.float32)
    slopes = np.asarray(_slopes(), np.float32).reshape(N_KV_HEADS, GROUP, 1)
    return jnp.asarray(-(slopes * dist[None, None, :]))


def kernel(x_prompt, x_sample, cache_meta_k, cache_meta_v, cache_win_k, cache_win_v, state_conv, state_h,
           meta_tokens, norm1_g, w_in, conv_w, conv_b, w_gate_x, b_gate_x, w_gate_a, b_gate_a, lru_a_param,
           attn_sinks, w_lru_out, w_attn_out, w_o, norm2_g, w_mlp_up, w_mlp_down, final_norm_g):
    row = lambda v: v.reshape(1, -1)
    g1, g2, gf = row(norm1_g[0]), row(norm2_g[0]), row(final_norm_g)
    wgx, wga = w_gate_x[0].astype(BF16), w_gate_a[0].astype(BF16)
    cw, cb = conv_w[0], row(conv_b[0])
    bgx, bga, ap = row(b_gate_x[0]), row(b_gate_a[0]), row(lru_a_param[0])
    sinks = attn_sinks[0]

    x_main = x_prompt.reshape(SEQ, D_MODEL)
    x_extra = jnp.concatenate([
        x_sample.reshape(DEC_BATCH, D_MODEL),
        jnp.zeros((EXTRA_ROWS - DEC_BATCH - N_META, D_MODEL), F32),
        meta_tokens.astype(F32)], axis=0)

    proj_e, w_in_b = _inproj(x_extra, g1, w_in[0])
    proj_m, wl, wa, wo, wu, wd = _inproj(
        x_main, g1, w_in_b, (w_lru_out[0], w_attn_out[0], w_o[0], w_mlp_up[0], w_mlp_down[0]))

    lru_e, h_sample, h_meta = _lru_extra(proj_e, state_conv[0].reshape(DEC_BATCH, (CONV_W - 1) * LRU_WIDTH),
                                         state_h[0], cw, cb, wgx, bgx, wga, bga, ap)
    tail = proj_e[EXTRA_ROWS - SUBLANES:, COL_XB:COL_XB + LRU_WIDTH]

    k_meta = proj_e[META_ROW0:, COL_K:COL_K + KV_WIDTH]
    v_meta = proj_e[META_ROW0:, COL_V:COL_V + KV_WIDTH]
    attn_meta = _attn_meta(sinks, proj_e[META_ROW0:, COL_Q:COL_Q + ATTN_WIDTH], k_meta, v_meta)
    k_new = proj_e[:DEC_BATCH, COL_K:COL_K + KV_WIDTH]
    v_new = proj_e[:DEC_BATCH, COL_V:COL_V + KV_WIDTH]
    q4 = proj_e[:DEC_BATCH, COL_Q:COL_Q + ATTN_WIDTH].reshape(DEC_BATCH, N_KV_HEADS, GROUP, HEAD_DIM)
    flat = lambda c, n: c[0].reshape(DEC_BATCH, n, KV_WIDTH)
    window_minor = lambda c: jnp.transpose(c[0], (0, 2, 3, 1))
    window_major = lambda c: jnp.transpose(c, (0, 3, 1, 2))[None]
    attn_s, kw_out, vw_out = _attn_sample(
        q4, flat(cache_meta_k, N_META), window_minor(cache_win_k), k_new,
        flat(cache_meta_v, N_META), window_minor(cache_win_v), v_new,
        _sample_bias(), sinks.reshape(N_KV_HEADS, GROUP, 1), bb=8)
    attn_e = jnp.concatenate([
        attn_s.reshape(DEC_BATCH, ATTN_WIDTH),
        jnp.zeros((EXTRA_ROWS - DEC_BATCH - N_META, ATTN_WIDTH), F32),
        attn_meta], axis=0).astype(BF16)

    res_m, resn_m, h_last = _mixers_outproj(
        sinks, proj_m, k_meta, v_meta, tail, h_meta, (cw, cb, wgx, bgx, wga, bga, ap), x_main, g2, wl, wa, wo)
    res_e, resn_e = _outproj(lru_e, attn_e, proj_e, x_extra, g2, wl, wa, wo)
    y_m = _mlp(res_m, resn_m, gf, wu, wd)
    y_e = _mlp(res_e, resn_e, gf, wu, wd)

    kv5 = lambda a, n: a.reshape(1, -1, n, N_KV_HEADS, HEAD_DIM)
    return (
        y_m.reshape(1, SEQ, D_MODEL),
        y_e[:DEC_BATCH].reshape(DEC_BATCH, 1, D_MODEL),
        kv5(k_meta, N_META), kv5(v_meta, N_META),
        kv5(proj_m[SEQ - WINDOW:, COL_K:COL_K + KV_WIDTH], WINDOW),
        kv5(proj_m[SEQ - WINDOW:, COL_V:COL_V + KV_WIDTH], WINDOW),
        proj_m[SEQ - (CONV_W - 1):, COL_XB:COL_XB + LRU_WIDTH].reshape(1, 1, CONV_W - 1, LRU_WIDTH),
        h_last.reshape(1, 1, LRU_WIDTH),
        window_major(kw_out), window_major(vw_out),
        jnp.concatenate([state_conv[0][:, 1:], proj_e[:DEC_BATCH, None, COL_XB:COL_XB + LRU_WIDTH]], axis=1)[None],
        h_sample[None],
    )
```

```python
import functools

import numpy as np
import jax
import jax.numpy as jnp
from jax import lax
from jax.experimental import pallas as pl
from jax.experimental.pallas import tpu as pltpu

D_MODEL = 2048
SEQ = 16384
DEC_BATCH = 128
N_META = 16
LRU_WIDTH = 1024
N_LRU_BLOCKS = 8
LRU_BLOCK = LRU_WIDTH // N_LRU_BLOCKS
CONV_W = 4
LRU_C = 8.0
N_HEADS = 16
N_KV_HEADS = 4
HEAD_DIM = 64
GROUP = N_HEADS // N_KV_HEADS
ATTN_WIDTH = N_HEADS * HEAD_DIM
KV_WIDTH = N_KV_HEADS * HEAD_DIM
WINDOW = 128
BLOCK = 128
D_FF = 4 * D_MODEL
EPS = 1e-6
NEG_INF = -1e30
IN_WIDTH = 2 * LRU_WIDTH + ATTN_WIDTH + 2 * KV_WIDTH + 2 * D_MODEL
COL_XB, COL_YB, COL_Q = 0, LRU_WIDTH, 2 * LRU_WIDTH
COL_K = COL_Q + ATTN_WIDTH
COL_V = COL_K + KV_WIDTH
COL_GL = COL_V + KV_WIDTH
COL_GA = COL_GL + D_MODEL
GATE_CHUNK = 512

EXTRA_ROWS = 256
META_ROW0 = EXTRA_ROWS - N_META
SUBLANES = 8
LANES = 128
MIB = 1024 * 1024

F32 = jnp.float32
BF16 = jnp.bfloat16


def _slopes():
    return [2.0 ** (-8.0 * (h + 1) / N_HEADS) for h in range(N_HEADS)]


V7X_VMEM_REQUEST_CAP = 60 * MIB


def _nbytes(shape, dtype):
    return int(np.prod(shape)) * jnp.dtype(dtype).itemsize


def _cparams(sem, pipelined, resident=0):
    estimate = 2 * pipelined + resident
    limit = min(V7X_VMEM_REQUEST_CAP, estimate + estimate // 4 + 2 * MIB)
    return pltpu.CompilerParams(dimension_semantics=sem, vmem_limit_bytes=limit)


def _tile_plan(rows):
    if rows == EXTRA_ROWS:
        return dict(inproj=(EXTRA_ROWS, 1536), outproj=EXTRA_ROWS, mlp=(EXTRA_ROWS, 2048))
    return dict(inproj=(1024, 1536), lru=512, outproj=256, mlp=(1024, 512))


CAST_STEPS = 64


def _cast_specs(weights, step_of):
    specs, shapes, nbytes = [], [], 0
    for w in weights:
        block = (w.shape[0] // CAST_STEPS, w.shape[1])
        specs.append(pl.BlockSpec(block, lambda *idx: (jnp.minimum(step_of(*idx), CAST_STEPS - 1), 0)))
        shapes.append(jax.ShapeDtypeStruct(w.shape, BF16))
        nbytes += _nbytes(block, F32) + _nbytes(block, BF16)
    return specs, shapes, nbytes


def _cast_chunks(src_refs, dst_refs):
    for src_ref, dst_ref in zip(src_refs, dst_refs):
        dst_ref[...] = src_ref[...].astype(BF16)


def _inproj_kernel(x_ref, g_ref, w_ref, *refs, n_cast, emit_w):
    cast_src, o_ref, cast_dst, xs_ref = refs[:n_cast], refs[n_cast], refs[n_cast + 1:2 * n_cast + 1], refs[-1]

    @pl.when(pl.program_id(1) == 0)
    def _():
        x = x_ref[...]
        ms = jnp.mean(x * x, axis=-1, keepdims=True)
        xs_ref[...] = (x * lax.rsqrt(ms + EPS) * g_ref[...]).astype(BF16)

    w = w_ref[...]
    if emit_w:
        w = w.astype(BF16)
        refs[-2][...] = w
    o_ref[...] = jnp.dot(xs_ref[...], w, preferred_element_type=F32)
    _cast_chunks(cast_src, cast_dst)


def _inproj(x, g, w, cast_weights=()):
    m = x.shape[0]
    tm, tn = _tile_plan(m)["inproj"]
    n_col = IN_WIDTH // tn
    emit_w = w.dtype == F32
    assert not emit_w or m == tm, "the bf16 copy of w is written once per column block"
    cast_specs, cast_shapes, cast_bytes = _cast_specs(cast_weights, lambda i, j: i * n_col + j)
    assert not cast_weights or (m // tm) * n_col >= CAST_STEPS
    w_spec = pl.BlockSpec((D_MODEL, tn), lambda i, j: (0, j))
    windows = (_nbytes((tm, D_MODEL), F32) + _nbytes((D_MODEL, tn), w.dtype) + _nbytes((tm, tn), F32) + cast_bytes
               + emit_w * _nbytes((D_MODEL, tn), BF16))
    return pl.pallas_call(
        functools.partial(_inproj_kernel, n_cast=len(cast_weights), emit_w=emit_w),
        grid=(m // tm, n_col),
        in_specs=[
            pl.BlockSpec((tm, D_MODEL), lambda i, j: (i, 0)),
            pl.BlockSpec((1, D_MODEL), lambda i, j: (0, 0)),
            w_spec,
            *cast_specs,
        ],
        out_specs=[pl.BlockSpec((tm, tn), lambda i, j: (i, j)), *cast_specs, *([w_spec] if emit_w else [])],
        out_shape=[jax.ShapeDtypeStruct((m, IN_WIDTH), F32), *cast_shapes,
                   *([jax.ShapeDtypeStruct(w.shape, BF16)] if emit_w else [])],
        scratch_shapes=[pltpu.VMEM((tm, D_MODEL), BF16)],
        compiler_params=_cparams(("arbitrary", "arbitrary"), windows, _nbytes((tm, D_MODEL), BF16)),
        name=f"inproj_{m}",
    )(x, g, w, *cast_weights)


def _gate_ab(xc_n, n, wgx_ref, bgx_ref, wga_ref, bga_ref, ap_ref):
    cols = slice(n * LRU_BLOCK, (n + 1) * LRU_BLOCK)
    xcb = xc_n.astype(BF16)
    gx = jnp.dot(xcb, wgx_ref[n], preferred_element_type=F32) + bgx_ref[:, cols]
    ga = jnp.dot(xcb, wga_ref[n], preferred_element_type=F32) + bga_ref[:, cols]
    gate_x = jax.nn.sigmoid(gx)
    gate_a = jax.nn.sigmoid(ga)
    log_a = -LRU_C * gate_a * jax.nn.softplus(-ap_ref[:, cols])
    a = jnp.exp(log_a)
    z = -jnp.tanh(log_a) * (a * a + 1.0)
    root = jnp.where(z > 0.0, z * lax.rsqrt(z), 0.0)
    b = root * gate_x * xc_n
    return a, b


def _lru_coeffs(rows, ext_ref, a3_ref, b3_ref, xb, cw_ref, cb_ref, wgx_ref, bgx_ref, wga_ref, bga_ref, ap_ref):
    for n in range(N_LRU_BLOCKS):
        _lru_coeffs_block(n, rows, ext_ref, a3_ref, b3_ref, xb, cw_ref, cb_ref, wgx_ref, bgx_ref, wga_ref, bga_ref,
                          ap_ref)


def _lru_coeffs_block(n, rows, ext_ref, a3_ref, b3_ref, xb, cw_ref, cb_ref, wgx_ref, bgx_ref, wga_ref, bga_ref,
                      ap_ref):
    groups = rows // SUBLANES
    cols = slice(n * LRU_BLOCK, (n + 1) * LRU_BLOCK)
    ext_ref[SUBLANES:SUBLANES + rows, cols] = xb[:, cols]
    x_ext = ext_ref[:, cols]
    xc = x_ext * cw_ref[0:1, cols]
    for t in range(1, CONV_W):
        xc = pltpu.roll(xc, 1, axis=0) + x_ext * cw_ref[t:t + 1, cols]
    xc = xc[SUBLANES:, :] + cb_ref[:, cols]
    a, b = _gate_ab(xc, n, wgx_ref, bgx_ref, wga_ref, bga_ref, ap_ref)
    a3_ref[:, n * SUBLANES:(n + 1) * SUBLANES, :] = a.reshape(groups, SUBLANES, LANES)
    b3_ref[:, n * SUBLANES:(n + 1) * SUBLANES, :] = b.reshape(groups, SUBLANES, LANES)


SCAN_UNROLL = 4


def _lru_scan(a3_ref, b3_ref, h3_ref, h_init, g_lo, g_hi):
    def group_step(i, h):
        for r in range(0, SUBLANES, 2):
            s0, s1 = (pl.ds(r + k, N_LRU_BLOCKS, stride=SUBLANES) for k in range(2))
            a0, b0, a1, b1 = a3_ref[i, s0, :], b3_ref[i, s0, :], a3_ref[i, s1, :], b3_ref[i, s1, :]
            h3_ref[i, s0, :] = a0 * h + b0
            h = (a1 * a0) * h + (a1 * b0 + b1)
            h3_ref[i, s1, :] = h
        return h

    return lax.fori_loop(g_lo, g_hi, group_step, h_init, unroll=SCAN_UNROLL)


def _lru_gate_out(rows, h3_ref, yb_ref, lru_ref, blocks=range(N_LRU_BLOCKS)):
    for n in blocks:
        cols = slice(n * LRU_BLOCK, (n + 1) * LRU_BLOCK)
        hs = h3_ref[:, n * SUBLANES:(n + 1) * SUBLANES, :].reshape(rows, LANES)
        lru_ref[:, cols] = (jax.nn.gelu(yb_ref[:, cols]) * hs).astype(BF16)


def _lru_rows(rows, ext_ref, a3_ref, b3_ref, h3_ref, xb, yb_ref, cw_ref, cb_ref,
              wgx_ref, bgx_ref, wga_ref, bga_ref, ap_ref, lru_ref, h_init, g_lo, g_hi):
    _lru_coeffs(rows, ext_ref, a3_ref, b3_ref, xb, cw_ref, cb_ref, wgx_ref, bgx_ref, wga_ref, bga_ref, ap_ref)
    h = _lru_scan(a3_ref, b3_ref, h3_ref, h_init, g_lo, g_hi)
    _lru_gate_out(rows, h3_ref, yb_ref, lru_ref)
    return h


def _lru_extra_kernel(xb_ref, yb_ref, sconv_ref, sh_ref, cw_ref, cb_ref, wgx_ref, bgx_ref, wga_ref, bga_ref,
                      ap_ref, lru_ref, hs_ref, hmeta_ref, ext_ref, a3_ref, b3_ref, h3_ref):
    nb = DEC_BATCH
    xb = xb_ref[0:nb, :]
    xc = sconv_ref[:, 0:LRU_WIDTH] * cw_ref[0:1, :]
    xc = xc + sconv_ref[:, LRU_WIDTH:2 * LRU_WIDTH] * cw_ref[1:2, :]
    xc = xc + sconv_ref[:, 2 * LRU_WIDTH:3 * LRU_WIDTH] * cw_ref[2:3, :]
    xc = xc + xb * cw_ref[3:4, :]
    xc = xc + cb_ref[...]
    for n in range(N_LRU_BLOCKS):
        cols = slice(n * LRU_BLOCK, (n + 1) * LRU_BLOCK)
        a, b = _gate_ab(xc[:, cols], n, wgx_ref, bgx_ref, wga_ref, bga_ref, ap_ref)
        h = a * sh_ref[:, cols] + b
        hs_ref[:, cols] = h
        lru_ref[0:nb, cols] = (jax.nn.gelu(yb_ref[0:nb, cols]) * h).astype(BF16)
    rows = EXTRA_ROWS - nb
    ext_ref[0:SUBLANES, :] = jnp.zeros((SUBLANES, LRU_WIDTH), F32)
    h3_ref[...] = jnp.zeros(h3_ref.shape, F32)
    first_group = (rows - N_META) // SUBLANES
    hmeta_ref[...] = _lru_rows(rows, ext_ref, a3_ref, b3_ref, h3_ref, xb_ref[nb:, :], yb_ref.at[nb:, :], cw_ref,
                               cb_ref, wgx_ref, bgx_ref, wga_ref, bga_ref, ap_ref, lru_ref.at[nb:, :],
                               jnp.zeros((SUBLANES, LANES), F32), first_group, rows // SUBLANES)


def _lru_extra(proj_e, sconv, sh, cw, cb, wgx, bgx, wga, bga, ap):
    const = lambda shape: pl.BlockSpec(shape, lambda i: (0,) * len(shape))
    rows = EXTRA_ROWS - DEC_BATCH
    scan_shape = (rows // SUBLANES, N_LRU_BLOCKS * SUBLANES, LANES)
    return pl.pallas_call(
        _lru_extra_kernel,
        grid=(1,),
        in_specs=[
            pl.BlockSpec((EXTRA_ROWS, LRU_WIDTH), lambda i: (0, COL_XB // LRU_WIDTH)),
            pl.BlockSpec((EXTRA_ROWS, LRU_WIDTH), lambda i: (0, COL_YB // LRU_WIDTH)),
            const((DEC_BATCH, (CONV_W - 1) * LRU_WIDTH)),
            const((DEC_BATCH, LRU_WIDTH)),
            const((CONV_W, LRU_WIDTH)),
            const((1, LRU_WIDTH)),
            const((N_LRU_BLOCKS, LRU_BLOCK, LRU_BLOCK)),
            const((1, LRU_WIDTH)),
            const((N_LRU_BLOCKS, LRU_BLOCK, LRU_BLOCK)),
            const((1, LRU_WIDTH)),
            const((1, LRU_WIDTH)),
        ],
        out_specs=[
            const((EXTRA_ROWS, LRU_WIDTH)),
            const((DEC_BATCH, LRU_WIDTH)),
            const((SUBLANES, LANES)),
        ],
        out_shape=[
            jax.ShapeDtypeStruct((EXTRA_ROWS, LRU_WIDTH), BF16),
            jax.ShapeDtypeStruct((DEC_BATCH, LRU_WIDTH), F32),
            jax.ShapeDtypeStruct((SUBLANES, LANES), F32),
        ],
        scratch_shapes=[
            pltpu.VMEM((rows + SUBLANES, LRU_WIDTH), F32),
            pltpu.VMEM(scan_shape, F32),
            pltpu.VMEM(scan_shape, F32),
            pltpu.VMEM(scan_shape, F32),
        ],
        compiler_params=_cparams(("arbitrary",), 8 * _nbytes((DEC_BATCH, LRU_WIDTH), F32),
                                 8 * _nbytes((rows, LRU_WIDTH), F32)),
        name="lru_extra",
    )(proj_e, proj_e, sconv, sh, cw, cb, wgx, bgx, wga, bga, ap)


def _attend_kv_head(q_h, kk, vv, bias, sinks):
    return _attend_values(*_attend_weights(q_h, kk, bias, sinks), vv)


def _attend_weights(q_h, kk, bias, sinks):
    r = q_h.shape[0]
    qs = jnp.concatenate([q_h[:, g * HEAD_DIM:(g + 1) * HEAD_DIM] for g in range(GROUP)], axis=0)
    qs = (qs * (HEAD_DIM ** -0.5)).astype(BF16)
    s = lax.dot_general(kk.astype(BF16), qs, (((1,), (1,)), ((), ())), preferred_element_type=F32) + bias
    sink = jnp.concatenate([jnp.full((1, r), sinks[g], F32) for g in range(GROUP)], axis=1)
    m = jnp.maximum(jnp.max(s, axis=0, keepdims=True), sink)
    e = jnp.exp(s - m)
    denom = jnp.sum(e, axis=0, keepdims=True) + jnp.exp(sink - m)
    return e.astype(BF16), denom


def _attend_values(e, denom, vv):
    r = e.shape[1] // GROUP
    o_t = jnp.dot(vv.T.astype(BF16), e, preferred_element_type=F32) / denom
    pairs = [jnp.concatenate([o_t[:, g * r:(g + 1) * r], o_t[:, (g + 1) * r:(g + 2) * r]], axis=0).T
             for g in range(0, GROUP, 2)]
    return jnp.concatenate(pairs, axis=-1)


ATTN_BLOCKS_PER_STEP = 2
MIXER_TILE = ATTN_BLOCKS_PER_STEP * BLOCK
ATTN_QUERIES = BLOCK // 2
ATTN_BAND = WINDOW + ATTN_QUERIES
ATTN_KEYS = N_META + ATTN_BAND
ATTN_TABLES = WINDOW // ATTN_QUERIES + 1


ATTN_BIAS_SHAPE = (ATTN_TABLES, N_KV_HEADS, ATTN_KEYS, GROUP * ATTN_QUERIES)


def _build_attn_bias(bias_ref):
    slopes = _slopes()
    key = lax.broadcasted_iota(jnp.int32, (ATTN_KEYS, ATTN_QUERIES), 0)
    qry = lax.broadcasted_iota(jnp.int32, (ATTN_KEYS, ATTN_QUERIES), 1)
    d = qry + WINDOW - (key - N_META)
    dist = jnp.where(key < N_META, 0, d).astype(F32)
    for j in range(ATTN_TABLES):
        band_ok = (d >= 0) & (d <= WINDOW) & (key >= N_META + WINDOW - j * ATTN_QUERIES)
        valid = (key < N_META) | band_ok
        for hd in range(N_HEADS):
            lanes = slice((hd % GROUP) * ATTN_QUERIES, (hd % GROUP + 1) * ATTN_QUERIES)
            bias_ref[j, hd // GROUP, :, lanes] = jnp.where(valid, -(slopes[hd] * dist), NEG_INF)


def _attend_stages(tile, sink_ref, q_ref, kc_ref, kp_ref, vc_ref, vp_ref, km_ref, vm_ref, bias_ref, o_ref):
    def pair(sub, h):
        rows = slice(sub * ATTN_QUERIES, (sub + 1) * ATTN_QUERIES)
        lo = sub * ATTN_QUERIES - WINDOW
        n_before = max(-lo, 0) // ATTN_QUERIES
        table = jnp.where(tile == 0, ATTN_TABLES - 1 - n_before, ATTN_TABLES - 1) if n_before else ATTN_TABLES - 1
        hc = slice(h * HEAD_DIM, (h + 1) * HEAD_DIM)
        qc = slice(h * GROUP * HEAD_DIM, (h + 1) * GROUP * HEAD_DIM)
        v = {}

        def keys(meta_ref, cur_ref, prev_ref):
            if lo < 0:
                band = [prev_ref[WINDOW + lo:WINDOW, hc], cur_ref[0:lo + ATTN_BAND, hc]]
            else:
                band = [cur_ref[lo:lo + ATTN_BAND, hc]]
            return jnp.concatenate([meta_ref[:, hc], *band], axis=0)

        def weights():
            v["w"] = _attend_weights(q_ref[rows, qc], keys(km_ref, kc_ref, kp_ref), bias_ref[table, h],
                                     [sink_ref[h * GROUP + g] for g in range(GROUP)])

        def values():
            o_ref[rows, qc] = _attend_values(*v["w"], keys(vm_ref, vc_ref, vp_ref)).astype(BF16)

        return weights, values

    return [pair(sub, h) for sub in range(MIXER_TILE // ATTN_QUERIES) for h in range(N_KV_HEADS)]


def _attn_meta_kernel(sink_ref, q_ref, km_ref, vm_ref, o_ref):
    key = lax.broadcasted_iota(jnp.int32, (N_META, GROUP * N_META), 0)
    qry = lax.broadcasted_iota(jnp.int32, (N_META, GROUP * N_META), 1) % N_META
    bias = jnp.where(qry >= key, 0.0, NEG_INF)
    for h in range(N_KV_HEADS):
        hc = slice(h * HEAD_DIM, (h + 1) * HEAD_DIM)
        q_h = q_ref[:, h * GROUP * HEAD_DIM:(h + 1) * GROUP * HEAD_DIM]
        o = _attend_kv_head(q_h, km_ref[:, hc], vm_ref[:, hc], bias,
                            [sink_ref[h * GROUP + g] for g in range(GROUP)])
        o_ref[:, h * GROUP * HEAD_DIM:(h + 1) * GROUP * HEAD_DIM] = o


def _attn_meta(sinks, q_meta, k_meta, v_meta):
    const = lambda shape: pl.BlockSpec(shape, lambda i: (0,) * len(shape))
    return pl.pallas_call(
        _attn_meta_kernel,
        grid=(1,),
        in_specs=[pl.BlockSpec(memory_space=pltpu.SMEM), const((N_META, ATTN_WIDTH)),
                  const((N_META, KV_WIDTH)), const((N_META, KV_WIDTH))],
        out_specs=const((N_META, ATTN_WIDTH)),
        out_shape=jax.ShapeDtypeStruct((N_META, ATTN_WIDTH), F32),
        compiler_params=_cparams(("arbitrary",), 4 * _nbytes((N_META, ATTN_WIDTH), F32)),
        name="attn_meta",
    )(sinks, q_meta, k_meta, v_meta)


def _attn_sample_kernel(q_ref, km_ref, kw_ref, kn_ref, knt_ref, vm_ref, vw_ref, vn_ref, vnt_ref, bias_ref, sink_ref,
                        o_ref, kwo_ref, vwo_ref):
    bb = q_ref.shape[0]
    position = lax.broadcasted_iota(jnp.int32, (HEAD_DIM, WINDOW), 1)
    for src_ref, newt_ref, dst_ref in ((kw_ref, knt_ref, kwo_ref), (vw_ref, vnt_ref, vwo_ref)):
        for h in range(N_KV_HEADS):
            new_t = newt_ref[0, h * HEAD_DIM:(h + 1) * HEAD_DIM, :]
            for j in range(bb):
                dst_ref[j, h] = jnp.where(position == WINDOW - 1, new_t[:, j:j + 1],
                                          pltpu.roll(src_ref[j, h], WINDOW - 1, axis=1))
    dot_qk = functools.partial(jnp.einsum, "bgd,bjd->bgj", preferred_element_type=F32)
    dot_qkt = functools.partial(jnp.einsum, "bgd,bdj->bgj", preferred_element_type=F32)
    dot_pv = functools.partial(jnp.einsum, "bgj,bjd->bgd", preferred_element_type=F32)
    dot_pvt = functools.partial(jnp.einsum, "bgj,bdj->bgd", preferred_element_type=F32)
    for h in range(N_KV_HEADS):
        hc = slice(h * HEAD_DIM, (h + 1) * HEAD_DIM)
        q = (q_ref[:, h] * (HEAD_DIM ** -0.5)).astype(BF16)
        k_new = kn_ref[:, :, hc].astype(BF16).astype(F32)
        v_new = vn_ref[:, :, hc].astype(BF16).astype(F32)
        s_m = dot_qk(q, km_ref[:, :, hc].astype(BF16))
        s_w = dot_qkt(q, kw_ref[:, h].astype(BF16)) + bias_ref[h]
        s_n = jnp.sum(q.astype(F32) * k_new, axis=-1, keepdims=True)
        sink = sink_ref[h]
        m = jnp.maximum(jnp.maximum(jnp.max(s_m, axis=-1, keepdims=True), jnp.max(s_w, axis=-1, keepdims=True)),
                        jnp.maximum(s_n, sink))
        e_m, e_w, e_n = jnp.exp(s_m - m), jnp.exp(s_w - m), jnp.exp(s_n - m)
        denom = (jnp.sum(e_m, axis=-1, keepdims=True) + jnp.sum(e_w, axis=-1, keepdims=True) + e_n
                 + jnp.exp(sink - m))
        o = (dot_pv(e_m.astype(BF16), vm_ref[:, :, hc].astype(BF16))
             + dot_pvt(e_w.astype(BF16), vw_ref[:, h].astype(BF16))
             + e_n.astype(BF16).astype(F32) * v_new)
        o_ref[:, h] = o / denom


def _attn_sample(q4, k_meta, k_win, k_new, v_meta, v_win, v_new, bias, sinks4, bb):
    b3 = lambda rows: pl.BlockSpec((bb, rows, KV_WIDTH), lambda i: (i, 0, 0))
    q_spec = pl.BlockSpec((bb, N_KV_HEADS, GROUP, HEAD_DIM), lambda i: (i, 0, 0, 0))
    win_spec = pl.BlockSpec((bb, N_KV_HEADS, HEAD_DIM, WINDOW), lambda i: (i, 0, 0, 0))
    new_t_spec = pl.BlockSpec((1, KV_WIDTH, bb), lambda i: (i, 0, 0))
    new_t = lambda new: jnp.transpose(new.reshape(DEC_BATCH // bb, bb, KV_WIDTH), (0, 2, 1))
    win_shape = jax.ShapeDtypeStruct((DEC_BATCH, N_KV_HEADS, HEAD_DIM, WINDOW), F32)
    return pl.pallas_call(
        _attn_sample_kernel,
        grid=(DEC_BATCH // bb,),
        in_specs=[
            q_spec,
            b3(N_META), win_spec, b3(1), new_t_spec,
            b3(N_META), win_spec, b3(1), new_t_spec,
            pl.BlockSpec((N_KV_HEADS, GROUP, WINDOW), lambda i: (0, 0, 0)),
            pl.BlockSpec((N_KV_HEADS, GROUP, 1), lambda i: (0, 0, 0)),
        ],
        out_specs=[q_spec, win_spec, win_spec],
        out_shape=[jax.ShapeDtypeStruct((DEC_BATCH, N_KV_HEADS, GROUP, HEAD_DIM), F32), win_shape, win_shape],
        compiler_params=_cparams(("arbitrary",), 5 * _nbytes((bb, WINDOW, KV_WIDTH), F32),
                                 4 * _nbytes((bb, WINDOW, KV_WIDTH), F32)),
        name="attn_sample",
    )(q4, k_meta, k_win, k_new[:, None, :], new_t(k_new), v_meta, v_win, v_new[:, None, :], new_t(v_new), bias, sinks4)


N_GATE_CHUNKS = D_MODEL // GATE_CHUNK
OUTPROJ_BRANCH_PARTS = 4
OUTPROJ_CHUNK = D_MODEL // (2 * OUTPROJ_BRANCH_PARTS)


def _outproj_stages(lru_ref, attn_ref, gl_refs, ga_refs, x_ref, g2_ref, wl_ref, wa_ref, wo_ref, h_ref, hn_ref):
    v = {"a": [], "l": [], "ss": 0.0}
    part_cols = D_MODEL // OUTPROJ_BRANCH_PARTS

    def branch_proj(key, src_ref, w_ref, part):
        if part == 0:
            v[key + "_in"] = src_ref[...]
        cols = slice(part * part_cols, (part + 1) * part_cols)
        v[key].append(jnp.dot(v[key + "_in"], w_ref[:, cols], preferred_element_type=F32))

    def merge():
        a, l = (jnp.concatenate(v[key], axis=1) for key in ("a", "l"))
        v["m"] = jnp.concatenate(
            [jax.nn.sigmoid(gl_refs[c][...]) * l[:, c * GATE_CHUNK:(c + 1) * GATE_CHUNK]
             + jax.nn.sigmoid(ga_refs[c][...]) * a[:, c * GATE_CHUNK:(c + 1) * GATE_CHUNK]
             for c in range(N_GATE_CHUNKS)], axis=1).astype(BF16)

    def out_chunk(c):
        cols = slice(c * OUTPROJ_CHUNK, (c + 1) * OUTPROJ_CHUNK)
        h = x_ref[:, cols] + jnp.dot(v["m"], wo_ref[:, cols], preferred_element_type=F32)
        h_ref[:, cols] = h
        v["ss"] = v["ss"] + jnp.sum(h * h, axis=-1, keepdims=True)

    def norm():
        scale = lax.rsqrt(v["ss"] * (1.0 / D_MODEL) + EPS)
        hn_ref[...] = (h_ref[...] * scale * g2_ref[...]).astype(BF16)

    def seq(*stages):
        return lambda: [stage() for stage in stages]

    out_chunks = [functools.partial(out_chunk, c) for c in range(D_MODEL // OUTPROJ_CHUNK)]
    return [*[functools.partial(branch_proj, "a", attn_ref, wa_ref, part) for part in range(OUTPROJ_BRANCH_PARTS)],
            *[functools.partial(branch_proj, "l", lru_ref, wl_ref, part) for part in range(OUTPROJ_BRANCH_PARTS)],
            seq(merge, out_chunks[0]), *out_chunks[1:-1], seq(out_chunks[-1], norm)]


def _run_interleaved(project, attend, before=()):
    for i in range(max(len(project), len(attend), len(before))):
        if i < len(attend):
            attend[i][0]()
        if i < len(before):
            before[i]()
        if i < len(project):
            project[i]()
        if i < len(attend):
            attend[i][1]()


def _outproj_kernel(lru_ref, attn_ref, *refs):
    n = 2 * N_GATE_CHUNKS
    tile_refs = (refs[:N_GATE_CHUNKS], refs[N_GATE_CHUNKS:n], *refs[n:n + 5])
    _run_interleaved(_outproj_stages(lru_ref, attn_ref, *tile_refs, *refs[n + 5:]), [])


N_MIXER_INPUTS = 19


def _mixers_outproj_kernel(*refs):
    (sink_ref, q_ref, kc_ref, kp_ref, vc_ref, vp_ref, km_ref, vm_ref,
     xb_ref, yb_ref, tail_ref, h0_ref, *lru_params) = refs[:N_MIXER_INPUTS]
    n = 2 * N_GATE_CHUNKS
    rest = refs[N_MIXER_INPUTS:]
    tile_refs = (rest[:N_GATE_CHUNKS], rest[N_GATE_CHUNKS:n], *rest[n:n + 5])
    h_ref, hn_ref, hlast_ref, bias_ref, attn_ref, lru_ref, ext_ref, a3_ref, b3_ref, h3_ref, hstate_ref = rest[n + 5:]
    rows = MIXER_TILE
    t = pl.program_id(0)
    last = pl.num_programs(0) - 1

    attend = lambda: _attend_stages(t, sink_ref, q_ref, kc_ref, kp_ref, vc_ref, vp_ref, km_ref, vm_ref,
                                    bias_ref, attn_ref)
    project = lambda: _outproj_stages(lru_ref, attn_ref, *tile_refs, h_ref, hn_ref)

    def coeffs(n_block):
        def stage():
            _lru_coeffs_block(n_block, rows, ext_ref, a3_ref, b3_ref, xb_ref, *lru_params)
            cols = slice(n_block * LRU_BLOCK, (n_block + 1) * LRU_BLOCK)
            ext_ref[0:SUBLANES, cols] = ext_ref[rows:rows + SUBLANES, cols]
        return stage

    def gate_out(blocks):
        return lambda: _lru_gate_out(rows, h3_ref, yb_ref, lru_ref, blocks)

    def seq(*stages):
        return lambda: [stage() for stage in stages]

    def scan():
        h = _lru_scan(a3_ref, b3_ref, h3_ref, hstate_ref[...], 0, rows // SUBLANES)
        hstate_ref[...] = h
        hlast_ref[...] = h

    half = N_LRU_BLOCKS // 2
    all_coeffs = [coeffs(n_block) for n_block in range(N_LRU_BLOCKS)]

    @pl.when(t == 0)
    def _():
        _build_attn_bias(bias_ref)
        ext_ref[0:SUBLANES, :] = tail_ref[...]
        hstate_ref[...] = h0_ref[...]
        _run_interleaved([], attend(), all_coeffs)
        scan()

    @pl.when((t > 0) & (t < last))
    def _():
        before = [seq(gate_out(range(half)), all_coeffs[0]), gate_out(range(half, N_LRU_BLOCKS))]
        for stage in all_coeffs[1:]:
            before += [stage, seq()]
        _run_interleaved(project(), attend(), before)
        scan()

    @pl.when(t == last)
    def _():
        gate_out(range(N_LRU_BLOCKS))()
        _run_interleaved(project(), [])


def _mixers_outproj(sinks, proj, k_meta, v_meta, tail, h0, lru_params, x, g2, wl, wa, wo):
    m = x.shape[0]
    tm = MIXER_TILE
    tiles = m // tm
    cur = lambda t: jnp.minimum(t, tiles - 1)
    prv = lambda t: jnp.maximum(t - 1, 0)
    const = lambda shape: pl.BlockSpec(shape, lambda t: (0,) * len(shape))
    resident = lambda shape: pl.BlockSpec(shape, lambda t: (0,) * len(shape), pipeline_mode=pl.Buffered(1))
    kv = lambda col: (pl.BlockSpec((tm, KV_WIDTH), lambda t: (cur(t), col // KV_WIDTH)),
                      pl.BlockSpec((BLOCK, KV_WIDTH),
                                   lambda t: (jnp.maximum(ATTN_BLOCKS_PER_STEP * cur(t) - 1, 0), col // KV_WIDTH)))
    kc, kp = kv(COL_K)
    vc, vp = kv(COL_V)
    prv_spec = lambda width, cb=0: pl.BlockSpec((tm, width), lambda t: (prv(t), cb))
    tile = _nbytes((tm, D_MODEL), F32)
    scan_shape = (tm // SUBLANES, N_LRU_BLOCKS * SUBLANES, LANES)
    windows = 9 * tile
    weights = _nbytes(wl.shape, BF16) + _nbytes(wa.shape, BF16) + _nbytes(wo.shape, BF16)
    scratch = weights + 6 * tile + 2 * _nbytes(ATTN_BIAS_SHAPE, F32)
    return pl.pallas_call(
        _mixers_outproj_kernel,
        grid=(tiles + 1,),
        in_specs=[
            pl.BlockSpec(memory_space=pltpu.SMEM),
            pl.BlockSpec((tm, ATTN_WIDTH), lambda t: (cur(t), COL_Q // ATTN_WIDTH)),
            kc, kp, vc, vp,
            const((N_META, KV_WIDTH)),
            const((N_META, KV_WIDTH)),
            pl.BlockSpec((tm, LRU_WIDTH), lambda t: (cur(t), COL_XB // LRU_WIDTH)),
            prv_spec(LRU_WIDTH, COL_YB // LRU_WIDTH),
            const((SUBLANES, LRU_WIDTH)),
            const((SUBLANES, LANES)),
            *[const(p.shape) for p in lru_params],
            *[prv_spec(GATE_CHUNK, col // GATE_CHUNK + c) for col in (COL_GL, COL_GA) for c in range(N_GATE_CHUNKS)],
            prv_spec(D_MODEL),
            const((1, D_MODEL)),
            resident((LRU_WIDTH, D_MODEL)),
            resident((ATTN_WIDTH, D_MODEL)),
            resident((D_MODEL, D_MODEL)),
        ],
        out_specs=[prv_spec(D_MODEL), prv_spec(D_MODEL), const((SUBLANES, LANES))],
        out_shape=[jax.ShapeDtypeStruct((m, D_MODEL), F32), jax.ShapeDtypeStruct((m, D_MODEL), BF16),
                   jax.ShapeDtypeStruct((SUBLANES, LANES), F32)],
        scratch_shapes=[
            pltpu.VMEM(ATTN_BIAS_SHAPE, F32),
            pltpu.VMEM((tm, ATTN_WIDTH), BF16),
            pltpu.VMEM((tm, LRU_WIDTH), BF16),
            pltpu.VMEM((tm + SUBLANES, LRU_WIDTH), F32),
            pltpu.VMEM(scan_shape, F32),
            pltpu.VMEM(scan_shape, F32),
            pltpu.VMEM(scan_shape, F32),
            pltpu.VMEM((SUBLANES, LANES), F32),
        ],
        compiler_params=_cparams(("arbitrary",), windows, scratch),
        name="mixers_outproj",
    )(sinks, proj, proj, proj, proj, proj, k_meta, v_meta, proj, proj, tail, h0, *lru_params,
      *([proj] * (2 * N_GATE_CHUNKS)), x, g2, wl, wa, wo)


def _outproj(lru, attn, proj, x, g2, wl, wa, wo):
    m = x.shape[0]
    tm = _tile_plan(m)["outproj"]
    const = lambda shape: pl.BlockSpec(shape, lambda i: (0,) * len(shape))
    tile = _nbytes((tm, D_MODEL), F32)
    windows = 6 * tile + _nbytes(wl.shape, BF16) + _nbytes(wa.shape, BF16) + _nbytes(wo.shape, BF16)
    row_spec = lambda width: pl.BlockSpec((tm, width), lambda i: (i, 0))
    return pl.pallas_call(
        _outproj_kernel,
        grid=(m // tm,),
        in_specs=[
            row_spec(LRU_WIDTH),
            row_spec(ATTN_WIDTH),
            *[pl.BlockSpec((tm, GATE_CHUNK), functools.partial(lambda i, cb: (i, cb), cb=col // GATE_CHUNK + c))
              for col in (COL_GL, COL_GA) for c in range(N_GATE_CHUNKS)],
            row_spec(D_MODEL),
            const((1, D_MODEL)),
            const((LRU_WIDTH, D_MODEL)),
            const((ATTN_WIDTH, D_MODEL)),
            const((D_MODEL, D_MODEL)),
        ],
        out_specs=[row_spec(D_MODEL), row_spec(D_MODEL)],
        out_shape=[jax.ShapeDtypeStruct((m, D_MODEL), F32), jax.ShapeDtypeStruct((m, D_MODEL), BF16)],
        compiler_params=_cparams(("arbitrary",), windows, 3 * tile),
        name=f"outproj_{m}",
    )(lru, attn, *([proj] * (2 * N_GATE_CHUNKS)), x, g2, wl, wa, wo)


MLP_EPILOGUE_ROWS = 256


def _mlp_kernel(h_ref, hn_ref, gf_ref, wu_ref, wd_ref, o_ref):
    f = pl.program_id(1)
    last = pl.num_programs(1) - 1

    def chunk(rows=slice(None)):
        u = jnp.dot(hn_ref[rows, :], wu_ref[...], preferred_element_type=F32)
        u = jnp.square(jnp.maximum(u, 0.0))
        return jnp.dot(u.astype(BF16), wd_ref[...], preferred_element_type=F32)

    @pl.when(f == 0)
    def _():
        o_ref[...] = chunk()

    @pl.when((f > 0) & (f < last))
    def _():
        o_ref[...] += chunk()

    @pl.when(f == last)
    def _():
        step = min(MLP_EPILOGUE_ROWS, o_ref.shape[0])
        for r0 in range(0, o_ref.shape[0], step):
            rows = slice(r0, r0 + step)
            out = h_ref[rows, :] + (o_ref[rows, :] + chunk(rows))
            ms = jnp.mean(out * out, axis=-1, keepdims=True)
            o_ref[rows, :] = out * lax.rsqrt(ms + EPS) * gf_ref[...]


def _mlp(h, hn, gf, wu, wd):
    m = h.shape[0]
    tm, tf = _tile_plan(m)["mlp"]
    windows = (2 * _nbytes((tm, D_MODEL), F32) + _nbytes((tm, D_MODEL), BF16) + 2 * _nbytes((D_MODEL, tf), BF16))
    n_f = D_FF // tf
    return pl.pallas_call(
        _mlp_kernel,
        grid=(m // tm, n_f),
        in_specs=[
            pl.BlockSpec((tm, D_MODEL), lambda i, f: (jnp.where(f == n_f - 1, i, jnp.maximum(i - 1, 0)), 0)),
            pl.BlockSpec((tm, D_MODEL), lambda i, f: (i, 0)),
            pl.BlockSpec((1, D_MODEL), lambda i, f: (0, 0)),
            pl.BlockSpec((D_MODEL, tf), lambda i, f: (0, f)),
            pl.BlockSpec((tf, D_MODEL), lambda i, f: (f, 0)),
        ],
        out_specs=pl.BlockSpec((tm, D_MODEL), lambda i, f: (i, 0)),
        out_shape=jax.ShapeDtypeStruct((m, D_MODEL), F32),
        compiler_params=_cparams(("arbitrary", "arbitrary"), windows, 2 * _nbytes((tm, tf), F32)),
        name=f"mlp_{m}",
    )(h, hn, gf, wu, wd)


def _sample_bias():
    dist = (WINDOW - np.arange(WINDOW)).astype(np.float32)
    slopes = np.asarray(_slopes(), np.float32).reshape(N_KV_HEADS, GROUP, 1)
    return jnp.asarray(-(slopes * dist[None, None, :]))


def kernel(x_prompt, x_sample, cache_meta_k, cache_meta_v, cache_win_k, cache_win_v, state_conv, state_h,
           meta_tokens, norm1_g, w_in, conv_w, conv_b, w_gate_x, b_gate_x, w_gate_a, b_gate_a, lru_a_param,
           attn_sinks, w_lru_out, w_attn_out, w_o, norm2_g, w_mlp_up, w_mlp_down, final_norm_g):
    row = lambda v: v.reshape(1, -1)
    g1, g2, gf = row(norm1_g[0]), row(norm2_g[0]), row(final_norm_g)
    wgx, wga = w_gate_x[0].astype(BF16), w_gate_a[0].astype(BF16)
    cw, cb = conv_w[0], row(conv_b[0])
    bgx, bga, ap = row(b_gate_x[0]), row(b_gate_a[0]), row(lru_a_param[0])
    sinks = attn_sinks[0]

    x_main = x_prompt.reshape(SEQ, D_MODEL)
    x_extra = jnp.concatenate([
        x_sample.reshape(DEC_BATCH, D_MODEL),
        jnp.zeros((EXTRA_ROWS - DEC_BATCH - N_META, D_MODEL), F32),
        meta_tokens.astype(F32)], axis=0)

    proj_e, w_in_b = _inproj(x_extra, g1, w_in[0])
    proj_m, wl, wa, wo, wu, wd = _inproj(
        x_main, g1, w_in_b, (w_lru_out[0], w_attn_out[0], w_o[0], w_mlp_up[0], w_mlp_down[0]))

    lru_e, h_sample, h_meta = _lru_extra(proj_e, state_conv[0].reshape(DEC_BATCH, (CONV_W - 1) * LRU_WIDTH),
                                         state_h[0], cw, cb, wgx, bgx, wga, bga, ap)
    tail = proj_e[EXTRA_ROWS - SUBLANES:, COL_XB:COL_XB + LRU_WIDTH]

    k_meta = proj_e[META_ROW0:, COL_K:COL_K + KV_WIDTH]
    v_meta = proj_e[META_ROW0:, COL_V:COL_V + KV_WIDTH]
    attn_meta = _attn_meta(sinks, proj_e[META_ROW0:, COL_Q:COL_Q + ATTN_WIDTH], k_meta, v_meta)
    k_new = proj_e[:DEC_BATCH, COL_K:COL_K + KV_WIDTH]
    v_new = proj_e[:DEC_BATCH, COL_V:COL_V + KV_WIDTH]
    q4 = proj_e[:DEC_BATCH, COL_Q:COL_Q + ATTN_WIDTH].reshape(DEC_BATCH, N_KV_HEADS, GROUP, HEAD_DIM)
    flat = lambda c, n: c[0].reshape(DEC_BATCH, n, KV_WIDTH)
    window_minor = lambda c: jnp.transpose(c[0], (0, 2, 3, 1))
    window_major = lambda c: jnp.transpose(c, (0, 3, 1, 2))[None]
    attn_s, kw_out, vw_out = _attn_sample(
        q4, flat(cache_meta_k, N_META), window_minor(cache_win_k), k_new,
        flat(cache_meta_v, N_META), window_minor(cache_win_v), v_new,
        _sample_bias(), sinks.reshape(N_KV_HEADS, GROUP, 1), bb=8)
    attn_e = jnp.concatenate([
        attn_s.reshape(DEC_BATCH, ATTN_WIDTH),
        jnp.zeros((EXTRA_ROWS - DEC_BATCH - N_META, ATTN_WIDTH), F32),
        attn_meta], axis=0).astype(BF16)

    res_m, resn_m, h_last = _mixers_outproj(
        sinks, proj_m, k_meta, v_meta, tail, h_meta, (cw, cb, wgx, bgx, wga, bga, ap), x_main, g2, wl, wa, wo)
    res_e, resn_e = _outproj(lru_e, attn_e, proj_e, x_extra, g2, wl, wa, wo)
    y_m = _mlp(res_m, resn_m, gf, wu, wd)
    y_e = _mlp(res_e, resn_e, gf, wu, wd)

    kv5 = lambda a, n: a.reshape(1, -1, n, N_KV_HEADS, HEAD_DIM)
    return (
        y_m.reshape(1, SEQ, D_MODEL),
        y_e[:DEC_BATCH].reshape(DEC_BATCH, 1, D_MODEL),
        kv5(k_meta, N_META), kv5(v_meta, N_META),
        window_major(proj_m[SEQ - WINDOW:, COL_K:COL_K + KV_WIDTH].T.reshape(1, N_KV_HEADS, HEAD_DIM, WINDOW)),
        window_major(proj_m[SEQ - WINDOW:, COL_V:COL_V + KV_WIDTH].T.reshape(1, N_KV_HEADS, HEAD_DIM, WINDOW)),
        proj_m[SEQ - (CONV_W - 1):, COL_XB:COL_XB + LRU_WIDTH].reshape(1, 1, CONV_W - 1, LRU_WIDTH),
        h_last.reshape(1, 1, LRU_WIDTH),
        window_major(kw_out), window_major(vw_out),
        jnp.concatenate([state_conv[0][:, 1:], proj_e[:DEC_BATCH, None, COL_XB:COL_XB + LRU_WIDTH]], axis=1)[None],
        h_sample[None],
    )
```

```python
import functools

import numpy as np
import jax
import jax.numpy as jnp
from jax import lax
from jax.experimental import pallas as pl
from jax.experimental.pallas import tpu as pltpu

D_MODEL = 2048
SEQ = 16384
DEC_BATCH = 128
N_META = 16
LRU_WIDTH = 1024
N_LRU_BLOCKS = 8
LRU_BLOCK = LRU_WIDTH // N_LRU_BLOCKS
CONV_W = 4
LRU_C = 8.0
N_HEADS = 16
N_KV_HEADS = 4
HEAD_DIM = 64
GROUP = N_HEADS // N_KV_HEADS
ATTN_WIDTH = N_HEADS * HEAD_DIM
KV_WIDTH = N_KV_HEADS * HEAD_DIM
WINDOW = 128
BLOCK = 128
D_FF = 4 * D_MODEL
EPS = 1e-6
NEG_INF = -1e30
IN_WIDTH = 2 * LRU_WIDTH + ATTN_WIDTH + 2 * KV_WIDTH + 2 * D_MODEL
COL_XB, COL_YB, COL_Q = 0, LRU_WIDTH, 2 * LRU_WIDTH
COL_K = COL_Q + ATTN_WIDTH
COL_V = COL_K + KV_WIDTH
COL_GL = COL_V + KV_WIDTH
COL_GA = COL_GL + D_MODEL
GATE_CHUNK = 512

EXTRA_ROWS = 256
META_ROW0 = EXTRA_ROWS - N_META
SUBLANES = 8
LANES = 128
MIB = 1024 * 1024

F32 = jnp.float32
BF16 = jnp.bfloat16


def _slopes():
    return [2.0 ** (-8.0 * (h + 1) / N_HEADS) for h in range(N_HEADS)]


V7X_VMEM_REQUEST_CAP = 60 * MIB


def _nbytes(shape, dtype):
    return int(np.prod(shape)) * jnp.dtype(dtype).itemsize


def _cparams(sem, pipelined, resident=0):
    estimate = 2 * pipelined + resident
    limit = min(V7X_VMEM_REQUEST_CAP, estimate + estimate // 4 + 2 * MIB)
    return pltpu.CompilerParams(dimension_semantics=sem, vmem_limit_bytes=limit)


def _tile_plan(rows):
    if rows == EXTRA_ROWS:
        return dict(inproj=(EXTRA_ROWS, 1536), outproj=EXTRA_ROWS, mlp=(EXTRA_ROWS, 2048))
    return dict(inproj=(1024, 1536), lru=512, outproj=256, mlp=(1024, 512))


CAST_STEPS = 64
INPROJ_NORM_ROWS = 256


def _cast_specs(weights, step_of):
    specs, shapes, nbytes = [], [], 0
    for w in weights:
        block = (w.shape[0] // CAST_STEPS, w.shape[1])
        specs.append(pl.BlockSpec(block, lambda *idx: (jnp.minimum(step_of(*idx), CAST_STEPS - 1), 0)))
        shapes.append(jax.ShapeDtypeStruct(w.shape, BF16))
        nbytes += _nbytes(block, F32) + _nbytes(block, BF16)
    return specs, shapes, nbytes


def _cast_chunks(src_refs, dst_refs):
    for src_ref, dst_ref in zip(src_refs, dst_refs):
        dst_ref[...] = src_ref[...].astype(BF16)


def _inproj_kernel(x_ref, g_ref, w_ref, *refs, n_cast, emit_w):
    cast_src, o_ref, cast_dst, xs_ref = refs[:n_cast], refs[n_cast], refs[n_cast + 1:2 * n_cast + 1], refs[-1]

    if emit_w:
        refs[-2][...] = w_ref[...].astype(BF16)
        w_ref = refs[-2]

    @pl.when(pl.program_id(1) == 0)
    def _():
        tm = x_ref.shape[0]
        for r in range(0, tm, INPROJ_NORM_ROWS):
            rows = slice(r, min(r + INPROJ_NORM_ROWS, tm))
            x = x_ref[rows, :]
            ms = jnp.mean(x * x, axis=-1, keepdims=True)
            xn = (x * lax.rsqrt(ms + EPS) * g_ref[...]).astype(BF16)
            xs_ref[rows, :] = xn
            o_ref[rows, :] = jnp.dot(xn, w_ref[...], preferred_element_type=F32)

    @pl.when(pl.program_id(1) != 0)
    def _():
        o_ref[...] = jnp.dot(xs_ref[...], w_ref[...], preferred_element_type=F32)

    _cast_chunks(cast_src, cast_dst)


def _inproj(x, g, w, cast_weights=()):
    m = x.shape[0]
    tm, tn = _tile_plan(m)["inproj"]
    n_col = IN_WIDTH // tn
    emit_w = w.dtype == F32
    assert not emit_w or m == tm, "the bf16 copy of w is written once per column block"
    cast_specs, cast_shapes, cast_bytes = _cast_specs(cast_weights, lambda i, j: i * n_col + j)
    assert not cast_weights or (m // tm) * n_col >= CAST_STEPS
    w_spec = pl.BlockSpec((D_MODEL, tn), lambda i, j: (0, j))
    windows = (_nbytes((tm, D_MODEL), F32) + _nbytes((D_MODEL, tn), w.dtype) + _nbytes((tm, tn), F32) + cast_bytes
               + emit_w * _nbytes((D_MODEL, tn), BF16))
    return pl.pallas_call(
        functools.partial(_inproj_kernel, n_cast=len(cast_weights), emit_w=emit_w),
        grid=(m // tm, n_col),
        in_specs=[
            pl.BlockSpec((tm, D_MODEL), lambda i, j: (i, 0)),
            pl.BlockSpec((1, D_MODEL), lambda i, j: (0, 0)),
            w_spec,
            *cast_specs,
        ],
        out_specs=[pl.BlockSpec((tm, tn), lambda i, j: (i, j)), *cast_specs, *([w_spec] if emit_w else [])],
        out_shape=[jax.ShapeDtypeStruct((m, IN_WIDTH), F32), *cast_shapes,
                   *([jax.ShapeDtypeStruct(w.shape, BF16)] if emit_w else [])],
        scratch_shapes=[pltpu.VMEM((tm, D_MODEL), BF16)],
        compiler_params=_cparams(("arbitrary", "arbitrary"), windows, _nbytes((tm, D_MODEL), BF16)),
        name=f"inproj_{m}",
    )(x, g, w, *cast_weights)


def _gate_ab(xc_n, n, wgx_ref, bgx_ref, wga_ref, bga_ref, ap_ref):
    cols = slice(n * LRU_BLOCK, (n + 1) * LRU_BLOCK)
    xcb = xc_n.astype(BF16)
    gx = jnp.dot(xcb, wgx_ref[n], preferred_element_type=F32) + bgx_ref[:, cols]
    ga = jnp.dot(xcb, wga_ref[n], preferred_element_type=F32) + bga_ref[:, cols]
    gate_x = jax.nn.sigmoid(gx)
    gate_a = jax.nn.sigmoid(ga)
    log_a = -LRU_C * gate_a * jax.nn.softplus(-ap_ref[:, cols])
    a = jnp.exp(log_a)
    z = -jnp.tanh(log_a) * (a * a + 1.0)
    root = jnp.where(z > 0.0, z * lax.rsqrt(z), 0.0)
    b = root * gate_x * xc_n
    return a, b


def _lru_coeffs(rows, ext_ref, a3_ref, b3_ref, xb, cw_ref, cb_ref, wgx_ref, bgx_ref, wga_ref, bga_ref, ap_ref):
    for n in range(N_LRU_BLOCKS):
        _lru_coeffs_block(n, rows, ext_ref, a3_ref, b3_ref, xb, cw_ref, cb_ref, wgx_ref, bgx_ref, wga_ref, bga_ref,
                          ap_ref)


def _lru_coeffs_block(n, rows, ext_ref, a3_ref, b3_ref, xb, cw_ref, cb_ref, wgx_ref, bgx_ref, wga_ref, bga_ref,
                      ap_ref):
    groups = rows // SUBLANES
    cols = slice(n * LRU_BLOCK, (n + 1) * LRU_BLOCK)
    ext_ref[SUBLANES:SUBLANES + rows, cols] = xb[:, cols]
    x_ext = ext_ref[:, cols]
    xc = x_ext * cw_ref[0:1, cols]
    for t in range(1, CONV_W):
        xc = pltpu.roll(xc, 1, axis=0) + x_ext * cw_ref[t:t + 1, cols]
    xc = xc[SUBLANES:, :] + cb_ref[:, cols]
    a, b = _gate_ab(xc, n, wgx_ref, bgx_ref, wga_ref, bga_ref, ap_ref)
    a3_ref[:, n * SUBLANES:(n + 1) * SUBLANES, :] = a.reshape(groups, SUBLANES, LANES)
    b3_ref[:, n * SUBLANES:(n + 1) * SUBLANES, :] = b.reshape(groups, SUBLANES, LANES)


SCAN_UNROLL = 4


def _lru_scan(a3_ref, b3_ref, h3_ref, h_init, g_lo, g_hi):
    def group_step(i, h):
        for r in range(0, SUBLANES, 2):
            s0, s1 = (pl.ds(r + k, N_LRU_BLOCKS, stride=SUBLANES) for k in range(2))
            a0, b0, a1, b1 = a3_ref[i, s0, :], b3_ref[i, s0, :], a3_ref[i, s1, :], b3_ref[i, s1, :]
            h3_ref[i, s0, :] = a0 * h + b0
            h = (a1 * a0) * h + (a1 * b0 + b1)
            h3_ref[i, s1, :] = h
        return h

    return lax.fori_loop(g_lo, g_hi, group_step, h_init, unroll=SCAN_UNROLL)


def _lru_gate_out(rows, h3_ref, yb_ref, lru_ref, blocks=range(N_LRU_BLOCKS)):
    for n in blocks:
        cols = slice(n * LRU_BLOCK, (n + 1) * LRU_BLOCK)
        hs = h3_ref[:, n * SUBLANES:(n + 1) * SUBLANES, :].reshape(rows, LANES)
        lru_ref[:, cols] = (jax.nn.gelu(yb_ref[:, cols]) * hs).astype(BF16)


def _lru_rows(rows, ext_ref, a3_ref, b3_ref, h3_ref, xb, yb_ref, cw_ref, cb_ref,
              wgx_ref, bgx_ref, wga_ref, bga_ref, ap_ref, lru_ref, h_init, g_lo, g_hi):
    _lru_coeffs(rows, ext_ref, a3_ref, b3_ref, xb, cw_ref, cb_ref, wgx_ref, bgx_ref, wga_ref, bga_ref, ap_ref)
    h = _lru_scan(a3_ref, b3_ref, h3_ref, h_init, g_lo, g_hi)
    _lru_gate_out(rows, h3_ref, yb_ref, lru_ref)
    return h


def _lru_extra_kernel(xb_ref, yb_ref, sconv_ref, sh_ref, cw_ref, cb_ref, wgx_ref, bgx_ref, wga_ref, bga_ref,
                      ap_ref, lru_ref, hs_ref, hmeta_ref, ext_ref, a3_ref, b3_ref, h3_ref):
    nb = DEC_BATCH
    xb = xb_ref[0:nb, :]
    xc = sconv_ref[:, 0:LRU_WIDTH] * cw_ref[0:1, :]
    xc = xc + sconv_ref[:, LRU_WIDTH:2 * LRU_WIDTH] * cw_ref[1:2, :]
    xc = xc + sconv_ref[:, 2 * LRU_WIDTH:3 * LRU_WIDTH] * cw_ref[2:3, :]
    xc = xc + xb * cw_ref[3:4, :]
    xc = xc + cb_ref[...]
    for n in range(N_LRU_BLOCKS):
        cols = slice(n * LRU_BLOCK, (n + 1) * LRU_BLOCK)
        a, b = _gate_ab(xc[:, cols], n, wgx_ref, bgx_ref, wga_ref, bga_ref, ap_ref)
        h = a * sh_ref[:, cols] + b
        hs_ref[:, cols] = h
        lru_ref[0:nb, cols] = (jax.nn.gelu(yb_ref[0:nb, cols]) * h).astype(BF16)
    rows = EXTRA_ROWS - nb
    ext_ref[0:SUBLANES, :] = jnp.zeros((SUBLANES, LRU_WIDTH), F32)
    h3_ref[...] = jnp.zeros(h3_ref.shape, F32)
    first_group = (rows - N_META) // SUBLANES
    hmeta_ref[...] = _lru_rows(rows, ext_ref, a3_ref, b3_ref, h3_ref, xb_ref[nb:, :], yb_ref.at[nb:, :], cw_ref,
                               cb_ref, wgx_ref, bgx_ref, wga_ref, bga_ref, ap_ref, lru_ref.at[nb:, :],
                               jnp.zeros((SUBLANES, LANES), F32), first_group, rows // SUBLANES)


def _lru_extra(proj_e, sconv, sh, cw, cb, wgx, bgx, wga, bga, ap):
    const = lambda shape: pl.BlockSpec(shape, lambda i: (0,) * len(shape))
    rows = EXTRA_ROWS - DEC_BATCH
    scan_shape = (rows // SUBLANES, N_LRU_BLOCKS * SUBLANES, LANES)
    return pl.pallas_call(
        _lru_extra_kernel,
        grid=(1,),
        in_specs=[
            pl.BlockSpec((EXTRA_ROWS, LRU_WIDTH), lambda i: (0, COL_XB // LRU_WIDTH)),
            pl.BlockSpec((EXTRA_ROWS, LRU_WIDTH), lambda i: (0, COL_YB // LRU_WIDTH)),
            const((DEC_BATCH, (CONV_W - 1) * LRU_WIDTH)),
            const((DEC_BATCH, LRU_WIDTH)),
            const((CONV_W, LRU_WIDTH)),
            const((1, LRU_WIDTH)),
            const((N_LRU_BLOCKS, LRU_BLOCK, LRU_BLOCK)),
            const((1, LRU_WIDTH)),
            const((N_LRU_BLOCKS, LRU_BLOCK, LRU_BLOCK)),
            const((1, LRU_WIDTH)),
            const((1, LRU_WIDTH)),
        ],
        out_specs=[
            const((EXTRA_ROWS, LRU_WIDTH)),
            const((DEC_BATCH, LRU_WIDTH)),
            const((SUBLANES, LANES)),
        ],
        out_shape=[
            jax.ShapeDtypeStruct((EXTRA_ROWS, LRU_WIDTH), BF16),
            jax.ShapeDtypeStruct((DEC_BATCH, LRU_WIDTH), F32),
            jax.ShapeDtypeStruct((SUBLANES, LANES), F32),
        ],
        scratch_shapes=[
            pltpu.VMEM((rows + SUBLANES, LRU_WIDTH), F32),
            pltpu.VMEM(scan_shape, F32),
            pltpu.VMEM(scan_shape, F32),
            pltpu.VMEM(scan_shape, F32),
        ],
        compiler_params=_cparams(("arbitrary",), 8 * _nbytes((DEC_BATCH, LRU_WIDTH), F32),
                                 8 * _nbytes((rows, LRU_WIDTH), F32)),
        name="lru_extra",
    )(proj_e, proj_e, sconv, sh, cw, cb, wgx, bgx, wga, bga, ap)


def _attend_kv_head(q_h, kk, vv, bias, sinks):
    return _attend_values(*_attend_weights(q_h, kk, bias, sinks), vv)


def _attend_weights(q_h, kk, bias, sinks):
    r = q_h.shape[0]
    qs = jnp.concatenate([q_h[:, g * HEAD_DIM:(g + 1) * HEAD_DIM] for g in range(GROUP)], axis=0)
    qs = (qs * (HEAD_DIM ** -0.5)).astype(BF16)
    s = lax.dot_general(kk.astype(BF16), qs, (((1,), (1,)), ((), ())), preferred_element_type=F32) + bias
    sink = jnp.concatenate([jnp.full((1, r), sinks[g], F32) for g in range(GROUP)], axis=1)
    m = jnp.maximum(jnp.max(s, axis=0, keepdims=True), sink)
    e = jnp.exp(s - m)
    denom = jnp.sum(e, axis=0, keepdims=True) + jnp.exp(sink - m)
    return e.astype(BF16), denom


def _attend_values(e, denom, vv):
    r = e.shape[1] // GROUP
    o_t = jnp.dot(vv.T.astype(BF16), e, preferred_element_type=F32) / denom
    pairs = [jnp.concatenate([o_t[:, g * r:(g + 1) * r], o_t[:, (g + 1) * r:(g + 2) * r]], axis=0).T
             for g in range(0, GROUP, 2)]
    return jnp.concatenate(pairs, axis=-1)


ATTN_BLOCKS_PER_STEP = 2
MIXER_TILE = ATTN_BLOCKS_PER_STEP * BLOCK
ATTN_QUERIES = BLOCK // 2
ATTN_BAND = WINDOW + ATTN_QUERIES
ATTN_KEYS = N_META + ATTN_BAND
ATTN_TABLES = WINDOW // ATTN_QUERIES + 1


ATTN_BIAS_SHAPE = (ATTN_TABLES, N_KV_HEADS, ATTN_KEYS, GROUP * ATTN_QUERIES)


def _build_attn_bias(bias_ref):
    slopes = _slopes()
    key = lax.broadcasted_iota(jnp.int32, (ATTN_KEYS, ATTN_QUERIES), 0)
    qry = lax.broadcasted_iota(jnp.int32, (ATTN_KEYS, ATTN_QUERIES), 1)
    d = qry + WINDOW - (key - N_META)
    dist = jnp.where(key < N_META, 0, d).astype(F32)
    for j in range(ATTN_TABLES):
        band_ok = (d >= 0) & (d <= WINDOW) & (key >= N_META + WINDOW - j * ATTN_QUERIES)
        valid = (key < N_META) | band_ok
        for hd in range(N_HEADS):
            lanes = slice((hd % GROUP) * ATTN_QUERIES, (hd % GROUP + 1) * ATTN_QUERIES)
            bias_ref[j, hd // GROUP, :, lanes] = jnp.where(valid, -(slopes[hd] * dist), NEG_INF)


def _attend_stages(tile, sink_ref, q_ref, kc_ref, kp_ref, vc_ref, vp_ref, km_ref, vm_ref, bias_ref, o_ref):
    def pair(sub, h):
        rows = slice(sub * ATTN_QUERIES, (sub + 1) * ATTN_QUERIES)
        lo = sub * ATTN_QUERIES - WINDOW
        n_before = max(-lo, 0) // ATTN_QUERIES
        table = jnp.where(tile == 0, ATTN_TABLES - 1 - n_before, ATTN_TABLES - 1) if n_before else ATTN_TABLES - 1
        hc = slice(h * HEAD_DIM, (h + 1) * HEAD_DIM)
        qc = slice(h * GROUP * HEAD_DIM, (h + 1) * GROUP * HEAD_DIM)
        v = {}

        def keys(meta_ref, cur_ref, prev_ref):
            if lo < 0:
                band = [prev_ref[WINDOW + lo:WINDOW, hc], cur_ref[0:lo + ATTN_BAND, hc]]
            else:
                band = [cur_ref[lo:lo + ATTN_BAND, hc]]
            return jnp.concatenate([meta_ref[:, hc], *band], axis=0)

        def weights():
            v["w"] = _attend_weights(q_ref[rows, qc], keys(km_ref, kc_ref, kp_ref), bias_ref[table, h],
                                     [sink_ref[h * GROUP + g] for g in range(GROUP)])

        def values():
            o_ref[rows, qc] = _attend_values(*v["w"], keys(vm_ref, vc_ref, vp_ref)).astype(BF16)

        return weights, values

    return [pair(sub, h) for sub in range(MIXER_TILE // ATTN_QUERIES) for h in range(N_KV_HEADS)]


def _attn_meta_kernel(sink_ref, q_ref, km_ref, vm_ref, o_ref):
    key = lax.broadcasted_iota(jnp.int32, (N_META, GROUP * N_META), 0)
    qry = lax.broadcasted_iota(jnp.int32, (N_META, GROUP * N_META), 1) % N_META
    bias = jnp.where(qry >= key, 0.0, NEG_INF)
    for h in range(N_KV_HEADS):
        hc = slice(h * HEAD_DIM, (h + 1) * HEAD_DIM)
        q_h = q_ref[:, h * GROUP * HEAD_DIM:(h + 1) * GROUP * HEAD_DIM]
        o = _attend_kv_head(q_h, km_ref[:, hc], vm_ref[:, hc], bias,
                            [sink_ref[h * GROUP + g] for g in range(GROUP)])
        o_ref[:, h * GROUP * HEAD_DIM:(h + 1) * GROUP * HEAD_DIM] = o


def _attn_meta(sinks, q_meta, k_meta, v_meta):
    const = lambda shape: pl.BlockSpec(shape, lambda i: (0,) * len(shape))
    return pl.pallas_call(
        _attn_meta_kernel,
        grid=(1,),
        in_specs=[pl.BlockSpec(memory_space=pltpu.SMEM), const((N_META, ATTN_WIDTH)),
                  const((N_META, KV_WIDTH)), const((N_META, KV_WIDTH))],
        out_specs=const((N_META, ATTN_WIDTH)),
        out_shape=jax.ShapeDtypeStruct((N_META, ATTN_WIDTH), F32),
        compiler_params=_cparams(("arbitrary",), 4 * _nbytes((N_META, ATTN_WIDTH), F32)),
        name="attn_meta",
    )(sinks, q_meta, k_meta, v_meta)


def _attn_sample_kernel(q_ref, km_ref, kw_ref, kn_ref, knt_ref, vm_ref, vw_ref, vn_ref, vnt_ref, bias_ref, sink_ref,
                        o_ref, kwo_ref, vwo_ref):
    bb = q_ref.shape[0]
    position = lax.broadcasted_iota(jnp.int32, (HEAD_DIM, WINDOW), 1)
    for src_ref, newt_ref, dst_ref in ((kw_ref, knt_ref, kwo_ref), (vw_ref, vnt_ref, vwo_ref)):
        for h in range(N_KV_HEADS):
            new_t = newt_ref[0, h * HEAD_DIM:(h + 1) * HEAD_DIM, :]
            for j in range(bb):
                dst_ref[j, h] = jnp.where(position == WINDOW - 1, new_t[:, j:j + 1],
                                          pltpu.roll(src_ref[j, h], WINDOW - 1, axis=1))
    dot_qk = functools.partial(jnp.einsum, "bgd,bjd->bgj", preferred_element_type=F32)
    dot_qkt = functools.partial(jnp.einsum, "bgd,bdj->bgj", preferred_element_type=F32)
    dot_pv = functools.partial(jnp.einsum, "bgj,bjd->bgd", preferred_element_type=F32)
    dot_pvt = functools.partial(jnp.einsum, "bgj,bdj->bgd", preferred_element_type=F32)
    for h in range(N_KV_HEADS):
        hc = slice(h * HEAD_DIM, (h + 1) * HEAD_DIM)
        q = (q_ref[:, h] * (HEAD_DIM ** -0.5)).astype(BF16)
        k_new = kn_ref[:, :, hc].astype(BF16).astype(F32)
        v_new = vn_ref[:, :, hc].astype(BF16).astype(F32)
        s_m = dot_qk(q, km_ref[:, :, hc].astype(BF16))
        s_w = dot_qkt(q, kw_ref[:, h].astype(BF16)) + bias_ref[h]
        s_n = jnp.sum(q.astype(F32) * k_new, axis=-1, keepdims=True)
        sink = sink_ref[h]
        m = jnp.maximum(jnp.maximum(jnp.max(s_m, axis=-1, keepdims=True), jnp.max(s_w, axis=-1, keepdims=True)),
                        jnp.maximum(s_n, sink))
        e_m, e_w, e_n = jnp.exp(s_m - m), jnp.exp(s_w - m), jnp.exp(s_n - m)
        denom = (jnp.sum(e_m, axis=-1, keepdims=True) + jnp.sum(e_w, axis=-1, keepdims=True) + e_n
                 + jnp.exp(sink - m))
        o = (dot_pv(e_m.astype(BF16), vm_ref[:, :, hc].astype(BF16))
             + dot_pvt(e_w.astype(BF16), vw_ref[:, h].astype(BF16))
             + e_n.astype(BF16).astype(F32) * v_new)
        o_ref[:, h] = o / denom


def _attn_sample(q4, k_meta, k_win, k_new, v_meta, v_win, v_new, bias, sinks4, bb):
    b3 = lambda rows: pl.BlockSpec((bb, rows, KV_WIDTH), lambda i: (i, 0, 0))
    q_spec = pl.BlockSpec((bb, N_KV_HEADS, GROUP, HEAD_DIM), lambda i: (i, 0, 0, 0))
    win_spec = pl.BlockSpec((bb, N_KV_HEADS, HEAD_DIM, WINDOW), lambda i: (i, 0, 0, 0))
    new_t_spec = pl.BlockSpec((1, KV_WIDTH, bb), lambda i: (i, 0, 0))
    new_t = lambda new: jnp.transpose(new.reshape(DEC_BATCH // bb, bb, KV_WIDTH), (0, 2, 1))
    win_shape = jax.ShapeDtypeStruct((DEC_BATCH, N_KV_HEADS, HEAD_DIM, WINDOW), F32)
    return pl.pallas_call(
        _attn_sample_kernel,
        grid=(DEC_BATCH // bb,),
        in_specs=[
            q_spec,
            b3(N_META), win_spec, b3(1), new_t_spec,
            b3(N_META), win_spec, b3(1), new_t_spec,
            pl.BlockSpec((N_KV_HEADS, GROUP, WINDOW), lambda i: (0, 0, 0)),
            pl.BlockSpec((N_KV_HEADS, GROUP, 1), lambda i: (0, 0, 0)),
        ],
        out_specs=[q_spec, win_spec, win_spec],
        out_shape=[jax.ShapeDtypeStruct((DEC_BATCH, N_KV_HEADS, GROUP, HEAD_DIM), F32), win_shape, win_shape],
        compiler_params=_cparams(("arbitrary",), 5 * _nbytes((bb, WINDOW, KV_WIDTH), F32),
                                 4 * _nbytes((bb, WINDOW, KV_WIDTH), F32)),
        name="attn_sample",
    )(q4, k_meta, k_win, k_new[:, None, :], new_t(k_new), v_meta, v_win, v_new[:, None, :], new_t(v_new), bias, sinks4)


N_GATE_CHUNKS = D_MODEL // GATE_CHUNK
OUTPROJ_BRANCH_PARTS = 4
OUTPROJ_CHUNK = D_MODEL // (2 * OUTPROJ_BRANCH_PARTS)


def _outproj_stages(lru_ref, attn_ref, gl_refs, ga_refs, x_ref, g2_ref, wl_ref, wa_ref, wo_ref, h_ref, hn_ref):
    v = {"a": [], "l": [], "ss": 0.0}
    part_cols = D_MODEL // OUTPROJ_BRANCH_PARTS

    def branch_proj(key, src_ref, w_ref, part):
        if part == 0:
            v[key + "_in"] = src_ref[...]
        cols = slice(part * part_cols, (part + 1) * part_cols)
        v[key].append(jnp.dot(v[key + "_in"], w_ref[:, cols], preferred_element_type=F32))

    def merge():
        a, l = (jnp.concatenate(v[key], axis=1) for key in ("a", "l"))
        v["m"] = jnp.concatenate(
            [jax.nn.sigmoid(gl_refs[c][...]) * l[:, c * GATE_CHUNK:(c + 1) * GATE_CHUNK]
             + jax.nn.sigmoid(ga_refs[c][...]) * a[:, c * GATE_CHUNK:(c + 1) * GATE_CHUNK]
             for c in range(N_GATE_CHUNKS)], axis=1).astype(BF16)

    def out_chunk(c):
        cols = slice(c * OUTPROJ_CHUNK, (c + 1) * OUTPROJ_CHUNK)
        h = x_ref[:, cols] + jnp.dot(v["m"], wo_ref[:, cols], preferred_element_type=F32)
        h_ref[:, cols] = h
        v["ss"] = v["ss"] + jnp.sum(h * h, axis=-1, keepdims=True)

    def norm():
        scale = lax.rsqrt(v["ss"] * (1.0 / D_MODEL) + EPS)
        hn_ref[...] = (h_ref[...] * scale * g2_ref[...]).astype(BF16)

    def seq(*stages):
        return lambda: [stage() for stage in stages]

    out_chunks = [functools.partial(out_chunk, c) for c in range(D_MODEL // OUTPROJ_CHUNK)]
    return [*[functools.partial(branch_proj, "a", attn_ref, wa_ref, part) for part in range(OUTPROJ_BRANCH_PARTS)],
            *[functools.partial(branch_proj, "l", lru_ref, wl_ref, part) for part in range(OUTPROJ_BRANCH_PARTS)],
            seq(merge, out_chunks[0]), *out_chunks[1:-1], seq(out_chunks[-1], norm)]


def _run_interleaved(project, attend, before=()):
    for i in range(max(len(project), len(attend), len(before))):
        if i < len(attend):
            attend[i][0]()
        if i < len(before):
            before[i]()
        if i < len(project):
            project[i]()
        if i < len(attend):
            attend[i][1]()


def _outproj_kernel(lru_ref, attn_ref, *refs):
    n = 2 * N_GATE_CHUNKS
    tile_refs = (refs[:N_GATE_CHUNKS], refs[N_GATE_CHUNKS:n], *refs[n:n + 5])
    _run_interleaved(_outproj_stages(lru_ref, attn_ref, *tile_refs, *refs[n + 5:]), [])


N_MIXER_INPUTS = 19


def _mixers_outproj_kernel(*refs):
    (sink_ref, q_ref, kc_ref, kp_ref, vc_ref, vp_ref, km_ref, vm_ref,
     xb_ref, yb_ref, tail_ref, h0_ref, *lru_params) = refs[:N_MIXER_INPUTS]
    n = 2 * N_GATE_CHUNKS
    rest = refs[N_MIXER_INPUTS:]
    tile_refs = (rest[:N_GATE_CHUNKS], rest[N_GATE_CHUNKS:n], *rest[n:n + 5])
    h_ref, hn_ref, hlast_ref, bias_ref, attn_ref, lru_ref, ext_ref, a3_ref, b3_ref, h3_ref, hstate_ref = rest[n + 5:]
    rows = MIXER_TILE
    t = pl.program_id(0)
    last = pl.num_programs(0) - 1

    attend = lambda: _attend_stages(t, sink_ref, q_ref, kc_ref, kp_ref, vc_ref, vp_ref, km_ref, vm_ref,
                                    bias_ref, attn_ref)
    project = lambda: _outproj_stages(lru_ref, attn_ref, *tile_refs, h_ref, hn_ref)

    def coeffs(n_block):
        def stage():
            _lru_coeffs_block(n_block, rows, ext_ref, a3_ref, b3_ref, xb_ref, *lru_params)
            cols = slice(n_block * LRU_BLOCK, (n_block + 1) * LRU_BLOCK)
            ext_ref[0:SUBLANES, cols] = ext_ref[rows:rows + SUBLANES, cols]
        return stage

    def gate_out(blocks):
        return lambda: _lru_gate_out(rows, h3_ref, yb_ref, lru_ref, blocks)

    def seq(*stages):
        return lambda: [stage() for stage in stages]

    def scan():
        h = _lru_scan(a3_ref, b3_ref, h3_ref, hstate_ref[...], 0, rows // SUBLANES)
        hstate_ref[...] = h
        hlast_ref[...] = h

    half = N_LRU_BLOCKS // 2
    all_coeffs = [coeffs(n_block) for n_block in range(N_LRU_BLOCKS)]

    @pl.when(t == 0)
    def _():
        _build_attn_bias(bias_ref)
        ext_ref[0:SUBLANES, :] = tail_ref[...]
        hstate_ref[...] = h0_ref[...]
        _run_interleaved([], attend(), all_coeffs)
        scan()

    @pl.when((t > 0) & (t < last))
    def _():
        before = [seq(gate_out(range(half)), all_coeffs[0]), gate_out(range(half, N_LRU_BLOCKS))]
        for stage in all_coeffs[1:]:
            before += [stage, seq()]
        _run_interleaved(project(), attend(), before)
        scan()

    @pl.when(t == last)
    def _():
        gate_out(range(N_LRU_BLOCKS))()
        _run_interleaved(project(), [])


def _mixers_outproj(sinks, proj, k_meta, v_meta, tail, h0, lru_params, x, g2, wl, wa, wo):
    m = x.shape[0]
    tm = MIXER_TILE
    tiles = m // tm
    cur = lambda t: jnp.minimum(t, tiles - 1)
    prv = lambda t: jnp.maximum(t - 1, 0)
    const = lambda shape: pl.BlockSpec(shape, lambda t: (0,) * len(shape))
    resident = lambda shape: pl.BlockSpec(shape, lambda t: (0,) * len(shape), pipeline_mode=pl.Buffered(1))
    kv = lambda col: (pl.BlockSpec((tm, KV_WIDTH), lambda t: (cur(t), col // KV_WIDTH)),
                      pl.BlockSpec((BLOCK, KV_WIDTH),
                                   lambda t: (jnp.maximum(ATTN_BLOCKS_PER_STEP * cur(t) - 1, 0), col // KV_WIDTH)))
    kc, kp = kv(COL_K)
    vc, vp = kv(COL_V)
    prv_spec = lambda width, cb=0: pl.BlockSpec((tm, width), lambda t: (prv(t), cb))
    tile = _nbytes((tm, D_MODEL), F32)
    scan_shape = (tm // SUBLANES, N_LRU_BLOCKS * SUBLANES, LANES)
    windows = 9 * tile
    weights = _nbytes(wl.shape, BF16) + _nbytes(wa.shape, BF16) + _nbytes(wo.shape, BF16)
    scratch = weights + 6 * tile + 2 * _nbytes(ATTN_BIAS_SHAPE, F32)
    return pl.pallas_call(
        _mixers_outproj_kernel,
        grid=(tiles + 1,),
        in_specs=[
            pl.BlockSpec(memory_space=pltpu.SMEM),
            pl.BlockSpec((tm, ATTN_WIDTH), lambda t: (cur(t), COL_Q // ATTN_WIDTH)),
            kc, kp, vc, vp,
            const((N_META, KV_WIDTH)),
            const((N_META, KV_WIDTH)),
            pl.BlockSpec((tm, LRU_WIDTH), lambda t: (cur(t), COL_XB // LRU_WIDTH)),
            prv_spec(LRU_WIDTH, COL_YB // LRU_WIDTH),
            const((SUBLANES, LRU_WIDTH)),
            const((SUBLANES, LANES)),
            *[const(p.shape) for p in lru_params],
            *[prv_spec(GATE_CHUNK, col // GATE_CHUNK + c) for col in (COL_GL, COL_GA) for c in range(N_GATE_CHUNKS)],
            prv_spec(D_MODEL),
            const((1, D_MODEL)),
            resident((LRU_WIDTH, D_MODEL)),
            resident((ATTN_WIDTH, D_MODEL)),
            resident((D_MODEL, D_MODEL)),
        ],
        out_specs=[prv_spec(D_MODEL), prv_spec(D_MODEL), const((SUBLANES, LANES))],
        out_shape=[jax.ShapeDtypeStruct((m, D_MODEL), F32), jax.ShapeDtypeStruct((m, D_MODEL), BF16),
                   jax.ShapeDtypeStruct((SUBLANES, LANES), F32)],
        scratch_shapes=[
            pltpu.VMEM(ATTN_BIAS_SHAPE, F32),
            pltpu.VMEM((tm, ATTN_WIDTH), BF16),
            pltpu.VMEM((tm, LRU_WIDTH), BF16),
            pltpu.VMEM((tm + SUBLANES, LRU_WIDTH), F32),
            pltpu.VMEM(scan_shape, F32),
            pltpu.VMEM(scan_shape, F32),
            pltpu.VMEM(scan_shape, F32),
            pltpu.VMEM((SUBLANES, LANES), F32),
        ],
        compiler_params=_cparams(("arbitrary",), windows, scratch),
        name="mixers_outproj",
    )(sinks, proj, proj, proj, proj, proj, k_meta, v_meta, proj, proj, tail, h0, *lru_params,
      *([proj] * (2 * N_GATE_CHUNKS)), x, g2, wl, wa, wo)


def _outproj(lru, attn, proj, x, g2, wl, wa, wo):
    m = x.shape[0]
    tm = _tile_plan(m)["outproj"]
    const = lambda shape: pl.BlockSpec(shape, lambda i: (0,) * len(shape))
    tile = _nbytes((tm, D_MODEL), F32)
    windows = 6 * tile + _nbytes(wl.shape, BF16) + _nbytes(wa.shape, BF16) + _nbytes(wo.shape, BF16)
    row_spec = lambda width: pl.BlockSpec((tm, width), lambda i: (i, 0))
    return pl.pallas_call(
        _outproj_kernel,
        grid=(m // tm,),
        in_specs=[
            row_spec(LRU_WIDTH),
            row_spec(ATTN_WIDTH),
            *[pl.BlockSpec((tm, GATE_CHUNK), functools.partial(lambda i, cb: (i, cb), cb=col // GATE_CHUNK + c))
              for col in (COL_GL, COL_GA) for c in range(N_GATE_CHUNKS)],
            row_spec(D_MODEL),
            const((1, D_MODEL)),
            const((LRU_WIDTH, D_MODEL)),
            const((ATTN_WIDTH, D_MODEL)),
            const((D_MODEL, D_MODEL)),
        ],
        out_specs=[row_spec(D_MODEL), row_spec(D_MODEL)],
        out_shape=[jax.ShapeDtypeStruct((m, D_MODEL), F32), jax.ShapeDtypeStruct((m, D_MODEL), BF16)],
        compiler_params=_cparams(("arbitrary",), windows, 3 * tile),
        name=f"outproj_{m}",
    )(lru, attn, *([proj] * (2 * N_GATE_CHUNKS)), x, g2, wl, wa, wo)


MLP_EPILOGUE_ROWS = 256


def _mlp_kernel(h_ref, hn_ref, gf_ref, wu_ref, wd_ref, o_ref):
    f = pl.program_id(1)
    last = pl.num_programs(1) - 1

    def chunk(rows=slice(None)):
        u = jnp.dot(hn_ref[rows, :], wu_ref[...], preferred_element_type=F32)
        u = jnp.square(jnp.maximum(u, 0.0))
        return jnp.dot(u.astype(BF16), wd_ref[...], preferred_element_type=F32)

    @pl.when(f == 0)
    def _():
        o_ref[...] = chunk()

    @pl.when((f > 0) & (f < last))
    def _():
        o_ref[...] += chunk()

    @pl.when(f == last)
    def _():
        step = min(MLP_EPILOGUE_ROWS, o_ref.shape[0])
        for r0 in range(0, o_ref.shape[0], step):
            rows = slice(r0, r0 + step)
            out = h_ref[rows, :] + (o_ref[rows, :] + chunk(rows))
            ms = jnp.mean(out * out, axis=-1, keepdims=True)
            o_ref[rows, :] = out * lax.rsqrt(ms + EPS) * gf_ref[...]


def _mlp(h, hn, gf, wu, wd):
    m = h.shape[0]
    tm, tf = _tile_plan(m)["mlp"]
    windows = (2 * _nbytes((tm, D_MODEL), F32) + _nbytes((tm, D_MODEL), BF16) + 2 * _nbytes((D_MODEL, tf), BF16))
    n_f = D_FF // tf
    return pl.pallas_call(
        _mlp_kernel,
        grid=(m // tm, n_f),
        in_specs=[
            pl.BlockSpec((tm, D_MODEL), lambda i, f: (jnp.where(f == n_f - 1, i, jnp.maximum(i - 1, 0)), 0)),
            pl.BlockSpec((tm, D_MODEL), lambda i, f: (i, 0)),
            pl.BlockSpec((1, D_MODEL), lambda i, f: (0, 0)),
            pl.BlockSpec((D_MODEL, tf), lambda i, f: (0, f)),
            pl.BlockSpec((tf, D_MODEL), lambda i, f: (f, 0)),
        ],
        out_specs=pl.BlockSpec((tm, D_MODEL), lambda i, f: (i, 0)),
        out_shape=jax.ShapeDtypeStruct((m, D_MODEL), F32),
        compiler_params=_cparams(("arbitrary", "arbitrary"), windows, 2 * _nbytes((tm, tf), F32)),
        name=f"mlp_{m}",
    )(h, hn, gf, wu, wd)


def _sample_bias():
    dist = (WINDOW - np.arange(WINDOW)).astype(np.float32)
    slopes = np.asarray(_slopes(), np.float32).reshape(N_KV_HEADS, GROUP, 1)
    return jnp.asarray(-(slopes * dist[None, None, :]))


def kernel(x_prompt, x_sample, cache_meta_k, cache_meta_v, cache_win_k, cache_win_v, state_conv, state_h,
           meta_tokens, norm1_g, w_in, conv_w, conv_b, w_gate_x, b_gate_x, w_gate_a, b_gate_a, lru_a_param,
           attn_sinks, w_lru_out, w_attn_out, w_o, norm2_g, w_mlp_up, w_mlp_down, final_norm_g):
    row = lambda v: v.reshape(1, -1)
    g1, g2, gf = row(norm1_g[0]), row(norm2_g[0]), row(final_norm_g)
    wgx, wga = w_gate_x[0].astype(BF16), w_gate_a[0].astype(BF16)
    cw, cb = conv_w[0], row(conv_b[0])
    bgx, bga, ap = row(b_gate_x[0]), row(b_gate_a[0]), row(lru_a_param[0])
    sinks = attn_sinks[0]

    x_main = x_prompt.reshape(SEQ, D_MODEL)
    x_extra = jnp.concatenate([
        x_sample.reshape(DEC_BATCH, D_MODEL),
        jnp.zeros((EXTRA_ROWS - DEC_BATCH - N_META, D_MODEL), F32),
        meta_tokens.astype(F32)], axis=0)

    proj_e, w_in_b = _inproj(x_extra, g1, w_in[0])
    proj_m, wl, wa, wo, wu, wd = _inproj(
        x_main, g1, w_in_b, (w_lru_out[0], w_attn_out[0], w_o[0], w_mlp_up[0], w_mlp_down[0]))

    lru_e, h_sample, h_meta = _lru_extra(proj_e, state_conv[0].reshape(DEC_BATCH, (CONV_W - 1) * LRU_WIDTH),
                                         state_h[0], cw, cb, wgx, bgx, wga, bga, ap)
    tail = proj_e[EXTRA_ROWS - SUBLANES:, COL_XB:COL_XB + LRU_WIDTH]

    k_meta = proj_e[META_ROW0:, COL_K:COL_K + KV_WIDTH]
    v_meta = proj_e[META_ROW0:, COL_V:COL_V + KV_WIDTH]
    attn_meta = _attn_meta(sinks, proj_e[META_ROW0:, COL_Q:COL_Q + ATTN_WIDTH], k_meta, v_meta)
    k_new = proj_e[:DEC_BATCH, COL_K:COL_K + KV_WIDTH]
    v_new = proj_e[:DEC_BATCH, COL_V:COL_V + KV_WIDTH]
    q4 = proj_e[:DEC_BATCH, COL_Q:COL_Q + ATTN_WIDTH].reshape(DEC_BATCH, N_KV_HEADS, GROUP, HEAD_DIM)
    flat = lambda c, n: c[0].reshape(DEC_BATCH, n, KV_WIDTH)
    window_minor = lambda c: jnp.transpose(c[0], (0, 2, 3, 1))
    window_major = lambda c: jnp.transpose(c, (0, 3, 1, 2))[None]
    attn_s, kw_out, vw_out = _attn_sample(
        q4, flat(cache_meta_k, N_META), window_minor(cache_win_k), k_new,
        flat(cache_meta_v, N_META), window_minor(cache_win_v), v_new,
        _sample_bias(), sinks.reshape(N_KV_HEADS, GROUP, 1), bb=8)
    attn_e = jnp.concatenate([
        attn_s.reshape(DEC_BATCH, ATTN_WIDTH),
        jnp.zeros((EXTRA_ROWS - DEC_BATCH - N_META, ATTN_WIDTH), F32),
        attn_meta], axis=0).astype(BF16)

    res_m, resn_m, h_last = _mixers_outproj(
        sinks, proj_m, k_meta, v_meta, tail, h_meta, (cw, cb, wgx, bgx, wga, bga, ap), x_main, g2, wl, wa, wo)
    res_e, resn_e = _outproj(lru_e, attn_e, proj_e, x_extra, g2, wl, wa, wo)
    y_m = _mlp(res_m, resn_m, gf, wu, wd)
    y_e = _mlp(res_e, resn_e, gf, wu, wd)

    kv5 = lambda a, n: a.reshape(1, -1, n, N_KV_HEADS, HEAD_DIM)
    return (
        y_m.reshape(1, SEQ, D_MODEL),
        y_e[:DEC_BATCH].reshape(DEC_BATCH, 1, D_MODEL),
        kv5(k_meta, N_META), kv5(v_meta, N_META),
        kv5(proj_m[SEQ - WINDOW:, COL_K:COL_K + KV_WIDTH], WINDOW),
        kv5(proj_m[SEQ - WINDOW:, COL_V:COL_V + KV_WIDTH], WINDOW),
        proj_m[SEQ - (CONV_W - 1):, COL_XB:COL_XB + LRU_WIDTH].reshape(1, 1, CONV_W - 1, LRU_WIDTH),
        h_last.reshape(1, 1, LRU_WIDTH),
        window_major(kw_out), window_major(vw_out),
        jnp.concatenate([state_conv[0][:, 1:], proj_e[:DEC_BATCH, None, COL_XB:COL_XB + LRU_WIDTH]], axis=1)[None],
        h_sample[None],
    )
```

```python
import functools

import numpy as np
import jax
import jax.numpy as jnp
from jax import lax
from jax.experimental import pallas as pl
from jax.experimental.pallas import tpu as pltpu

D_MODEL = 2048
SEQ = 16384
DEC_BATCH = 128
N_META = 16
LRU_WIDTH = 1024
N_LRU_BLOCKS = 8
LRU_BLOCK = LRU_WIDTH // N_LRU_BLOCKS
CONV_W = 4
LRU_C = 8.0
N_HEADS = 16
N_KV_HEADS = 4
HEAD_DIM = 64
GROUP = N_HEADS // N_KV_HEADS
ATTN_WIDTH = N_HEADS * HEAD_DIM
KV_WIDTH = N_KV_HEADS * HEAD_DIM
WINDOW = 128
BLOCK = 128
D_FF = 4 * D_MODEL
EPS = 1e-6
NEG_INF = -1e30
IN_WIDTH = 2 * LRU_WIDTH + ATTN_WIDTH + 2 * KV_WIDTH + 2 * D_MODEL
COL_XB, COL_YB, COL_Q = 0, LRU_WIDTH, 2 * LRU_WIDTH
COL_K = COL_Q + ATTN_WIDTH
COL_V = COL_K + KV_WIDTH
COL_GL = COL_V + KV_WIDTH
COL_GA = COL_GL + D_MODEL
GATE_CHUNK = 512

EXTRA_ROWS = 256
META_ROW0 = EXTRA_ROWS - N_META
SUBLANES = 8
LANES = 128
MIB = 1024 * 1024

F32 = jnp.float32
BF16 = jnp.bfloat16


def _slopes():
    return [2.0 ** (-8.0 * (h + 1) / N_HEADS) for h in range(N_HEADS)]


V7X_VMEM_REQUEST_CAP = 60 * MIB


def _nbytes(shape, dtype):
    return int(np.prod(shape)) * jnp.dtype(dtype).itemsize


def _cparams(sem, pipelined, resident=0):
    estimate = 2 * pipelined + resident
    limit = min(V7X_VMEM_REQUEST_CAP, estimate + estimate // 4 + 2 * MIB)
    return pltpu.CompilerParams(dimension_semantics=sem, vmem_limit_bytes=limit)


def _tile_plan(rows):
    if rows == EXTRA_ROWS:
        return dict(inproj=(EXTRA_ROWS, 1536), outproj=EXTRA_ROWS, mlp=(EXTRA_ROWS, 2048))
    return dict(inproj=(1024, 1536), lru=512, outproj=256, mlp=(1024, 512))


CAST_STEPS = 64
INPROJ_NORM_ROWS = 128


def _cast_specs(weights, step_of):
    specs, shapes, nbytes = [], [], 0
    for w in weights:
        block = (w.shape[0] // CAST_STEPS, w.shape[1])
        specs.append(pl.BlockSpec(block, lambda *idx: (jnp.minimum(step_of(*idx), CAST_STEPS - 1), 0)))
        shapes.append(jax.ShapeDtypeStruct(w.shape, BF16))
        nbytes += _nbytes(block, F32) + _nbytes(block, BF16)
    return specs, shapes, nbytes


def _cast_chunks(src_refs, dst_refs):
    for src_ref, dst_ref in zip(src_refs, dst_refs):
        dst_ref[...] = src_ref[...].astype(BF16)


def _inproj_kernel(x_ref, g_ref, w_ref, *refs, n_cast, emit_w):
    cast_src, o_ref, cast_dst, xs_ref = refs[:n_cast], refs[n_cast], refs[n_cast + 1:2 * n_cast + 1], refs[-1]

    if emit_w:
        refs[-2][...] = w_ref[...].astype(BF16)
        w_ref = refs[-2]

    @pl.when(pl.program_id(1) == 0)
    def _():
        tm = x_ref.shape[0]
        for r in range(0, tm, INPROJ_NORM_ROWS):
            rows = slice(r, min(r + INPROJ_NORM_ROWS, tm))
            x = x_ref[rows, :]
            ms = jnp.mean(x * x, axis=-1, keepdims=True)
            xn = (x * lax.rsqrt(ms + EPS) * g_ref[...]).astype(BF16)
            xs_ref[rows, :] = xn
            o_ref[rows, :] = jnp.dot(xn, w_ref[...], preferred_element_type=F32)

    @pl.when(pl.program_id(1) != 0)
    def _():
        o_ref[...] = jnp.dot(xs_ref[...], w_ref[...], preferred_element_type=F32)

    _cast_chunks(cast_src, cast_dst)


def _inproj(x, g, w, cast_weights=()):
    m = x.shape[0]
    tm, tn = _tile_plan(m)["inproj"]
    n_col = IN_WIDTH // tn
    emit_w = w.dtype == F32
    assert not emit_w or m == tm, "the bf16 copy of w is written once per column block"
    cast_specs, cast_shapes, cast_bytes = _cast_specs(cast_weights, lambda i, j: i * n_col + j)
    assert not cast_weights or (m // tm) * n_col >= CAST_STEPS
    w_spec = pl.BlockSpec((D_MODEL, tn), lambda i, j: (0, j))
    windows = (_nbytes((tm, D_MODEL), F32) + _nbytes((D_MODEL, tn), w.dtype) + _nbytes((tm, tn), F32) + cast_bytes
               + emit_w * _nbytes((D_MODEL, tn), BF16))
    return pl.pallas_call(
        functools.partial(_inproj_kernel, n_cast=len(cast_weights), emit_w=emit_w),
        grid=(m // tm, n_col),
        in_specs=[
            pl.BlockSpec((tm, D_MODEL), lambda i, j: (i, 0)),
            pl.BlockSpec((1, D_MODEL), lambda i, j: (0, 0)),
            w_spec,
            *cast_specs,
        ],
        out_specs=[pl.BlockSpec((tm, tn), lambda i, j: (i, j)), *cast_specs, *([w_spec] if emit_w else [])],
        out_shape=[jax.ShapeDtypeStruct((m, IN_WIDTH), F32), *cast_shapes,
                   *([jax.ShapeDtypeStruct(w.shape, BF16)] if emit_w else [])],
        scratch_shapes=[pltpu.VMEM((tm, D_MODEL), BF16)],
        compiler_params=_cparams(("arbitrary", "arbitrary"), windows, _nbytes((tm, D_MODEL), BF16)),
        name=f"inproj_{m}",
    )(x, g, w, *cast_weights)


def _gate_ab(xc_n, n, wgx_ref, bgx_ref, wga_ref, bga_ref, ap_ref):
    cols = slice(n * LRU_BLOCK, (n + 1) * LRU_BLOCK)
    xcb = xc_n.astype(BF16)
    gx = jnp.dot(xcb, wgx_ref[n], preferred_element_type=F32) + bgx_ref[:, cols]
    ga = jnp.dot(xcb, wga_ref[n], preferred_element_type=F32) + bga_ref[:, cols]
    gate_x = jax.nn.sigmoid(gx)
    gate_a = jax.nn.sigmoid(ga)
    log_a = -LRU_C * gate_a * jax.nn.softplus(-ap_ref[:, cols])
    a = jnp.exp(log_a)
    z = -jnp.tanh(log_a) * (a * a + 1.0)
    root = jnp.where(z > 0.0, z * lax.rsqrt(z), 0.0)
    b = root * gate_x * xc_n
    return a, b


def _lru_coeffs(rows, ext_ref, a3_ref, b3_ref, xb, cw_ref, cb_ref, wgx_ref, bgx_ref, wga_ref, bga_ref, ap_ref):
    for n in range(N_LRU_BLOCKS):
        _lru_coeffs_block(n, rows, ext_ref, a3_ref, b3_ref, xb, cw_ref, cb_ref, wgx_ref, bgx_ref, wga_ref, bga_ref,
                          ap_ref)


def _lru_coeffs_block(n, rows, ext_ref, a3_ref, b3_ref, xb, cw_ref, cb_ref, wgx_ref, bgx_ref, wga_ref, bga_ref,
                      ap_ref):
    groups = rows // SUBLANES
    cols = slice(n * LRU_BLOCK, (n + 1) * LRU_BLOCK)
    ext_ref[SUBLANES:SUBLANES + rows, cols] = xb[:, cols]
    x_ext = ext_ref[:, cols]
    xc = x_ext * cw_ref[0:1, cols]
    for t in range(1, CONV_W):
        xc = pltpu.roll(xc, 1, axis=0) + x_ext * cw_ref[t:t + 1, cols]
    xc = xc[SUBLANES:, :] + cb_ref[:, cols]
    a, b = _gate_ab(xc, n, wgx_ref, bgx_ref, wga_ref, bga_ref, ap_ref)
    a3_ref[:, n * SUBLANES:(n + 1) * SUBLANES, :] = a.reshape(groups, SUBLANES, LANES)
    b3_ref[:, n * SUBLANES:(n + 1) * SUBLANES, :] = b.reshape(groups, SUBLANES, LANES)


SCAN_UNROLL = 4


def _lru_scan(a3_ref, b3_ref, h3_ref, h_init, g_lo, g_hi):
    def group_step(i, h):
        for r in range(0, SUBLANES, 2):
            s0, s1 = (pl.ds(r + k, N_LRU_BLOCKS, stride=SUBLANES) for k in range(2))
            a0, b0, a1, b1 = a3_ref[i, s0, :], b3_ref[i, s0, :], a3_ref[i, s1, :], b3_ref[i, s1, :]
            h3_ref[i, s0, :] = a0 * h + b0
            h = (a1 * a0) * h + (a1 * b0 + b1)
            h3_ref[i, s1, :] = h
        return h

    return lax.fori_loop(g_lo, g_hi, group_step, h_init, unroll=SCAN_UNROLL)


def _lru_gate_out(rows, h3_ref, yb_ref, lru_ref, blocks=range(N_LRU_BLOCKS)):
    for n in blocks:
        cols = slice(n * LRU_BLOCK, (n + 1) * LRU_BLOCK)
        hs = h3_ref[:, n * SUBLANES:(n + 1) * SUBLANES, :].reshape(rows, LANES)
        lru_ref[:, cols] = (jax.nn.gelu(yb_ref[:, cols]) * hs).astype(BF16)


def _lru_rows(rows, ext_ref, a3_ref, b3_ref, h3_ref, xb, yb_ref, cw_ref, cb_ref,
              wgx_ref, bgx_ref, wga_ref, bga_ref, ap_ref, lru_ref, h_init, g_lo, g_hi):
    _lru_coeffs(rows, ext_ref, a3_ref, b3_ref, xb, cw_ref, cb_ref, wgx_ref, bgx_ref, wga_ref, bga_ref, ap_ref)
    h = _lru_scan(a3_ref, b3_ref, h3_ref, h_init, g_lo, g_hi)
    _lru_gate_out(rows, h3_ref, yb_ref, lru_ref)
    return h


def _lru_extra_kernel(xb_ref, yb_ref, sconv_ref, sh_ref, cw_ref, cb_ref, wgx_ref, bgx_ref, wga_ref, bga_ref,
                      ap_ref, lru_ref, hs_ref, hmeta_ref, ext_ref, a3_ref, b3_ref, h3_ref):
    nb = DEC_BATCH
    xb = xb_ref[0:nb, :]
    xc = sconv_ref[:, 0:LRU_WIDTH] * cw_ref[0:1, :]
    xc = xc + sconv_ref[:, LRU_WIDTH:2 * LRU_WIDTH] * cw_ref[1:2, :]
    xc = xc + sconv_ref[:, 2 * LRU_WIDTH:3 * LRU_WIDTH] * cw_ref[2:3, :]
    xc = xc + xb * cw_ref[3:4, :]
    xc = xc + cb_ref[...]
    for n in range(N_LRU_BLOCKS):
        cols = slice(n * LRU_BLOCK, (n + 1) * LRU_BLOCK)
        a, b = _gate_ab(xc[:, cols], n, wgx_ref, bgx_ref, wga_ref, bga_ref, ap_ref)
        h = a * sh_ref[:, cols] + b
        hs_ref[:, cols] = h
        lru_ref[0:nb, cols] = (jax.nn.gelu(yb_ref[0:nb, cols]) * h).astype(BF16)
    rows = EXTRA_ROWS - nb
    ext_ref[0:SUBLANES, :] = jnp.zeros((SUBLANES, LRU_WIDTH), F32)
    h3_ref[...] = jnp.zeros(h3_ref.shape, F32)
    first_group = (rows - N_META) // SUBLANES
    hmeta_ref[...] = _lru_rows(rows, ext_ref, a3_ref, b3_ref, h3_ref, xb_ref[nb:, :], yb_ref.at[nb:, :], cw_ref,
                               cb_ref, wgx_ref, bgx_ref, wga_ref, bga_ref, ap_ref, lru_ref.at[nb:, :],
                               jnp.zeros((SUBLANES, LANES), F32), first_group, rows // SUBLANES)


def _lru_extra(proj_e, sconv, sh, cw, cb, wgx, bgx, wga, bga, ap):
    const = lambda shape: pl.BlockSpec(shape, lambda i: (0,) * len(shape))
    rows = EXTRA_ROWS - DEC_BATCH
    scan_shape = (rows // SUBLANES, N_LRU_BLOCKS * SUBLANES, LANES)
    return pl.pallas_call(
        _lru_extra_kernel,
        grid=(1,),
        in_specs=[
            pl.BlockSpec((EXTRA_ROWS, LRU_WIDTH), lambda i: (0, COL_XB // LRU_WIDTH)),
            pl.BlockSpec((EXTRA_ROWS, LRU_WIDTH), lambda i: (0, COL_YB // LRU_WIDTH)),
            const((DEC_BATCH, (CONV_W - 1) * LRU_WIDTH)),
            const((DEC_BATCH, LRU_WIDTH)),
            const((CONV_W, LRU_WIDTH)),
            const((1, LRU_WIDTH)),
            const((N_LRU_BLOCKS, LRU_BLOCK, LRU_BLOCK)),
            const((1, LRU_WIDTH)),
            const((N_LRU_BLOCKS, LRU_BLOCK, LRU_BLOCK)),
            const((1, LRU_WIDTH)),
            const((1, LRU_WIDTH)),
        ],
        out_specs=[
            const((EXTRA_ROWS, LRU_WIDTH)),
            const((DEC_BATCH, LRU_WIDTH)),
            const((SUBLANES, LANES)),
        ],
        out_shape=[
            jax.ShapeDtypeStruct((EXTRA_ROWS, LRU_WIDTH), BF16),
            jax.ShapeDtypeStruct((DEC_BATCH, LRU_WIDTH), F32),
            jax.ShapeDtypeStruct((SUBLANES, LANES), F32),
        ],
        scratch_shapes=[
            pltpu.VMEM((rows + SUBLANES, LRU_WIDTH), F32),
            pltpu.VMEM(scan_shape, F32),
            pltpu.VMEM(scan_shape, F32),
            pltpu.VMEM(scan_shape, F32),
        ],
        compiler_params=_cparams(("arbitrary",), 8 * _nbytes((DEC_BATCH, LRU_WIDTH), F32),
                                 8 * _nbytes((rows, LRU_WIDTH), F32)),
        name="lru_extra",
    )(proj_e, proj_e, sconv, sh, cw, cb, wgx, bgx, wga, bga, ap)


def _attend_kv_head(q_h, kk, vv, bias, sinks):
    return _attend_values(*_attend_weights(q_h, kk, bias, sinks), vv)


def _attend_weights(q_h, kk, bias, sinks):
    r = q_h.shape[0]
    qs = jnp.concatenate([q_h[:, g * HEAD_DIM:(g + 1) * HEAD_DIM] for g in range(GROUP)], axis=0)
    qs = (qs * (HEAD_DIM ** -0.5)).astype(BF16)
    s = lax.dot_general(kk.astype(BF16), qs, (((1,), (1,)), ((), ())), preferred_element_type=F32) + bias
    sink = jnp.concatenate([jnp.full((1, r), sinks[g], F32) for g in range(GROUP)], axis=1)
    m = jnp.maximum(jnp.max(s, axis=0, keepdims=True), sink)
    e = jnp.exp(s - m)
    denom = jnp.sum(e, axis=0, keepdims=True) + jnp.exp(sink - m)
    return e.astype(BF16), denom


def _attend_values(e, denom, vv):
    r = e.shape[1] // GROUP
    o_t = jnp.dot(vv.T.astype(BF16), e, preferred_element_type=F32) / denom
    pairs = [jnp.concatenate([o_t[:, g * r:(g + 1) * r], o_t[:, (g + 1) * r:(g + 2) * r]], axis=0).T
             for g in range(0, GROUP, 2)]
    return jnp.concatenate(pairs, axis=-1)


ATTN_BLOCKS_PER_STEP = 2
MIXER_TILE = ATTN_BLOCKS_PER_STEP * BLOCK
ATTN_QUERIES = BLOCK // 2
ATTN_BAND = WINDOW + ATTN_QUERIES
ATTN_KEYS = N_META + ATTN_BAND
ATTN_TABLES = WINDOW // ATTN_QUERIES + 1


ATTN_BIAS_SHAPE = (ATTN_TABLES, N_KV_HEADS, ATTN_KEYS, GROUP * ATTN_QUERIES)


def _build_attn_bias(bias_ref):
    slopes = _slopes()
    key = lax.broadcasted_iota(jnp.int32, (ATTN_KEYS, ATTN_QUERIES), 0)
    qry = lax.broadcasted_iota(jnp.int32, (ATTN_KEYS, ATTN_QUERIES), 1)
    d = qry + WINDOW - (key - N_META)
    dist = jnp.where(key < N_META, 0, d).astype(F32)
    for j in range(ATTN_TABLES):
        band_ok = (d >= 0) & (d <= WINDOW) & (key >= N_META + WINDOW - j * ATTN_QUERIES)
        valid = (key < N_META) | band_ok
        for hd in range(N_HEADS):
            lanes = slice((hd % GROUP) * ATTN_QUERIES, (hd % GROUP + 1) * ATTN_QUERIES)
            bias_ref[j, hd // GROUP, :, lanes] = jnp.where(valid, -(slopes[hd] * dist), NEG_INF)


def _attend_stages(tile, sink_ref, q_ref, kc_ref, kp_ref, vc_ref, vp_ref, km_ref, vm_ref, bias_ref, o_ref):
    def pair(sub, h):
        rows = slice(sub * ATTN_QUERIES, (sub + 1) * ATTN_QUERIES)
        lo = sub * ATTN_QUERIES - WINDOW
        n_before = max(-lo, 0) // ATTN_QUERIES
        table = jnp.where(tile == 0, ATTN_TABLES - 1 - n_before, ATTN_TABLES - 1) if n_before else ATTN_TABLES - 1
        hc = slice(h * HEAD_DIM, (h + 1) * HEAD_DIM)
        qc = slice(h * GROUP * HEAD_DIM, (h + 1) * GROUP * HEAD_DIM)
        v = {}

        def keys(meta_ref, cur_ref, prev_ref):
            if lo < 0:
                band = [prev_ref[WINDOW + lo:WINDOW, hc], cur_ref[0:lo + ATTN_BAND, hc]]
            else:
                band = [cur_ref[lo:lo + ATTN_BAND, hc]]
            return jnp.concatenate([meta_ref[:, hc], *band], axis=0)

        def weights():
            v["w"] = _attend_weights(q_ref[rows, qc], keys(km_ref, kc_ref, kp_ref), bias_ref[table, h],
                                     [sink_ref[h * GROUP + g] for g in range(GROUP)])

        def values():
            o_ref[rows, qc] = _attend_values(*v["w"], keys(vm_ref, vc_ref, vp_ref)).astype(BF16)

        return weights, values

    return [pair(sub, h) for sub in range(MIXER_TILE // ATTN_QUERIES) for h in range(N_KV_HEADS)]


def _attn_meta_kernel(sink_ref, q_ref, km_ref, vm_ref, o_ref):
    key = lax.broadcasted_iota(jnp.int32, (N_META, GROUP * N_META), 0)
    qry = lax.broadcasted_iota(jnp.int32, (N_META, GROUP * N_META), 1) % N_META
    bias = jnp.where(qry >= key, 0.0, NEG_INF)
    for h in range(N_KV_HEADS):
        hc = slice(h * HEAD_DIM, (h + 1) * HEAD_DIM)
        q_h = q_ref[:, h * GROUP * HEAD_DIM:(h + 1) * GROUP * HEAD_DIM]
        o = _attend_kv_head(q_h, km_ref[:, hc], vm_ref[:, hc], bias,
                            [sink_ref[h * GROUP + g] for g in range(GROUP)])
        o_ref[:, h * GROUP * HEAD_DIM:(h + 1) * GROUP * HEAD_DIM] = o


def _attn_meta(sinks, q_meta, k_meta, v_meta):
    const = lambda shape: pl.BlockSpec(shape, lambda i: (0,) * len(shape))
    return pl.pallas_call(
        _attn_meta_kernel,
        grid=(1,),
        in_specs=[pl.BlockSpec(memory_space=pltpu.SMEM), const((N_META, ATTN_WIDTH)),
                  const((N_META, KV_WIDTH)), const((N_META, KV_WIDTH))],
        out_specs=const((N_META, ATTN_WIDTH)),
        out_shape=jax.ShapeDtypeStruct((N_META, ATTN_WIDTH), F32),
        compiler_params=_cparams(("arbitrary",), 4 * _nbytes((N_META, ATTN_WIDTH), F32)),
        name="attn_meta",
    )(sinks, q_meta, k_meta, v_meta)


def _attn_sample_kernel(q_ref, km_ref, kw_ref, kn_ref, knt_ref, vm_ref, vw_ref, vn_ref, vnt_ref, bias_ref, sink_ref,
                        o_ref, kwo_ref, vwo_ref):
    bb = q_ref.shape[0]
    position = lax.broadcasted_iota(jnp.int32, (HEAD_DIM, WINDOW), 1)
    for src_ref, newt_ref, dst_ref in ((kw_ref, knt_ref, kwo_ref), (vw_ref, vnt_ref, vwo_ref)):
        for h in range(N_KV_HEADS):
            new_t = newt_ref[0, h * HEAD_DIM:(h + 1) * HEAD_DIM, :]
            for j in range(bb):
                dst_ref[j, h] = jnp.where(position == WINDOW - 1, new_t[:, j:j + 1],
                                          pltpu.roll(src_ref[j, h], WINDOW - 1, axis=1))
    dot_qk = functools.partial(jnp.einsum, "bgd,bjd->bgj", preferred_element_type=F32)
    dot_qkt = functools.partial(jnp.einsum, "bgd,bdj->bgj", preferred_element_type=F32)
    dot_pv = functools.partial(jnp.einsum, "bgj,bjd->bgd", preferred_element_type=F32)
    dot_pvt = functools.partial(jnp.einsum, "bgj,bdj->bgd", preferred_element_type=F32)
    for h in range(N_KV_HEADS):
        hc = slice(h * HEAD_DIM, (h + 1) * HEAD_DIM)
        q = (q_ref[:, h] * (HEAD_DIM ** -0.5)).astype(BF16)
        k_new = kn_ref[:, :, hc].astype(BF16).astype(F32)
        v_new = vn_ref[:, :, hc].astype(BF16).astype(F32)
        s_m = dot_qk(q, km_ref[:, :, hc].astype(BF16))
        s_w = dot_qkt(q, kw_ref[:, h].astype(BF16)) + bias_ref[h]
        s_n = jnp.sum(q.astype(F32) * k_new, axis=-1, keepdims=True)
        sink = sink_ref[h]
        m = jnp.maximum(jnp.maximum(jnp.max(s_m, axis=-1, keepdims=True), jnp.max(s_w, axis=-1, keepdims=True)),
                        jnp.maximum(s_n, sink))
        e_m, e_w, e_n = jnp.exp(s_m - m), jnp.exp(s_w - m), jnp.exp(s_n - m)
        denom = (jnp.sum(e_m, axis=-1, keepdims=True) + jnp.sum(e_w, axis=-1, keepdims=True) + e_n
                 + jnp.exp(sink - m))
        o = (dot_pv(e_m.astype(BF16), vm_ref[:, :, hc].astype(BF16))
             + dot_pvt(e_w.astype(BF16), vw_ref[:, h].astype(BF16))
             + e_n.astype(BF16).astype(F32) * v_new)
        o_ref[:, h] = o / denom


def _attn_sample(q4, k_meta, k_win, k_new, v_meta, v_win, v_new, bias, sinks4, bb):
    b3 = lambda rows: pl.BlockSpec((bb, rows, KV_WIDTH), lambda i: (i, 0, 0))
    q_spec = pl.BlockSpec((bb, N_KV_HEADS, GROUP, HEAD_DIM), lambda i: (i, 0, 0, 0))
    win_spec = pl.BlockSpec((bb, N_KV_HEADS, HEAD_DIM, WINDOW), lambda i: (i, 0, 0, 0))
    new_t_spec = pl.BlockSpec((1, KV_WIDTH, bb), lambda i: (i, 0, 0))
    new_t = lambda new: jnp.transpose(new.reshape(DEC_BATCH // bb, bb, KV_WIDTH), (0, 2, 1))
    win_shape = jax.ShapeDtypeStruct((DEC_BATCH, N_KV_HEADS, HEAD_DIM, WINDOW), F32)
    return pl.pallas_call(
        _attn_sample_kernel,
        grid=(DEC_BATCH // bb,),
        in_specs=[
            q_spec,
            b3(N_META), win_spec, b3(1), new_t_spec,
            b3(N_META), win_spec, b3(1), new_t_spec,
            pl.BlockSpec((N_KV_HEADS, GROUP, WINDOW), lambda i: (0, 0, 0)),
            pl.BlockSpec((N_KV_HEADS, GROUP, 1), lambda i: (0, 0, 0)),
        ],
        out_specs=[q_spec, win_spec, win_spec],
        out_shape=[jax.ShapeDtypeStruct((DEC_BATCH, N_KV_HEADS, GROUP, HEAD_DIM), F32), win_shape, win_shape],
        compiler_params=_cparams(("arbitrary",), 5 * _nbytes((bb, WINDOW, KV_WIDTH), F32),
                                 4 * _nbytes((bb, WINDOW, KV_WIDTH), F32)),
        name="attn_sample",
    )(q4, k_meta, k_win, k_new[:, None, :], new_t(k_new), v_meta, v_win, v_new[:, None, :], new_t(v_new), bias, sinks4)


N_GATE_CHUNKS = D_MODEL // GATE_CHUNK
OUTPROJ_BRANCH_PARTS = 4
OUTPROJ_CHUNK = D_MODEL // (2 * OUTPROJ_BRANCH_PARTS)


def _outproj_stages(lru_ref, attn_ref, gl_refs, ga_refs, x_ref, g2_ref, wl_ref, wa_ref, wo_ref, h_ref, hn_ref):
    v = {"a": [], "l": [], "ss": 0.0}
    part_cols = D_MODEL // OUTPROJ_BRANCH_PARTS

    def branch_proj(key, src_ref, w_ref, part):
        if part == 0:
            v[key + "_in"] = src_ref[...]
        cols = slice(part * part_cols, (part + 1) * part_cols)
        v[key].append(jnp.dot(v[key + "_in"], w_ref[:, cols], preferred_element_type=F32))

    def merge():
        a, l = (jnp.concatenate(v[key], axis=1) for key in ("a", "l"))
        v["m"] = jnp.concatenate(
            [jax.nn.sigmoid(gl_refs[c][...]) * l[:, c * GATE_CHUNK:(c + 1) * GATE_CHUNK]
             + jax.nn.sigmoid(ga_refs[c][...]) * a[:, c * GATE_CHUNK:(c + 1) * GATE_CHUNK]
             for c in range(N_GATE_CHUNKS)], axis=1).astype(BF16)

    def out_chunk(c):
        cols = slice(c * OUTPROJ_CHUNK, (c + 1) * OUTPROJ_CHUNK)
        h = x_ref[:, cols] + jnp.dot(v["m"], wo_ref[:, cols], preferred_element_type=F32)
        h_ref[:, cols] = h
        v["ss"] = v["ss"] + jnp.sum(h * h, axis=-1, keepdims=True)

    def norm():
        scale = lax.rsqrt(v["ss"] * (1.0 / D_MODEL) + EPS)
        hn_ref[...] = (h_ref[...] * scale * g2_ref[...]).astype(BF16)

    def seq(*stages):
        return lambda: [stage() for stage in stages]

    out_chunks = [functools.partial(out_chunk, c) for c in range(D_MODEL // OUTPROJ_CHUNK)]
    return [*[functools.partial(branch_proj, "a", attn_ref, wa_ref, part) for part in range(OUTPROJ_BRANCH_PARTS)],
            *[functools.partial(branch_proj, "l", lru_ref, wl_ref, part) for part in range(OUTPROJ_BRANCH_PARTS)],
            seq(merge, out_chunks[0]), *out_chunks[1:-1], seq(out_chunks[-1], norm)]


def _run_interleaved(project, attend, before=()):
    for i in range(max(len(project), len(attend), len(before))):
        if i < len(attend):
            attend[i][0]()
        if i < len(before):
            before[i]()
        if i < len(project):
            project[i]()
        if i < len(attend):
            attend[i][1]()


def _outproj_kernel(lru_ref, attn_ref, *refs):
    n = 2 * N_GATE_CHUNKS
    tile_refs = (refs[:N_GATE_CHUNKS], refs[N_GATE_CHUNKS:n], *refs[n:n + 5])
    _run_interleaved(_outproj_stages(lru_ref, attn_ref, *tile_refs, *refs[n + 5:]), [])


N_MIXER_INPUTS = 19


def _mixers_outproj_kernel(*refs):
    (sink_ref, q_ref, kc_ref, kp_ref, vc_ref, vp_ref, km_ref, vm_ref,
     xb_ref, yb_ref, tail_ref, h0_ref, *lru_params) = refs[:N_MIXER_INPUTS]
    n = 2 * N_GATE_CHUNKS
    rest = refs[N_MIXER_INPUTS:]
    tile_refs = (rest[:N_GATE_CHUNKS], rest[N_GATE_CHUNKS:n], *rest[n:n + 5])
    h_ref, hn_ref, hlast_ref, bias_ref, attn_ref, lru_ref, ext_ref, a3_ref, b3_ref, h3_ref, hstate_ref = rest[n + 5:]
    rows = MIXER_TILE
    t = pl.program_id(0)
    last = pl.num_programs(0) - 1

    attend = lambda: _attend_stages(t, sink_ref, q_ref, kc_ref, kp_ref, vc_ref, vp_ref, km_ref, vm_ref,
                                    bias_ref, attn_ref)
    project = lambda: _outproj_stages(lru_ref, attn_ref, *tile_refs, h_ref, hn_ref)

    def coeffs(n_block):
        def stage():
            _lru_coeffs_block(n_block, rows, ext_ref, a3_ref, b3_ref, xb_ref, *lru_params)
            cols = slice(n_block * LRU_BLOCK, (n_block + 1) * LRU_BLOCK)
            ext_ref[0:SUBLANES, cols] = ext_ref[rows:rows + SUBLANES, cols]
        return stage

    def gate_out(blocks):
        return lambda: _lru_gate_out(rows, h3_ref, yb_ref, lru_ref, blocks)

    def seq(*stages):
        return lambda: [stage() for stage in stages]

    def scan():
        h = _lru_scan(a3_ref, b3_ref, h3_ref, hstate_ref[...], 0, rows // SUBLANES)
        hstate_ref[...] = h
        hlast_ref[...] = h

    half = N_LRU_BLOCKS // 2
    all_coeffs = [coeffs(n_block) for n_block in range(N_LRU_BLOCKS)]

    @pl.when(t == 0)
    def _():
        _build_attn_bias(bias_ref)
        ext_ref[0:SUBLANES, :] = tail_ref[...]
        hstate_ref[...] = h0_ref[...]
        _run_interleaved([], attend(), all_coeffs)
        scan()

    @pl.when((t > 0) & (t < last))
    def _():
        before = [seq(gate_out(range(half)), all_coeffs[0]), gate_out(range(half, N_LRU_BLOCKS))]
        for stage in all_coeffs[1:]:
            before += [stage, seq()]
        _run_interleaved(project(), attend(), before)
        scan()

    @pl.when(t == last)
    def _():
        gate_out(range(N_LRU_BLOCKS))()
        _run_interleaved(project(), [])


def _mixers_outproj(sinks, proj, k_meta, v_meta, tail, h0, lru_params, x, g2, wl, wa, wo):
    m = x.shape[0]
    tm = MIXER_TILE
    tiles = m // tm
    cur = lambda t: jnp.minimum(t, tiles - 1)
    prv = lambda t: jnp.maximum(t - 1, 0)
    const = lambda shape: pl.BlockSpec(shape, lambda t: (0,) * len(shape))
    resident = lambda shape: pl.BlockSpec(shape, lambda t: (0,) * len(shape), pipeline_mode=pl.Buffered(1))
    kv = lambda col: (pl.BlockSpec((tm, KV_WIDTH), lambda t: (cur(t), col // KV_WIDTH)),
                      pl.BlockSpec((BLOCK, KV_WIDTH),
                                   lambda t: (jnp.maximum(ATTN_BLOCKS_PER_STEP * cur(t) - 1, 0), col // KV_WIDTH)))
    kc, kp = kv(COL_K)
    vc, vp = kv(COL_V)
    prv_spec = lambda width, cb=0: pl.BlockSpec((tm, width), lambda t: (prv(t), cb))
    tile = _nbytes((tm, D_MODEL), F32)
    scan_shape = (tm // SUBLANES, N_LRU_BLOCKS * SUBLANES, LANES)
    windows = 9 * tile
    weights = _nbytes(wl.shape, BF16) + _nbytes(wa.shape, BF16) + _nbytes(wo.shape, BF16)
    scratch = weights + 6 * tile + 2 * _nbytes(ATTN_BIAS_SHAPE, F32)
    return pl.pallas_call(
        _mixers_outproj_kernel,
        grid=(tiles + 1,),
        in_specs=[
            pl.BlockSpec(memory_space=pltpu.SMEM),
            pl.BlockSpec((tm, ATTN_WIDTH), lambda t: (cur(t), COL_Q // ATTN_WIDTH)),
            kc, kp, vc, vp,
            const((N_META, KV_WIDTH)),
            const((N_META, KV_WIDTH)),
            pl.BlockSpec((tm, LRU_WIDTH), lambda t: (cur(t), COL_XB // LRU_WIDTH)),
            prv_spec(LRU_WIDTH, COL_YB // LRU_WIDTH),
            const((SUBLANES, LRU_WIDTH)),
            const((SUBLANES, LANES)),
            *[const(p.shape) for p in lru_params],
            *[prv_spec(GATE_CHUNK, col // GATE_CHUNK + c) for col in (COL_GL, COL_GA) for c in range(N_GATE_CHUNKS)],
            prv_spec(D_MODEL),
            const((1, D_MODEL)),
            resident((LRU_WIDTH, D_MODEL)),
            resident((ATTN_WIDTH, D_MODEL)),
            resident((D_MODEL, D_MODEL)),
        ],
        out_specs=[prv_spec(D_MODEL), prv_spec(D_MODEL), const((SUBLANES, LANES))],
        out_shape=[jax.ShapeDtypeStruct((m, D_MODEL), F32), jax.ShapeDtypeStruct((m, D_MODEL), BF16),
                   jax.ShapeDtypeStruct((SUBLANES, LANES), F32)],
        scratch_shapes=[
            pltpu.VMEM(ATTN_BIAS_SHAPE, F32),
            pltpu.VMEM((tm, ATTN_WIDTH), BF16),
            pltpu.VMEM((tm, LRU_WIDTH), BF16),
            pltpu.VMEM((tm + SUBLANES, LRU_WIDTH), F32),
            pltpu.VMEM(scan_shape, F32),
            pltpu.VMEM(scan_shape, F32),
            pltpu.VMEM(scan_shape, F32),
            pltpu.VMEM((SUBLANES, LANES), F32),
        ],
        compiler_params=_cparams(("arbitrary",), windows, scratch),
        name="mixers_outproj",
    )(sinks, proj, proj, proj, proj, proj, k_meta, v_meta, proj, proj, tail, h0, *lru_params,
      *([proj] * (2 * N_GATE_CHUNKS)), x, g2, wl, wa, wo)


def _outproj(lru, attn, proj, x, g2, wl, wa, wo):
    m = x.shape[0]
    tm = _tile_plan(m)["outproj"]
    const = lambda shape: pl.BlockSpec(shape, lambda i: (0,) * len(shape))
    tile = _nbytes((tm, D_MODEL), F32)
    windows = 6 * tile + _nbytes(wl.shape, BF16) + _nbytes(wa.shape, BF16) + _nbytes(wo.shape, BF16)
    row_spec = lambda width: pl.BlockSpec((tm, width), lambda i: (i, 0))
    return pl.pallas_call(
        _outproj_kernel,
        grid=(m // tm,),
        in_specs=[
            row_spec(LRU_WIDTH),
            row_spec(ATTN_WIDTH),
            *[pl.BlockSpec((tm, GATE_CHUNK), functools.partial(lambda i, cb: (i, cb), cb=col // GATE_CHUNK + c))
              for col in (COL_GL, COL_GA) for c in range(N_GATE_CHUNKS)],
            row_spec(D_MODEL),
            const((1, D_MODEL)),
            const((LRU_WIDTH, D_MODEL)),
            const((ATTN_WIDTH, D_MODEL)),
            const((D_MODEL, D_MODEL)),
        ],
        out_specs=[row_spec(D_MODEL), row_spec(D_MODEL)],
        out_shape=[jax.ShapeDtypeStruct((m, D_MODEL), F32), jax.ShapeDtypeStruct((m, D_MODEL), BF16)],
        compiler_params=_cparams(("arbitrary",), windows, 3 * tile),
        name=f"outproj_{m}",
    )(lru, attn, *([proj] * (2 * N_GATE_CHUNKS)), x, g2, wl, wa, wo)


MLP_EPILOGUE_ROWS = 256


def _mlp_kernel(h_ref, hn_ref, gf_ref, wu_ref, wd_ref, o_ref):
    f = pl.program_id(1)
    last = pl.num_programs(1) - 1

    def chunk(rows=slice(None)):
        u = jnp.dot(hn_ref[rows, :], wu_ref[...], preferred_element_type=F32)
        u = jnp.square(jnp.maximum(u, 0.0))
        return jnp.dot(u.astype(BF16), wd_ref[...], preferred_element_type=F32)

    @pl.when(f == 0)
    def _():
        o_ref[...] = chunk()

    @pl.when((f > 0) & (f < last))
    def _():
        o_ref[...] += chunk()

    @pl.when(f == last)
    def _():
        step = min(MLP_EPILOGUE_ROWS, o_ref.shape[0])
        for r0 in range(0, o_ref.shape[0], step):
            rows = slice(r0, r0 + step)
            out = h_ref[rows, :] + (o_ref[rows, :] + chunk(rows))
            ms = jnp.mean(out * out, axis=-1, keepdims=True)
            o_ref[rows, :] = out * lax.rsqrt(ms + EPS) * gf_ref[...]


def _mlp(h, hn, gf, wu, wd):
    m = h.shape[0]
    tm, tf = _tile_plan(m)["mlp"]
    windows = (2 * _nbytes((tm, D_MODEL), F32) + _nbytes((tm, D_MODEL), BF16) + 2 * _nbytes((D_MODEL, tf), BF16))
    n_f = D_FF // tf
    return pl.pallas_call(
        _mlp_kernel,
        grid=(m // tm, n_f),
        in_specs=[
            pl.BlockSpec((tm, D_MODEL), lambda i, f: (jnp.where(f == n_f - 1, i, jnp.maximum(i - 1, 0)), 0)),
            pl.BlockSpec((tm, D_MODEL), lambda i, f: (i, 0)),
            pl.BlockSpec((1, D_MODEL), lambda i, f: (0, 0)),
            pl.BlockSpec((D_MODEL, tf), lambda i, f: (0, f)),
            pl.BlockSpec((tf, D_MODEL), lambda i, f: (f, 0)),
        ],
        out_specs=pl.BlockSpec((tm, D_MODEL), lambda i, f: (i, 0)),
        out_shape=jax.ShapeDtypeStruct((m, D_MODEL), F32),
        compiler_params=_cparams(("arbitrary", "arbitrary"), windows, 2 * _nbytes((tm, tf), F32)),
        name=f"mlp_{m}",
    )(h, hn, gf, wu, wd)


def _sample_bias():
    dist = (WINDOW - np.arange(WINDOW)).astype(np.float32)
    slopes = np.asarray(_slopes(), np.float32).reshape(N_KV_HEADS, GROUP, 1)
    return jnp.asarray(-(slopes * dist[None, None, :]))


def kernel(x_prompt, x_sample, cache_meta_k, cache_meta_v, cache_win_k, cache_win_v, state_conv, state_h,
           meta_tokens, norm1_g, w_in, conv_w, conv_b, w_gate_x, b_gate_x, w_gate_a, b_gate_a, lru_a_param,
           attn_sinks, w_lru_out, w_attn_out, w_o, norm2_g, w_mlp_up, w_mlp_down, final_norm_g):
    row = lambda v: v.reshape(1, -1)
    g1, g2, gf = row(norm1_g[0]), row(norm2_g[0]), row(final_norm_g)
    wgx, wga = w_gate_x[0].astype(BF16), w_gate_a[0].astype(BF16)
    cw, cb = conv_w[0], row(conv_b[0])
    bgx, bga, ap = row(b_gate_x[0]), row(b_gate_a[0]), row(lru_a_param[0])
    sinks = attn_sinks[0]

    x_main = x_prompt.reshape(SEQ, D_MODEL)
    x_extra = jnp.concatenate([
        x_sample.reshape(DEC_BATCH, D_MODEL),
        jnp.zeros((EXTRA_ROWS - DEC_BATCH - N_META, D_MODEL), F32),
        meta_tokens.astype(F32)], axis=0)

    proj_e, w_in_b = _inproj(x_extra, g1, w_in[0])
    proj_m, wl, wa, wo, wu, wd = _inproj(
        x_main, g1, w_in_b, (w_lru_out[0], w_attn_out[0], w_o[0], w_mlp_up[0], w_mlp_down[0]))

    lru_e, h_sample, h_meta = _lru_extra(proj_e, state_conv[0].reshape(DEC_BATCH, (CONV_W - 1) * LRU_WIDTH),
                                         state_h[0], cw, cb, wgx, bgx, wga, bga, ap)
    tail = proj_e[EXTRA_ROWS - SUBLANES:, COL_XB:COL_XB + LRU_WIDTH]

    k_meta = proj_e[META_ROW0:, COL_K:COL_K + KV_WIDTH]
    v_meta = proj_e[META_ROW0:, COL_V:COL_V + KV_WIDTH]
    attn_meta = _attn_meta(sinks, proj_e[META_ROW0:, COL_Q:COL_Q + ATTN_WIDTH], k_meta, v_meta)
    k_new = proj_e[:DEC_BATCH, COL_K:COL_K + KV_WIDTH]
    v_new = proj_e[:DEC_BATCH, COL_V:COL_V + KV_WIDTH]
    q4 = proj_e[:DEC_BATCH, COL_Q:COL_Q + ATTN_WIDTH].reshape(DEC_BATCH, N_KV_HEADS, GROUP, HEAD_DIM)
    flat = lambda c, n: c[0].reshape(DEC_BATCH, n, KV_WIDTH)
    window_minor = lambda c: jnp.transpose(c[0], (0, 2, 3, 1))
    window_major = lambda c: jnp.transpose(c, (0, 3, 1, 2))[None]
    attn_s, kw_out, vw_out = _attn_sample(
        q4, flat(cache_meta_k, N_META), window_minor(cache_win_k), k_new,
        flat(cache_meta_v, N_META), window_minor(cache_win_v), v_new,
        _sample_bias(), sinks.reshape(N_KV_HEADS, GROUP, 1), bb=16)
    attn_e = jnp.concatenate([
        attn_s.reshape(DEC_BATCH, ATTN_WIDTH),
        jnp.zeros((EXTRA_ROWS - DEC_BATCH - N_META, ATTN_WIDTH), F32),
        attn_meta], axis=0).astype(BF16)

    res_m, resn_m, h_last = _mixers_outproj(
        sinks, proj_m, k_meta, v_meta, tail, h_meta, (cw, cb, wgx, bgx, wga, bga, ap), x_main, g2, wl, wa, wo)
    res_e, resn_e = _outproj(lru_e, attn_e, proj_e, x_extra, g2, wl, wa, wo)
    y_m = _mlp(res_m, resn_m, gf, wu, wd)
    y_e = _mlp(res_e, resn_e, gf, wu, wd)

    kv5 = lambda a, n: a.reshape(1, -1, n, N_KV_HEADS, HEAD_DIM)
    return (
        y_m.reshape(1, SEQ, D_MODEL),
        y_e[:DEC_BATCH].reshape(DEC_BATCH, 1, D_MODEL),
        kv5(k_meta, N_META), kv5(v_meta, N_META),
        kv5(proj_m[SEQ - WINDOW:, COL_K:COL_K + KV_WIDTH], WINDOW),
        kv5(proj_m[SEQ - WINDOW:, COL_V:COL_V + KV_WIDTH], WINDOW),
        proj_m[SEQ - (CONV_W - 1):, COL_XB:COL_XB + LRU_WIDTH].reshape(1, 1, CONV_W - 1, LRU_WIDTH),
        h_last.reshape(1, 1, LRU_WIDTH),
        window_major(kw_out), window_major(vw_out),
        jnp.concatenate([state_conv[0][:, 1:], proj_e[:DEC_BATCH, None, COL_XB:COL_XB + LRU_WIDTH]], axis=1)[None],
        h_sample[None],
    )
```
